```python
import math
import jax
import jax.numpy as jnp
from jax import lax
import numpy as np

D_MODEL = 2048
BATCH = 16
SEQ = 256
DEPTH = 2
DEC_BATCH = 8
DEC_SEQ = 1024
PAST_LEN = 512

GRID_W = 64
HEAD_DIM = 64
SHORT_W = 3
ATT_WIDTH = 3 * D_MODEL // 8
N_ATT_HEADS = ATT_WIDTH // HEAD_DIM
N_KV_HEADS = N_ATT_HEADS // 3
ATT_GROUP = N_ATT_HEADS // N_KV_HEADS
ATT_KV = N_KV_HEADS * HEAD_DIM
WINDOW = 128
BLOCK = 128
ROPE_THETA = 10000.0
NEG_INF = -1e30
HY_CH = D_MODEL // 4
HY_ORDER = 2
HY_N_FILT = HY_ORDER - 1
HY_BANDS = 16
HY_EMB = 1 + 2 * HY_BANDS
HY_FFN = 64
HY_MOD_SHIFT = 0.05
HY_DECAY_MIN = -math.log(1e-2) / 1.5
HY_DECAY_MAX = -math.log(1e-2) / 0.3
RWKV_WIDTH = D_MODEL - ATT_WIDTH - HY_CH
N_RWKV_HEADS = RWKV_WIDTH // HEAD_DIM
DECAY_LORA = 96
AAA_LORA = 96
GATE_LORA = 256
RWKV_GN_EPS = 64e-5
MIX_WIDTH = ATT_WIDTH + HY_CH + RWKV_WIDTH
D_FF = 4 * D_MODEL
N_MOD = 6
RMS_EPS = 1e-6
IN_WIDTHS = (ATT_WIDTH, ATT_KV, ATT_KV, (HY_ORDER + 1) * HY_CH, 3 * RWKV_WIDTH,
             2 * DECAY_LORA, 2 * AAA_LORA, GATE_LORA)
IN_COLS = sum(IN_WIDTHS)
IN_SPLITS = tuple(sum(IN_WIDTHS[:i + 1]) for i in range(len(IN_WIDTHS) - 1))

kernel_name = 'hybrid_dit_attn_hyena_rwkv7_step'

f32 = jnp.float32


def _rmsnorm(x, g):
    xf = x.astype(f32)
    y = xf * lax.rsqrt(jnp.mean(xf * xf, axis=-1, keepdims=True) + RMS_EPS)
    return (y * g.astype(f32)).astype(x.dtype)


def _modulation(cond, ada_w, ada_b):
    m = jax.nn.silu(cond) @ ada_w + ada_b
    m = m.reshape(cond.shape[0], 1, N_MOD, D_MODEL)
    return [m[:, :, i] for i in range(N_MOD)]


def _short_conv(x, w):
    xp = jnp.pad(x, ((0, 0), (1, 1), (0, 0)))
    return xp[:, :-2] * w[0] + xp[:, 1:-1] * w[1] + xp[:, 2:] * w[2]


def _rope_2d(x):
    L = x.shape[1]
    n_rows = L // GRID_W
    rows, cols = jnp.meshgrid(jnp.arange(n_rows), jnp.arange(GRID_W), indexing='ij')
    rows = rows.reshape(-1).astype(f32)
    cols = cols.reshape(-1).astype(f32)
    half = x.shape[-1] // 2
    freqs = ROPE_THETA ** (-jnp.arange(0, half, 2, dtype=f32) / half)

    def rot(xh, pos):
        ang = pos[:, None] * freqs[None, :]
        cos = jnp.cos(ang)[None, :, None, :]
        sin = jnp.sin(ang)[None, :, None, :]
        x1, x2 = jnp.split(xh, 2, axis=-1)
        return jnp.concatenate([x1 * cos - x2 * sin, x2 * cos + x1 * sin], axis=-1)

    xf = x.astype(f32)
    return jnp.concatenate([rot(xf[..., :half], rows), rot(xf[..., half:], cols)], axis=-1).astype(x.dtype)


def _attend(q, k, v, mask, sink):
    s = jnp.einsum('bqngd,bknd->bngqk', q.astype(f32), k.astype(f32)) * (HEAD_DIM ** -0.5)
    if mask is not None:
        s = jnp.where(mask, s, NEG_INF)
    sk = jnp.broadcast_to(sink.astype(f32)[None, :, :, None, None], s.shape[:-1] + (1,))
    p = jax.nn.softmax(jnp.concatenate([s, sk], axis=-1), axis=-1)[..., :-1]
    return jnp.einsum('bngqk,bknd->bqngd', p, v.astype(f32))


def _context_attention(q, k, v, sink):
    B, L = q.shape[:2]
    nb = L // BLOCK
    qb = q.reshape(B, nb, BLOCK, N_KV_HEADS, ATT_GROUP, HEAD_DIM).swapaxes(0, 1)
    sink_g = sink.reshape(N_KV_HEADS, ATT_GROUP)
    out = lax.map(lambda qi: _attend(qi, k, v, None, sink_g), qb)
    return out.swapaxes(0, 1).reshape(B, L, ATT_WIDTH)


def _latent_attention(q, k, v, k_ctx, v_ctx, sink):
    B, L = q.shape[:2]
    nb = L // BLOCK
    Lc = k_ctx.shape[1]
    q5 = q.reshape(B, L, N_KV_HEADS, ATT_GROUP, HEAD_DIM)
    pad = ((0, 0), (BLOCK, BLOCK), (0, 0), (0, 0))
    k_pad = jnp.pad(k, pad)
    v_pad = jnp.pad(v, pad)
    sink_g = sink.reshape(N_KV_HEADS, ATT_GROUP)
    ctx_mask = jnp.ones((BLOCK, Lc), dtype=bool)

    def block(i):
        start = i * BLOCK
        qi = lax.dynamic_slice_in_dim(q5, start, BLOCK, axis=1)
        ki = lax.dynamic_slice_in_dim(k_pad, start, 3 * BLOCK, axis=1)
        vi = lax.dynamic_slice_in_dim(v_pad, start, 3 * BLOCK, axis=1)
        qpos = start + jnp.arange(BLOCK)
        kpos = start - BLOCK + jnp.arange(3 * BLOCK)
        local = ((jnp.abs(qpos[:, None] - kpos[None, :]) <= WINDOW)
                 & (kpos >= 0)[None, :] & (kpos < L)[None, :])
        mask = jnp.concatenate([local, ctx_mask], axis=1)
        kk = jnp.concatenate([ki, k_ctx], axis=1)
        vv = jnp.concatenate([vi, v_ctx], axis=1)
        return _attend(qi, kk, vv, mask, sink_g)

    out = lax.map(block, jnp.arange(nb))
    return out.swapaxes(0, 1).reshape(B, L, ATT_WIDTH)


def _hyena_filter_spectrum(L, f1, b1, f2, b2, f3, decay):
    t = jnp.arange(L, dtype=f32)
    t01 = (t / max(L - 1, 1))[:, None]
    bands = jnp.linspace(1e-4, HY_BANDS - 1, HY_BANDS, dtype=f32)
    ang = (2.0 * math.pi / L) * t[:, None] * bands[None, :]
    feat = jnp.concatenate([t01, jnp.cos(ang), -jnp.sin(ang)], axis=-1)
    h = jnp.sin(feat @ f1.astype(f32) + b1.astype(f32))
    h = jnp.sin(h @ f2.astype(f32) + b2.astype(f32))
    h = h @ f3.astype(f32)
    h = h * (jnp.exp(-t01 * jnp.abs(decay.astype(f32))) + HY_MOD_SHIFT)
    h = h.reshape(L, 2, HY_N_FILT, HY_CH)
    fwd = h[:, 0]
    bwd = h[:, 1]
    two_sided = jnp.concatenate([fwd, jnp.zeros((1, HY_N_FILT, HY_CH), f32), bwd[:0:-1]], axis=0)
    return jnp.fft.rfft(two_sided, axis=0)


def _hyena(u, p):
    L = u.shape[1]
    u = _short_conv(u, p['hy_short_w']).astype(f32)
    parts = jnp.split(u, HY_ORDER + 1, axis=-1)
    gates, z = parts[:-1], parts[-1]
    spec = _hyena_filter_spectrum(L, p['hy_f1'], p['hy_b1'], p['hy_f2'], p['hy_b2'], p['hy_f3'], p['hy_decay'])
    for o in range(HY_N_FILT):
        z = z * gates[HY_ORDER - 1 - o]
        zf = jnp.fft.rfft(z, n=2 * L, axis=1)
        y = jnp.fft.irfft(zf * spec[None, :, o], n=2 * L, axis=1)[:, :L]
        z = y + p['hy_skip'][o].astype(f32) * z
    return z * gates[0]


def _rwkv(u_rkv, w_dn, a_dn, g_dn, p, S0):
    B, L, _ = u_rkv.shape
    rkv = _short_conv(u_rkv, p['rw_short_w']).astype(f32)
    r, k, v = jnp.split(rkv, 3, axis=-1)
    w_dn = w_dn.astype(f32).reshape(B, L, 2, DECAY_LORA)
    a_dn = a_dn.astype(f32).reshape(B, L, 2, AAA_LORA)
    w_lo = jnp.einsum('bldr,drc->dblc', jnp.tanh(w_dn), p['rw_w_up'].astype(f32))
    w_log = -jax.nn.softplus(-(p['rw_w0'].astype(f32)[:, None, None, :] + w_lo)) - 0.5
    decay = jnp.exp(-jnp.exp(w_log))
    a = jax.nn.sigmoid(p['rw_a0'].astype(f32)[:, None, None, :]
                       + jnp.einsum('bldr,drc->dblc', a_dn, p['rw_a_up'].astype(f32)))
    g = jax.nn.sigmoid(g_dn.astype(f32)) @ p['rw_g_up'].astype(f32)

    def hsplit(t):
        return t.reshape(t.shape[:-1] + (N_RWKV_HEADS, HEAD_DIM))

    kk = hsplit(k * p['rw_k_k'].astype(f32))
    kk = kk / jnp.maximum(jnp.sqrt(jnp.sum(kk * kk, axis=-1, keepdims=True)), 1e-12)
    k_dir = hsplit(k[None] * (1.0 + (a - 1.0) * p['rw_k_a'].astype(f32)))
    r_h = hsplit(r)
    v_h = hsplit(v)

    def both(t):
        return jnp.broadcast_to(t[None], (2,) + t.shape)

    def orient(t):
        return jnp.moveaxis(jnp.stack([t[0], jnp.flip(t[1], axis=1)], axis=0), 2, 0)

    xs = (orient(both(r_h)), orient(hsplit(decay)), orient(k_dir),
          orient(both(v_h)), orient(both(kk)), orient(hsplit(a)))

    def step(S, inp):
        r_t, w_t, k_t, v_t, kk_t, a_t = inp
        sk = jnp.einsum('dbhvk,dbhk->dbhv', S, kk_t)
        S = (S * w_t[..., None, :] - sk[..., None] * (kk_t * a_t)[..., None, :]
             + v_t[..., None] * k_t[..., None, :])
        return S, jnp.einsum('dbhvk,dbhk->dbhv', S, r_t)

    S_fin, ys = lax.scan(step, S0.astype(f32), xs)
    ys = jnp.moveaxis(ys, 0, 2)
    y = ys[0] + jnp.flip(ys[1], axis=1)
    mu = jnp.mean(y, axis=-1, keepdims=True)
    var = jnp.mean(jnp.square(y - mu), axis=-1, keepdims=True)
    yn = ((y - mu) * lax.rsqrt(var + RWKV_GN_EPS)).reshape(B, L, RWKV_WIDTH)
    yn = yn * p['rw_gn_w'].astype(f32) + p['rw_gn_b'].astype(f32)
    bonus = jnp.sum(r_h * hsplit(k) * p['rw_r_k'].astype(f32), axis=-1, keepdims=True) * v_h
    out = (yn + bonus.reshape(B, L, RWKV_WIDTH)) * g
    return out, S_fin


def _mixer(h, p, ctx):
    B, L, _ = h.shape
    u = h @ p['w_in']
    q, k, v, u_hy, u_rkv, w_dn, a_dn, g_dn = jnp.split(u, IN_SPLITS, axis=-1)
    q = q.reshape(B, L, N_ATT_HEADS, HEAD_DIM)
    k = k.reshape(B, L, N_KV_HEADS, HEAD_DIM)
    v = v.reshape(B, L, N_KV_HEADS, HEAD_DIM)
    if ctx is None:
        att = _context_attention(q, k, v, p['attn_sink'])
        S0 = jnp.zeros((2, B, N_RWKV_HEADS, HEAD_DIM, HEAD_DIM), f32)
    else:
        k_ctx, v_ctx, s_ctx = ctx
        att = _latent_attention(_rope_2d(q), _rope_2d(k), v, k_ctx, v_ctx, p['attn_sink'])
        S0 = jnp.swapaxes(s_ctx, 0, 1)
    hy = _hyena(u_hy, p)
    rw, S_fin = _rwkv(u_rkv, w_dn, a_dn, g_dn, p, S0)
    o = jnp.concatenate([att.astype(h.dtype), hy.astype(h.dtype), rw.astype(h.dtype)], axis=-1) @ p['w_out']
    return o, (k, v, jnp.swapaxes(S_fin, 0, 1))


def _layer(x, cond, p, ctx):
    sh1, sc1, g1, sh2, sc2, g2 = _modulation(cond, p['ada_w'], p['ada_b'])
    nw = p['norm_w']
    h = _rmsnorm(x, nw[0]) * (1.0 + sc1) + sh1
    o, st = _mixer(h, p, ctx)
    x = x + g1 * _rmsnorm(o, nw[1])
    h = _rmsnorm(x, nw[2]) * (1.0 + sc2) + sh2
    f = jnp.square(jax.nn.relu(h @ p['mlp_w1'])) @ p['mlp_w2']
    x = x + g2 * _rmsnorm(f, nw[3])
    return x, st


def setup_inputs(seed: int = 0) -> dict:
    key = jax.random.key(seed)
    ks = iter(jax.random.split(key, 40))

    def nrm(shape, scale):
        return jax.random.normal(next(ks), shape, f32) * scale

    def unif(shape, lo, hi):
        return jax.random.uniform(next(ks), shape, f32, lo, hi)

    D = D_MODEL
    return {
        'x_prompt': nrm((BATCH, SEQ, D), 1.0),
        'x_sample': nrm((DEC_BATCH, DEC_SEQ, D), 1.0),
        'cache_k': nrm((DEC_BATCH, DEPTH, PAST_LEN, N_KV_HEADS, HEAD_DIM), 1.0),
        'cache_v': nrm((DEC_BATCH, DEPTH, PAST_LEN, N_KV_HEADS, HEAD_DIM), 1.0),
        'state_rwkv': nrm((DEC_BATCH, DEPTH, 2, N_RWKV_HEADS, HEAD_DIM, HEAD_DIM), 1.0),
        'c': nrm((DEC_BATCH, D), 1.0),
        'c_ctx': nrm((D,), 1.0),
        'ada_w': nrm((DEPTH, D, N_MOD * D), 0.5 * D ** -0.5),
        'ada_b': nrm((DEPTH, N_MOD * D), 0.02),
        'norm_w': 1.0 + nrm((DEPTH, 4, D), 0.02),
        'w_in': nrm((DEPTH, D, IN_COLS), D ** -0.5),
        'w_out': nrm((DEPTH, MIX_WIDTH, D), MIX_WIDTH ** -0.5),
        'attn_sink': nrm((DEPTH, N_ATT_HEADS), 0.5),
        'hy_short_w': nrm((DEPTH, SHORT_W, (HY_ORDER + 1) * HY_CH), SHORT_W ** -0.5),
        'hy_f1': nrm((DEPTH, HY_EMB, HY_FFN), HY_EMB ** -0.5),
        'hy_b1': nrm((DEPTH, HY_FFN), 0.02),
        'hy_f2': nrm((DEPTH, HY_FFN, HY_FFN), HY_FFN ** -0.5),
        'hy_b2': nrm((DEPTH, HY_FFN), 0.02),
        'hy_f3': nrm((DEPTH, HY_FFN, 2 * HY_N_FILT * HY_CH), 0.2 * HY_FFN ** -0.5),
        'hy_decay': unif((DEPTH, 2 * HY_N_FILT * HY_CH), HY_DECAY_MIN, HY_DECAY_MAX),
        'hy_skip': nrm((DEPTH, HY_N_FILT, HY_CH), 0.5),
        'rw_short_w': nrm((DEPTH, SHORT_W, 3 * RWKV_WIDTH), SHORT_W ** -0.5),
        'rw_w0': unif((DEPTH, 2, RWKV_WIDTH), -2.0, 1.0),
        'rw_w_up': nrm((DEPTH, 2, DECAY_LORA, RWKV_WIDTH), 0.5 * DECAY_LORA ** -0.5),
        'rw_a0': nrm((DEPTH, 2, RWKV_WIDTH), 0.1),
        'rw_a_up': nrm((DEPTH, 2, AAA_LORA, RWKV_WIDTH), 0.5 * AAA_LORA ** -0.5),
        'rw_g_up': nrm((DEPTH, GATE_LORA, RWKV_WIDTH), GATE_LORA ** -0.5),
        'rw_k_k': 0.85 + nrm((DEPTH, RWKV_WIDTH), 0.02),
        'rw_k_a': 1.0 + nrm((DEPTH, RWKV_WIDTH), 0.02),
        'rw_r_k': nrm((DEPTH, N_RWKV_HEADS, HEAD_DIM), 0.1),
        'rw_gn_w': 1.0 + nrm((DEPTH, RWKV_WIDTH), 0.02),
        'rw_gn_b': nrm((DEPTH, RWKV_WIDTH), 0.02),
        'mlp_w1': nrm((DEPTH, D, D_FF), D ** -0.5),
        'mlp_w2': nrm((DEPTH, D_FF, D), D_FF ** -0.5),
    }


def reference(x_prompt, x_sample, cache_k, cache_v, state_rwkv, c, c_ctx,
              ada_w, ada_b, norm_w, w_in, w_out, attn_sink,
              hy_short_w, hy_f1, hy_b1, hy_f2, hy_b2, hy_f3, hy_decay, hy_skip,
              rw_short_w, rw_w0, rw_w_up, rw_a0, rw_a_up, rw_g_up, rw_k_k, rw_k_a, rw_r_k,
              rw_gn_w, rw_gn_b, mlp_w1, mlp_w2):
    cond_ctx = c_ctx[None, :]
    y_p = x_prompt
    y_s = x_sample
    new_k, new_v, new_s = [], [], []
    for l in range(DEPTH):
        p = {
            'ada_w': ada_w[l], 'ada_b': ada_b[l], 'norm_w': norm_w[l],
            'w_in': w_in[l], 'w_out': w_out[l], 'attn_sink': attn_sink[l],
            'hy_short_w': hy_short_w[l], 'hy_f1': hy_f1[l], 'hy_b1': hy_b1[l],
            'hy_f2': hy_f2[l], 'hy_b2': hy_b2[l], 'hy_f3': hy_f3[l],
            'hy_decay': hy_decay[l], 'hy_skip': hy_skip[l],
            'rw_short_w': rw_short_w[l], 'rw_w0': rw_w0[l], 'rw_w_up': rw_w_up[l],
            'rw_a0': rw_a0[l], 'rw_a_up': rw_a_up[l], 'rw_g_up': rw_g_up[l],
            'rw_k_k': rw_k_k[l], 'rw_k_a': rw_k_a[l], 'rw_r_k': rw_r_k[l],
            'rw_gn_w': rw_gn_w[l], 'rw_gn_b': rw_gn_b[l],
            'mlp_w1': mlp_w1[l], 'mlp_w2': mlp_w2[l],
        }
        y_p, (k_l, v_l, s_l) = _layer(y_p, cond_ctx, p, None)
        new_k.append(k_l)
        new_v.append(v_l)
        new_s.append(s_l)
        y_s, _ = _layer(y_s, c, p, (cache_k[:, l], cache_v[:, l], state_rwkv[:, l]))
    new_cache_k = jnp.stack(new_k, axis=1)
    new_cache_v = jnp.stack(new_v, axis=1)
    new_state_rwkv = jnp.stack(new_s, axis=1)
    return (y_p, y_s, new_cache_k, new_cache_v, new_state_rwkv)
```

```python
import functools
import math

import numpy as np
import jax
import jax.numpy as jnp
from jax import lax
from jax.experimental import pallas as pl
from jax.experimental.pallas import tpu as pltpu

f32 = jnp.float32
bf16 = jnp.bfloat16

HEAD_DIM = 64
N_ATT_HEADS = 12
N_KV_HEADS = 4
ATT_GROUP = 3
ATT_WIDTH = 768
ATT_KV = 256
WINDOW = 128
ROPE_THETA = 10000.0
NEG_INF = -1e30
HY_CH = 512
HY_BANDS = 16
HY_EMB = 33
HY_FFN = 64
HY_MOD_SHIFT = 0.05
RWKV_WIDTH = 768
N_RWKV_HEADS = 12
DECAY_LORA = 96
AAA_LORA = 96
GATE_LORA = 256
RWKV_GN_EPS = 64e-5
N_MOD = 6
RMS_EPS = 1e-6
CHUNK = 64
LORA_IN = 2 * DECAY_LORA + 2 * AAA_LORA + GATE_LORA
CONV_CH = 3 * HY_CH + 3 * RWKV_WIDTH
HI = lax.Precision.HIGHEST
VMEM_LIMIT = 56 * 1024 * 1024


def _cp(sem, vmem=VMEM_LIMIT):
    return pltpu.CompilerParams(dimension_semantics=sem, vmem_limit_bytes=vmem)


def _dot(a, b):
    return jnp.dot(a, b, preferred_element_type=f32)


def _dot_nt(a, b):
    return lax.dot_general(a, b, (((1,), (1,)), ((), ())), preferred_element_type=f32)


def _dot_tn(a, b):
    return lax.dot_general(a, b, (((0,), (0,)), ((), ())), preferred_element_type=f32)


def _dot_hi(a, b):
    return jnp.dot(a, b, preferred_element_type=f32, precision=HI)


def _mod_body(c_ref, w_ref, b_ref, o_ref):
    c = c_ref[...]
    s = (c * jax.nn.sigmoid(c)).astype(bf16)
    o_ref[...] = _dot(s, w_ref[...].astype(bf16)) + b_ref[...]


def _modulation(cond16, ada_w, ada_b):
    depth, d, n = ada_w.shape
    tn = 1024
    return pl.pallas_call(
        _mod_body,
        grid=(depth, n // tn),
        in_specs=[pl.BlockSpec((16, d), lambda l, j: (0, 0)),
                  pl.BlockSpec((None, d, tn), lambda l, j: (l, 0, j)),
                  pl.BlockSpec((None, 1, tn), lambda l, j: (l, 0, j))],
        out_specs=pl.BlockSpec((None, 16, tn), lambda l, j: (l, 0, j)),
        out_shape=jax.ShapeDtypeStruct((depth, 16, n), f32),
        compiler_params=_cp(("arbitrary", "arbitrary")),
        name="modulation",
    )(cond16, ada_w, ada_b)


def _normmod(x, nw, sc, sh):
    y = x * lax.rsqrt(jnp.mean(x * x, axis=-1, keepdims=True) + RMS_EPS) * nw
    return y * (1.0 + sc) + sh


class _Rows:
    def __init__(self, b_ctx, l_ctx, b_lat, l_lat):
        self.b_ctx, self.l_ctx, self.b_lat, self.l_lat = b_ctx, l_ctx, b_lat, l_lat
        self.t_ctx = b_ctx * l_ctx
        self.t_lat = b_lat * l_lat
        self.t = self.t_ctx + self.t_lat

    def mod_index(self, tm):
        nctx = self.t_ctx // tm
        per = self.l_lat // tm

        def f(i):
            return jnp.where(i < nctx, 0, 1 + (i - nctx) // per)
        return f


def _inproj_body(x_ref, sh_ref, sc_ref, nw_ref, w_ref, o_ref, h_scr):
    @pl.when(pl.program_id(1) == 0)
    def _():
        h_scr[...] = _normmod(x_ref[...], nw_ref[...], sc_ref[...], sh_ref[...]).astype(bf16)
    o_ref[...] = _dot(h_scr[...], w_ref[...])


def _in_proj(x, mod, nw, w_in, rows, tm=512, tn=1920):
    t, d = x.shape
    n = w_in.shape[1]
    midx = rows.mod_index(tm)
    return pl.pallas_call(
        _inproj_body,
        grid=(t // tm, n // tn),
        in_specs=[pl.BlockSpec((tm, d), lambda i, j: (i, 0)),
                  pl.BlockSpec((None, 1, d), lambda i, j: (midx(i), 0, 0)),
                  pl.BlockSpec((None, 1, d), lambda i, j: (midx(i), 0, 1)),
                  pl.BlockSpec((None, 1, d), lambda i, j: (0, 0, 0)),
                  pl.BlockSpec((d, tn), lambda i, j: (0, j))],
        out_specs=pl.BlockSpec((tm, tn), lambda i, j: (i, j)),
        out_shape=jax.ShapeDtypeStruct((t, n), f32),
        scratch_shapes=[pltpu.VMEM((tm, d), bf16)],
        compiler_params=_cp(("arbitrary", "arbitrary")),
        name="in_proj",
    )(x, mod, mod, nw, w_in)


def _conv_body(x_ref, p_ref, n_ref, w_ref, o_ref, *, n_ctx_blk, lat_blk_per_seq, ctx_blk_per_seq):
    i = pl.program_id(0)
    tm = x_ref.shape[0]
    in_ctx = i < n_ctx_blk
    jc = i % ctx_blk_per_seq
    jl = (i - n_ctx_blk) % lat_blk_per_seq
    is_start = jnp.where(in_ctx, jc == 0, jl == 0)
    is_end = jnp.where(in_ctx, jc == ctx_blk_per_seq - 1, jl == lat_blk_per_seq - 1)
    x = x_ref[...]
    prev = jnp.where(is_start, 0.0, p_ref[7:8, :])
    nxt = jnp.where(is_end, 0.0, n_ref[0:1, :])
    row = lax.broadcasted_iota(jnp.int32, (tm, 1), 0)
    xm = jnp.where(row == 0, prev, pltpu.roll(x, 1, 0))
    xp = jnp.where(row == tm - 1, nxt, pltpu.roll(x, tm - 1, 0))
    w = w_ref[...]
    o_ref[...] = xm * w[0:1, :] + x * w[1:2, :] + xp * w[2:3, :]


def _short_conv(u, wconv, rows, col0, tm=256, ct=1280):
    t = u.shape[0]
    c = wconv.shape[1]
    cb0 = col0 // ct
    hb = tm // 8
    nhb = t // 8
    body = functools.partial(_conv_body, n_ctx_blk=rows.t_ctx // tm,
                             lat_blk_per_seq=rows.l_lat // tm, ctx_blk_per_seq=rows.l_ctx // tm)
    return pl.pallas_call(
        body,
        grid=(t // tm, c // ct),
        in_specs=[pl.BlockSpec((tm, ct), lambda i, j: (i, cb0 + j)),
                  pl.BlockSpec((8, ct), lambda i, j: (jnp.maximum(i * hb - 1, 0), cb0 + j)),
                  pl.BlockSpec((8, ct), lambda i, j: (jnp.minimum((i + 1) * hb, nhb - 1), cb0 + j)),
                  pl.BlockSpec((3, ct), lambda i, j: (0, j))],
        out_specs=pl.BlockSpec((tm, ct), lambda i, j: (i, j)),
        out_shape=jax.ShapeDtypeStruct((t, c), f32),
        compiler_params=_cp(("arbitrary", "arbitrary")),
        name="short_conv",
    )(u, u, u, wconv)


def _softmax_pv(s_list, v_list, sink_col):
    m = sink_col
    for s in s_list:
        m = jnp.maximum(m, jnp.max(s, axis=-1, keepdims=True))
    den = jnp.exp(sink_col - m)
    acc = None
    for s, v in zip(s_list, v_list):
        p = jnp.exp(s - m)
        den = den + jnp.sum(p, axis=-1, keepdims=True)
        pv = _dot(p.astype(bf16), v)
        acc = pv if acc is None else acc + pv
    return acc / den


def _attn_ctx_body(sink_ref, q_ref, k_ref, v_ref, o_ref):
    l = q_ref.shape[0]
    scale = HEAD_DIM ** -0.5
    for n in range(N_KV_HEADS):
        kn = k_ref[:, n * HEAD_DIM:(n + 1) * HEAD_DIM].astype(bf16)
        vn = v_ref[:, n * HEAD_DIM:(n + 1) * HEAD_DIM].astype(bf16)
        for g in range(ATT_GROUP):
            h = n * ATT_GROUP + g
            q = (q_ref[:, h * HEAD_DIM:(h + 1) * HEAD_DIM] * scale).astype(bf16)
            s = _dot_nt(q, kn)
            sink_col = jnp.full((l, 1), sink_ref[h], f32)
            o_ref[:, h * HEAD_DIM:(h + 1) * HEAD_DIM] = _softmax_pv([s], [vn], sink_col)


def _attn_ctx(u, sink, rows):
    b, l = rows.b_ctx, rows.l_ctx
    return pl.pallas_call(
        _attn_ctx_body,
        grid=(b,),
        in_specs=[pl.BlockSpec(memory_space=pltpu.SMEM),
                  pl.BlockSpec((l, ATT_WIDTH), lambda i: (i, 0)),
                  pl.BlockSpec((l, ATT_KV), lambda i: (i, ATT_WIDTH // ATT_KV)),
                  pl.BlockSpec((l, ATT_KV), lambda i: (i, ATT_WIDTH // ATT_KV + 1))],
        out_specs=pl.BlockSpec((l, ATT_WIDTH), lambda i: (i, 0)),
        out_shape=jax.ShapeDtypeStruct((b * l, ATT_WIDTH), f32),
        compiler_params=_cp(("arbitrary",)),
        name="attn_ctx",
    )(sink, u, u, u)


def _rope(x, cos, sin_signed):
    w = x.shape[1]
    lane = lax.broadcasted_iota(jnp.int32, (1, w), 1)
    swapped = jnp.where((lane % 32) < 16, pltpu.roll(x, w - 16, 1), pltpu.roll(x, 16, 1))
    return x * cos + swapped * sin_signed


def _attn_lat_body(sink_ref, q_ref, k_ref, v_ref, kc_ref, vc_ref, cos_ref, sin_ref, o_ref,
                   q_scr, k_scr, v_scr):
    l = q_ref.shape[0]
    blk = WINDOW
    nb = l // blk
    scale = HEAD_DIM ** -0.5
    cos = cos_ref[...]
    sin = sin_ref[...]
    q_scr[...] = (_rope(q_ref[...], cos, sin) * scale).astype(bf16)
    zpad = jnp.zeros((blk, ATT_KV), bf16)
    k_scr[0:blk, :] = zpad
    k_scr[blk + l:, :] = zpad
    v_scr[0:blk, :] = zpad
    v_scr[blk + l:, :] = zpad
    k_scr[blk:blk + l, :] = _rope(k_ref[...], cos[:, :ATT_KV], sin[:, :ATT_KV]).astype(bf16)
    v_scr[blk:blk + l, :] = v_ref[...].astype(bf16)
    r = lax.broadcasted_iota(jnp.int32, (ATT_GROUP * blk, 3 * blk), 0) % blk
    c = lax.broadcasted_iota(jnp.int32, (ATT_GROUP * blk, 3 * blk), 1)
    band = (c - r >= 0) & (c - r <= 2 * WINDOW)
    for i in range(nb):
        kpos = (i - 1) * blk + c
        mask = band & (kpos >= 0) & (kpos < l)
        for n in range(N_KV_HEADS):
            cs = slice(n * HEAD_DIM, (n + 1) * HEAD_DIM)
            kl = k_scr[i * blk:(i + 3) * blk, cs]
            vl = v_scr[i * blk:(i + 3) * blk, cs]
            kc = kc_ref[:, cs].astype(bf16)
            vc = vc_ref[:, cs].astype(bf16)
            qs = jnp.concatenate(
                [q_scr[i * blk:(i + 1) * blk, (n * ATT_GROUP + g) * HEAD_DIM:(n * ATT_GROUP + g + 1) * HEAD_DIM]
                 for g in range(ATT_GROUP)], axis=0)
            s_loc = jnp.where(mask, _dot_nt(qs, kl), NEG_INF)
            s_ctx = _dot_nt(qs, kc)
            sink_col = jnp.concatenate(
                [jnp.full((blk, 1), sink_ref[n * ATT_GROUP + g], f32) for g in range(ATT_GROUP)], axis=0)
            o = _softmax_pv([s_loc, s_ctx], [vl, vc], sink_col)
            for g in range(ATT_GROUP):
                h = n * ATT_GROUP + g
                o_ref[i * blk:(i + 1) * blk, h * HEAD_DIM:(h + 1) * HEAD_DIM] = o[g * blk:(g + 1) * blk, :]


def _attn_lat(u, kc, vc, sink, cos, sin, rows):
    b, l = rows.b_lat, rows.l_lat
    rb0 = rows.t_ctx // l
    lc = kc.shape[1]
    return pl.pallas_call(
        _attn_lat_body,
        grid=(b,),
        in_specs=[pl.BlockSpec(memory_space=pltpu.SMEM),
                  pl.BlockSpec((l, ATT_WIDTH), lambda i: (rb0 + i, 0)),
                  pl.BlockSpec((l, ATT_KV), lambda i: (rb0 + i, ATT_WIDTH // ATT_KV)),
                  pl.BlockSpec((l, ATT_KV), lambda i: (rb0 + i, ATT_WIDTH // ATT_KV + 1)),
                  pl.BlockSpec((None, lc, ATT_KV), lambda i: (i, 0, 0)),
                  pl.BlockSpec((None, lc, ATT_KV), lambda i: (i, 0, 0)),
                  pl.BlockSpec((l, ATT_WIDTH), lambda i: (0, 0)),
                  pl.BlockSpec((l, ATT_WIDTH), lambda i: (0, 0))],
        out_specs=pl.BlockSpec((l, ATT_WIDTH), lambda i: (i, 0)),
        out_shape=jax.ShapeDtypeStruct((b * l, ATT_WIDTH), f32),
        scratch_shapes=[pltpu.VMEM((l, ATT_WIDTH), bf16),
                        pltpu.VMEM((l + 2 * WINDOW, ATT_KV), bf16),
                        pltpu.VMEM((l + 2 * WINDOW, ATT_KV), bf16)],
        compiler_params=_cp(("arbitrary",)),
        name="attn_lat",
    )(sink, u, u, u, kc, vc, cos, sin)


def _rope_tables(l, grid_w):
    half = HEAD_DIM // 2
    pos = jnp.arange(l)
    rowp = (pos // grid_w).astype(f32)
    colp = (pos % grid_w).astype(f32)
    freqs = ROPE_THETA ** (-jnp.arange(0, half, 2, dtype=f32) / half)
    ar = rowp[:, None] * freqs[None, :]
    ac = colp[:, None] * freqs[None, :]
    cos = jnp.concatenate([jnp.cos(ar), jnp.cos(ar), jnp.cos(ac), jnp.cos(ac)], axis=-1)
    sin = jnp.concatenate([-jnp.sin(ar), jnp.sin(ar), -jnp.sin(ac), jnp.sin(ac)], axis=-1)
    return jnp.tile(cos, (1, N_ATT_HEADS)), jnp.tile(sin, (1, N_ATT_HEADS))


def _dft_tables(l):
    k = jnp.arange(l, dtype=jnp.int32)[:, None]
    s = jnp.arange(l, dtype=jnp.int32)[None, :]
    m = ((2 * k + 1) * s) % (4 * l)
    theta = m.astype(f32) * (math.pi / (2 * l))
    return jnp.cos(theta), jnp.sin(theta)


def _hy_feat(l):
    t = jnp.arange(l, dtype=f32)
    t01 = (t / max(l - 1, 1))[:, None]
    bands = jnp.linspace(1e-4, HY_BANDS - 1, HY_BANDS, dtype=f32)
    ang = (2.0 * math.pi / l) * t[:, None] * bands[None, :]
    feat = jnp.concatenate([t01, jnp.cos(ang), -jnp.sin(ang)], axis=-1)
    return jnp.pad(feat, ((0, 0), (0, 128 - HY_EMB))), t01


def _hy_filter_body(feat_ref, t01_ref, f1_ref, b1_ref, f2_ref, b2_ref, f3_ref, dec_ref, cm_ref, sm_ref,
                    hr_ref, hi_ref):
    h = jnp.sin(_dot_hi(feat_ref[...], f1_ref[...]) + b1_ref[...])
    h = jnp.sin(_dot_hi(h, f2_ref[...]) + b2_ref[...])
    h = _dot_hi(h, f3_ref[...])
    h = h * (jnp.exp(-t01_ref[...] * jnp.abs(dec_ref[...])) + HY_MOD_SHIFT)
    l = h.shape[0]
    fwd = h[:, :HY_CH]
    row = lax.broadcasted_iota(jnp.int32, (l, 1), 0)
    bwd = jnp.where(row == 0, 0.0, h[:, HY_CH:])
    hr_ref[...] = _dot_hi(cm_ref[...], fwd + bwd)
    hi_ref[...] = _dot_hi(sm_ref[...], bwd - fwd)


def _hy_filter(l, feat, t01, f1p, b1, f2, b2, f3, dec, cm, sm):
    args = (feat, t01, f1p, b1, f2, b2, f3, dec, cm, sm)
    return pl.pallas_call(
        _hy_filter_body,
        grid=(1,),
        in_specs=[pl.BlockSpec(a.shape, lambda i, nd=a.ndim: (0,) * nd) for a in args],
        out_specs=[pl.BlockSpec((l, HY_CH), lambda i: (0, 0))] * 2,
        out_shape=[jax.ShapeDtypeStruct((l, HY_CH), f32)] * 2,
        compiler_params=_cp(("arbitrary",)),
        name="hy_filter",
    )(*args)


def _hy_conv_body(x0_ref, x1_ref, v_ref, hr_ref, hi_ref, skip_ref, cm_ref, sm_ref, cmt_ref, smt_ref, o_ref):
    l = v_ref.shape[0]
    z = v_ref[...] * x1_ref[...]
    zb = z.astype(bf16)
    zr = _dot(cm_ref[...], zb)
    zs = _dot(sm_ref[...], zb)
    hr = hr_ref[...]
    hi = hi_ref[...]
    yr = (zr * hr + zs * hi).astype(bf16)
    yi = (zr * hi - zs * hr).astype(bf16)
    y = (_dot(cmt_ref[...], yr) - _dot(smt_ref[...], yi)) * (1.0 / l)
    o_ref[...] = (y + skip_ref[...] * z) * x0_ref[...]


def _hy_conv(uc, hr, hi, skip, cm, sm, cmt, smt, b, l, rb0):
    full = lambda a: pl.BlockSpec(a.shape, lambda i, nd=a.ndim: (0,) * nd)
    return pl.pallas_call(
        _hy_conv_body,
        grid=(b,),
        in_specs=[pl.BlockSpec((l, HY_CH), lambda i: (rb0 + i, 0)),
                  pl.BlockSpec((l, HY_CH), lambda i: (rb0 + i, 1)),
                  pl.BlockSpec((l, HY_CH), lambda i: (rb0 + i, 2)),
                  full(hr), full(hi), full(skip), full(cm), full(sm), full(cmt), full(smt)],
        out_specs=pl.BlockSpec((l, HY_CH), lambda i: (i, 0)),
        out_shape=jax.ShapeDtypeStruct((b * l, HY_CH), f32),
        compiler_params=_cp(("arbitrary",)),
        name="hy_conv",
    )(uc, uc, uc, hr, hi, skip, cm, sm, cmt, smt)


def _head_sum_matrix(scale=1.0):
    i = np.arange(RWKV_WIDTH) // HEAD_DIM
    return jnp.asarray((i[:, None] == i[None, :]).astype(np.float32) * scale)


def _rw_prep_body(lo_ref, k_ref, wl_ref, w0_ref, a0_ref, kk_ref_, hs_ref, kk_o, lw_o, a_o, g_o):
    x = lo_ref[...]
    lane = lax.broadcasted_iota(jnp.int32, x.shape, 1)
    act = jnp.where(lane < 2 * DECAY_LORA, jnp.tanh(x),
                    jnp.where(lane < 2 * DECAY_LORA + 2 * AAA_LORA, x, jax.nn.sigmoid(x)))
    lo = _dot(act.astype(bf16), wl_ref[...])
    c = RWKV_WIDTH
    for d in range(2):
        zneg = -(w0_ref[d:d + 1, :] + lo[:, d * c:(d + 1) * c])
        softplus = jnp.maximum(zneg, 0.0) + jnp.log(1.0 + jnp.exp(-jnp.abs(zneg)))
        w_log = -softplus - 0.5
        lw_o[d] = -jnp.exp(w_log)
        a_o[d] = jax.nn.sigmoid(a0_ref[d:d + 1, :] + lo[:, (2 + d) * c:(3 + d) * c])
    g_o[...] = lo[:, 4 * c:5 * c]
    kk = k_ref[...] * kk_ref_[...]
    ss = _dot_hi(kk * kk, hs_ref[...])
    kk_o[...] = kk / jnp.maximum(jnp.sqrt(ss), 1e-12)


def _rw_prep(u, uc, w_lora, w0, a0, k_k, hsum, tm=256):
    t = u.shape[0]
    c = RWKV_WIDTH
    full = lambda a: pl.BlockSpec(a.shape, lambda i, nd=a.ndim: (0,) * nd)
    lora_cb = (u.shape[1] - LORA_IN) // LORA_IN
    k_cb = (3 * HY_CH) // c + 1
    return pl.pallas_call(
        _rw_prep_body,
        grid=(t // tm,),
        in_specs=[pl.BlockSpec((tm, LORA_IN), lambda i: (i, lora_cb)),
                  pl.BlockSpec((tm, c), lambda i: (i, k_cb)),
                  full(w_lora), full(w0), full(a0), full(k_k), full(hsum)],
        out_specs=[pl.BlockSpec((tm, c), lambda i: (i, 0)),
                   pl.BlockSpec((2, tm, c), lambda i: (0, i, 0)),
                   pl.BlockSpec((2, tm, c), lambda i: (0, i, 0)),
                   pl.BlockSpec((tm, c), lambda i: (i, 0))],
        out_shape=[jax.ShapeDtypeStruct((t, c), f32),
                   jax.ShapeDtypeStruct((2, t, c), f32),
                   jax.ShapeDtypeStruct((2, t, c), f32),
                   jax.ShapeDtypeStruct((t, c), f32)],
        compiler_params=_cp(("arbitrary",)),
        name="rw_prep",
    )(u, uc, w_lora, w0, a0, k_k, hsum)


def _rw_scan_body(rb_ref, first_ref, seq_ref,
                  r_ref, k_ref, v_ref, kk_ref, lw_ref, a_ref, ka_ref, h0_ref,
                  y_ref, hf_ref, h_scr):
    d = pl.program_id(0)
    s = pl.program_id(1)
    cdim = CHUNK

    @pl.when(first_ref[s] == 1)
    def _():
        h_scr[...] = h0_ref[...]

    row = lax.broadcasted_iota(jnp.int32, (cdim, cdim), 0)
    col = lax.broadcasted_iota(jnp.int32, (cdim, cdim), 1)
    fwd = d == 0
    lag = jnp.where(fwd, row - col, col - row)
    incl = lag >= 0
    strict = lag > 0
    tri = incl.astype(f32)
    eye = (row == col).astype(f32)

    lw = lw_ref[...]
    cs = _dot_hi(tri, lw)
    p_in = jnp.exp(cs)
    p_ex = jnp.exp(cs - lw)
    p_inv = jnp.exp(-cs)
    p_all = jnp.exp(jnp.sum(lw, axis=0, keepdims=True))
    a = a_ref[...]
    kk = kk_ref[...]
    k = k_ref[...]
    kap_t = kk * p_ex
    r_t = r_ref[...] * p_in
    beta_t = kk * a * p_inv
    kdir_t = k * (1.0 + (a - 1.0) * ka_ref[...]) * p_inv
    v = v_ref[...]

    for h in range(N_RWKV_HEADS):
        hs = slice(h * HEAD_DIM, (h + 1) * HEAD_DIM)
        lhs = jnp.concatenate([kap_t[:, hs], r_t[:, hs]], axis=0)
        rhs = jnp.concatenate([beta_t[:, hs], kdir_t[:, hs]], axis=0)
        m = lax.dot_general(lhs, rhs, (((1,), (1,)), ((), ())), preferred_element_type=f32, precision=HI)
        a_ab = jnp.where(strict, m[:cdim, :cdim], 0.0)
        a_ak = jnp.where(strict, m[:cdim, cdim:], 0.0)
        m_rb = jnp.where(incl, m[cdim:, :cdim], 0.0)
        m_rk = jnp.where(incl, m[cdim:, cdim:], 0.0)
        h0 = h_scr[h]
        vh = v[:, hs]
        g = _dot_hi(kap_t[:, hs], h0) + _dot_hi(a_ak, vh)
        x = -a_ab
        uu = g
        n_sq = int(math.log2(cdim))
        for it in range(n_sq):
            uu = uu + _dot_hi(x, uu)
            if it + 1 < n_sq:
                x = _dot_hi(x, x)
        uu = -uu
        y_ref[:, hs] = _dot_hi(r_t[:, hs], h0) + _dot_hi(m_rb, uu) + _dot_hi(m_rk, vh)
        upd = (lax.dot_general(beta_t[:, hs], uu, (((0,), (0,)), ((), ())), preferred_element_type=f32, precision=HI)
               + lax.dot_general(kdir_t[:, hs], vh, (((0,), (0,)), ((), ())), preferred_element_type=f32, precision=HI))
        pcol = jnp.sum(eye * p_all[:, hs], axis=1, keepdims=True)
        h_new = (h0 + upd) * pcol
        h_scr[h] = h_new
        hf_ref[h] = h_new


def _rw_scan(uc, kk, lw, a, k_a, h0, tabs, n_steps):
    c = RWKV_WIDTH
    t = uc.shape[0]
    nseq = h0.shape[0]
    rkv_cb = (3 * HY_CH) // c
    grid_spec = pltpu.PrefetchScalarGridSpec(
        num_scalar_prefetch=3,
        grid=(2, n_steps),
        in_specs=[pl.BlockSpec((CHUNK, c), lambda d, s, rb, fi, sq: (rb[d, s], rkv_cb)),
                  pl.BlockSpec((CHUNK, c), lambda d, s, rb, fi, sq: (rb[d, s], rkv_cb + 1)),
                  pl.BlockSpec((CHUNK, c), lambda d, s, rb, fi, sq: (rb[d, s], rkv_cb + 2)),
                  pl.BlockSpec((CHUNK, c), lambda d, s, rb, fi, sq: (rb[d, s], 0)),
                  pl.BlockSpec((None, CHUNK, c), lambda d, s, rb, fi, sq: (d, rb[d, s], 0)),
                  pl.BlockSpec((None, CHUNK, c), lambda d, s, rb, fi, sq: (d, rb[d, s], 0)),
                  pl.BlockSpec((1, c), lambda d, s, rb, fi, sq: (0, 0)),
                  pl.BlockSpec((None, None, N_RWKV_HEADS, HEAD_DIM, HEAD_DIM),
                               lambda d, s, rb, fi, sq: (sq[s], d, 0, 0, 0))],
        out_specs=[pl.BlockSpec((None, CHUNK, c), lambda d, s, rb, fi, sq: (d, rb[d, s], 0)),
                   pl.BlockSpec((None, None, N_RWKV_HEADS, HEAD_DIM, HEAD_DIM),
                                lambda d, s, rb, fi, sq: (sq[s], d, 0, 0, 0))],
        scratch_shapes=[pltpu.VMEM((N_RWKV_HEADS, HEAD_DIM, HEAD_DIM), f32)],
    )
    return pl.pallas_call(
        _rw_scan_body,
        grid_spec=grid_spec,
        out_shape=[jax.ShapeDtypeStruct((2, t, c), f32),
                   jax.ShapeDtypeStruct((nseq, 2, N_RWKV_HEADS, HEAD_DIM, HEAD_DIM), f32)],
        compiler_params=_cp(("arbitrary", "arbitrary")),
        name="rw_scan",
    )(*tabs, uc, uc, uc, kk, lw, a, k_a, h0)


def _scan_tables(rows):
    rb = [[], []]
    first, seq = [], []
    base = 0
    sidx = 0
    for b, l in ((rows.b_ctx, rows.l_ctx), (rows.b_lat, rows.l_lat)):
        nc = l // CHUNK
        for i in range(b):
            for c in range(nc):
                rb[0].append(base + c)
                rb[1].append(base + nc - 1 - c)
                first.append(1 if c == 0 else 0)
                seq.append(sidx)
            base += nc
            sidx += 1
    return (jnp.asarray(np.array(rb, np.int32)), jnp.asarray(np.array(first, np.int32)),
            jnp.asarray(np.array(seq, np.int32))), len(first)


def _rw_post_body(y_ref, r_ref, k_ref, v_ref, g_ref, rk_ref, gw_ref, gb_ref, hm_ref, hs_ref, o_ref):
    y = y_ref[0] + y_ref[1]
    mu = _dot_hi(y, hm_ref[...])
    yc = y - mu
    var = _dot_hi(yc * yc, hm_ref[...])
    yn = yc * lax.rsqrt(var + RWKV_GN_EPS) * gw_ref[...] + gb_ref[...]
    bonus = _dot_hi(r_ref[...] * k_ref[...] * rk_ref[...], hs_ref[...]) * v_ref[...]
    o_ref[...] = (yn + bonus) * g_ref[...]


def _rw_post(y, uc, g, r_k, gn_w, gn_b, hmean, hsum, tm=256):
    t = uc.shape[0]
    c = RWKV_WIDTH
    rkv_cb = (3 * HY_CH) // c
    full = lambda a: pl.BlockSpec(a.shape, lambda i, nd=a.ndim: (0,) * nd)
    return pl.pallas_call(
        _rw_post_body,
        grid=(t // tm,),
        in_specs=[pl.BlockSpec((2, tm, c), lambda i: (0, i, 0)),
                  pl.BlockSpec((tm, c), lambda i: (i, rkv_cb)),
                  pl.BlockSpec((tm, c), lambda i: (i, rkv_cb + 1)),
                  pl.BlockSpec((tm, c), lambda i: (i, rkv_cb + 2)),
                  pl.BlockSpec((tm, c), lambda i: (i, 0)),
                  full(r_k), full(gn_w), full(gn_b), full(hmean), full(hsum)],
        out_specs=pl.BlockSpec((tm, c), lambda i: (i, 0)),
        out_shape=jax.ShapeDtypeStruct((t, c), f32),
        compiler_params=_cp(("arbitrary",)),
        name="rw_post",
    )(y, uc, uc, uc, g, r_k, gn_w, gn_b, hmean, hsum)


def _outproj_body(att_ref, hy_ref, rw_ref, w_ref, x_ref, g_ref, nw_ref, o_ref):
    o = (_dot(att_ref[...].astype(bf16), w_ref[0:ATT_WIDTH, :])
         + _dot(hy_ref[...].astype(bf16), w_ref[ATT_WIDTH:ATT_WIDTH + HY_CH, :])
         + _dot(rw_ref[...].astype(bf16), w_ref[ATT_WIDTH + HY_CH:, :]))
    y = o * lax.rsqrt(jnp.mean(o * o, axis=-1, keepdims=True) + RMS_EPS) * nw_ref[...]
    o_ref[...] = x_ref[...] + g_ref[...] * y


def _out_proj(att, hy, rw, w_out, x, mod, nw, rows, tm=256):
    t, d = x.shape
    midx = rows.mod_index(tm)
    return pl.pallas_call(
        _outproj_body,
        grid=(t // tm,),
        in_specs=[pl.BlockSpec((tm, ATT_WIDTH), lambda i: (i, 0)),
                  pl.BlockSpec((tm, HY_CH), lambda i: (i, 0)),
                  pl.BlockSpec((tm, RWKV_WIDTH), lambda i: (i, 0)),
                  pl.BlockSpec(w_out.shape, lambda i: (0, 0)),
                  pl.BlockSpec((tm, d), lambda i: (i, 0)),
                  pl.BlockSpec((None, 1, d), lambda i: (midx(i), 0, 2)),
                  pl.BlockSpec((None, 1, d), lambda i: (1, 0, 0))],
        out_specs=pl.BlockSpec((tm, d), lambda i: (i, 0)),
        out_shape=jax.ShapeDtypeStruct((t, d), f32),
        compiler_params=_cp(("arbitrary",)),
        name="out_proj",
    )(att, hy, rw, w_out, x, mod, nw)


def _mlp_body(x_ref, sh_ref, sc_ref, g_ref, nw2_ref, nw3_ref, w1_ref, w2_ref, o_ref, h_scr, acc_scr):
    j = pl.program_id(1)

    @pl.when(j == 0)
    def _():
        h_scr[...] = _normmod(x_ref[...], nw2_ref[...], sc_ref[...], sh_ref[...]).astype(bf16)
        acc_scr[...] = jnp.zeros_like(acc_scr)

    a = jnp.maximum(_dot(h_scr[...], w1_ref[...]), 0.0)
    acc_scr[...] += _dot((a * a).astype(bf16), w2_ref[...])

    @pl.when(j == pl.num_programs(1) - 1)
    def _():
        f = acc_scr[...]
        y = f * lax.rsqrt(jnp.mean(f * f, axis=-1, keepdims=True) + RMS_EPS) * nw3_ref[...]
        o_ref[...] = x_ref[...] + g_ref[...] * y


def _mlp(x, mod, nw, w1, w2, rows, tm=512, tf=1024):
    t, d = x.shape
    dff = w1.shape[1]
    midx = rows.mod_index(tm)
    return pl.pallas_call(
        _mlp_body,
        grid=(t // tm, dff // tf),
        in_specs=[pl.BlockSpec((tm, d), lambda i, j: (i, 0)),
                  pl.BlockSpec((None, 1, d), lambda i, j: (midx(i), 0, 3)),
                  pl.BlockSpec((None, 1, d), lambda i, j: (midx(i), 0, 4)),
                  pl.BlockSpec((None, 1, d), lambda i, j: (midx(i), 0, 5)),
                  pl.BlockSpec((None, 1, d), lambda i, j: (2, 0, 0)),
                  pl.BlockSpec((None, 1, d), lambda i, j: (3, 0, 0)),
                  pl.BlockSpec((d, tf), lambda i, j: (0, j)),
                  pl.BlockSpec((tf, d), lambda i, j: (j, 0))],
        out_specs=pl.BlockSpec((tm, d), lambda i, j: (i, 0)),
        out_shape=jax.ShapeDtypeStruct((t, d), f32),
        scratch_shapes=[pltpu.VMEM((tm, d), bf16), pltpu.VMEM((tm, d), f32)],
        compiler_params=_cp(("arbitrary", "arbitrary")),
        name="mlp",
    )(x, mod, mod, mod, nw, nw, w1, w2)


def _lora_weight(w_up, a_up, g_up):
    c = RWKV_WIDTH
    w = jnp.zeros((LORA_IN, 5 * c), f32)
    r = 0
    for j, blk in enumerate((w_up[0], w_up[1], a_up[0], a_up[1], g_up)):
        w = w.at[r:r + blk.shape[0], j * c:(j + 1) * c].set(blk)
        r += blk.shape[0]
    return w.astype(bf16)


def kernel(x_prompt, x_sample, cache_k, cache_v, state_rwkv, c, c_ctx, ada_w, ada_b, norm_w, w_in, w_out,
           attn_sink, hy_short_w, hy_f1, hy_b1, hy_f2, hy_b2, hy_f3, hy_decay, hy_skip, rw_short_w, rw_w0,
           rw_w_up, rw_a0, rw_a_up, rw_g_up, rw_k_k, rw_k_a, rw_r_k, rw_gn_w, rw_gn_b, mlp_w1, mlp_w2):
    b_ctx, l_ctx, d = x_prompt.shape
    b_lat, l_lat, _ = x_sample.shape
    depth = ada_w.shape[0]
    grid_w = 64
    rows = _Rows(b_ctx, l_ctx, b_lat, l_lat)
    t = rows.t

    x = jnp.concatenate([x_prompt.reshape(rows.t_ctx, d), x_sample.reshape(rows.t_lat, d)], axis=0)
    cond16 = jnp.zeros((16, d), f32).at[0].set(c_ctx).at[1:1 + b_lat].set(c)
    mod_all = _modulation(cond16, ada_w, ada_b[:, None, :])

    cos_t, sin_t = _rope_tables(l_lat, grid_w)
    hsum = _head_sum_matrix()
    hmean = _head_sum_matrix(1.0 / HEAD_DIM)
    scan_tabs, n_steps = _scan_tables(rows)
    dft = {}
    for l in (l_ctx, l_lat):
        cm, sm = _dft_tables(l)
        feat, t01 = _hy_feat(l)
        dft[l] = dict(cm=cm, sm=sm, cmb=cm.astype(bf16), smb=sm.astype(bf16),
                      cmtb=cm.T.astype(bf16), smtb=sm.T.astype(bf16), feat=feat, t01=t01)

    new_k, new_v, new_s = [], [], []
    for l in range(depth):
        mod = mod_all[l].reshape(16, 1, N_MOD * d)
        nw = norm_w[l].reshape(4, 1, d)
        u = _in_proj(x, mod, nw, w_in[l].astype(bf16), rows)

        kv0 = ATT_WIDTH
        new_k.append(u[:rows.t_ctx, kv0:kv0 + ATT_KV].reshape(b_ctx, l_ctx, N_KV_HEADS, HEAD_DIM))
        new_v.append(u[:rows.t_ctx, kv0 + ATT_KV:kv0 + 2 * ATT_KV].reshape(b_ctx, l_ctx, N_KV_HEADS, HEAD_DIM))

        wconv = jnp.concatenate([hy_short_w[l], rw_short_w[l]], axis=1)
        uc = _short_conv(u, wconv, rows, ATT_WIDTH + 2 * ATT_KV)

        att_c = _attn_ctx(u, attn_sink[l], rows)
        att_l = _attn_lat(u, cache_k[:, l].reshape(b_lat, -1, ATT_KV), cache_v[:, l].reshape(b_lat, -1, ATT_KV),
                          attn_sink[l], cos_t, sin_t, rows)
        att = jnp.concatenate([att_c, att_l], axis=0)

        f1p = jnp.pad(hy_f1[l], ((0, 128 - HY_EMB), (0, 0)))
        hys = []
        for (bb, ll, rb0) in ((b_ctx, l_ctx, 0), (b_lat, l_lat, rows.t_ctx // l_lat)):
            tb = dft[ll]
            hr, hi = _hy_filter(ll, tb['feat'], tb['t01'], f1p, hy_b1[l][None], hy_f2[l], hy_b2[l][None],
                                hy_f3[l], hy_decay[l][None], tb['cm'], tb['sm'])
            hys.append(_hy_conv(uc, hr, hi, hy_skip[l], tb['cmb'], tb['smb'], tb['cmtb'], tb['smtb'],
                                bb, ll, rb0))
        hy = jnp.concatenate(hys, axis=0)

        w_lora = _lora_weight(rw_w_up[l], rw_a_up[l], rw_g_up[l])
        kk, lw, a, g = _rw_prep(u, uc, w_lora, rw_w0[l], rw_a0[l], rw_k_k[l][None], hsum)
        h0 = jnp.concatenate([jnp.zeros((b_ctx, 2, N_RWKV_HEADS, HEAD_DIM, HEAD_DIM), f32),
                              jnp.swapaxes(state_rwkv[:, l], -1, -2)], axis=0)
        y, hfin = _rw_scan(uc, kk, lw, a, rw_k_a[l][None], h0, scan_tabs, n_steps)
        new_s.append(jnp.swapaxes(hfin[:b_ctx], -1, -2))
        rw = _rw_post(y, uc, g, rw_r_k[l].reshape(1, RWKV_WIDTH), rw_gn_w[l][None], rw_gn_b[l][None],
                      hmean, hsum)

        x = _out_proj(att, hy, rw, w_out[l].astype(bf16), x, mod, nw, rows)
        x = _mlp(x, mod, nw, mlp_w1[l].astype(bf16), mlp_w2[l].astype(bf16), rows)

    y_p = x[:rows.t_ctx].reshape(b_ctx, l_ctx, d)
    y_s = x[rows.t_ctx:].reshape(b_lat, l_lat, d)
    return (y_p, y_s, jnp.stack(new_k, axis=1), jnp.stack(new_v, axis=1), jnp.stack(new_s, axis=1))
```

```python
import functools
import math

import numpy as np
import jax
import jax.numpy as jnp
from jax import lax
from jax.experimental import pallas as pl
from jax.experimental.pallas import tpu as pltpu

f32 = jnp.float32
bf16 = jnp.bfloat16

HEAD_DIM = 64
N_ATT_HEADS = 12
N_KV_HEADS = 4
ATT_GROUP = 3
ATT_WIDTH = 768
ATT_KV = 256
WINDOW = 128
ROPE_THETA = 10000.0
NEG_INF = -1e30
HY_CH = 512
HY_BANDS = 16
HY_EMB = 33
HY_FFN = 64
HY_MOD_SHIFT = 0.05
RWKV_WIDTH = 768
N_RWKV_HEADS = 12
DECAY_LORA = 96
AAA_LORA = 96
GATE_LORA = 256
RWKV_GN_EPS = 64e-5
N_MOD = 6
RMS_EPS = 1e-6
CHUNK = 64
LORA_IN = 2 * DECAY_LORA + 2 * AAA_LORA + GATE_LORA
CONV_CH = 3 * HY_CH + 3 * RWKV_WIDTH
HI = lax.Precision.HIGHEST
VMEM_LIMIT = 56 * 1024 * 1024


def _cp(sem, vmem=VMEM_LIMIT):
    return pltpu.CompilerParams(dimension_semantics=sem, vmem_limit_bytes=vmem)


def _dot(a, b):
    return jnp.dot(a, b, preferred_element_type=f32)


def _dot_nt(a, b):
    return lax.dot_general(a, b, (((1,), (1,)), ((), ())), preferred_element_type=f32)


def _dot_tn(a, b):
    return lax.dot_general(a, b, (((0,), (0,)), ((), ())), preferred_element_type=f32)


def _dot_hi(a, b):
    return jnp.dot(a, b, preferred_element_type=f32, precision=HI)


def _mod_body(c_ref, w_ref, b_ref, o_ref):
    c = c_ref[...]
    s = (c * jax.nn.sigmoid(c)).astype(bf16)
    o_ref[...] = _dot(s, w_ref[...].astype(bf16)) + b_ref[...]


def _modulation(cond16, ada_w, ada_b):
    depth, d, n = ada_w.shape
    tn = 1024
    return pl.pallas_call(
        _mod_body,
        grid=(depth, n // tn),
        in_specs=[pl.BlockSpec((16, d), lambda l, j: (0, 0)),
                  pl.BlockSpec((None, d, tn), lambda l, j: (l, 0, j)),
                  pl.BlockSpec((None, 1, tn), lambda l, j: (l, 0, j))],
        out_specs=pl.BlockSpec((None, 16, tn), lambda l, j: (l, 0, j)),
        out_shape=jax.ShapeDtypeStruct((depth, 16, n), f32),
        compiler_params=_cp(("arbitrary", "arbitrary")),
        name="modulation",
    )(cond16, ada_w, ada_b)


def _normmod(x, nw, sc, sh):
    y = x * lax.rsqrt(jnp.mean(x * x, axis=-1, keepdims=True) + RMS_EPS) * nw
    return y * (1.0 + sc) + sh


class _Rows:
    def __init__(self, b_ctx, l_ctx, b_lat, l_lat):
        self.b_ctx, self.l_ctx, self.b_lat, self.l_lat = b_ctx, l_ctx, b_lat, l_lat
        self.t_ctx = b_ctx * l_ctx
        self.t_lat = b_lat * l_lat
        self.t = self.t_ctx + self.t_lat

    def mod_index(self, tm):
        nctx = self.t_ctx // tm
        per = self.l_lat // tm

        def f(i):
            return jnp.where(i < nctx, 0, 1 + (i - nctx) // per)
        return f


def _inproj_body(x_ref, sh_ref, sc_ref, nw_ref, w_ref, o_ref, h_scr):
    @pl.when(pl.program_id(1) == 0)
    def _():
        h_scr[...] = _normmod(x_ref[...], nw_ref[...], sc_ref[...], sh_ref[...]).astype(bf16)
    o_ref[...] = _dot(h_scr[...], w_ref[...])


def _in_proj(x, mod, nw, w_in, rows, tm=512, tn=1920):
    t, d = x.shape
    n = w_in.shape[1]
    midx = rows.mod_index(tm)
    return pl.pallas_call(
        _inproj_body,
        grid=(t // tm, n // tn),
        in_specs=[pl.BlockSpec((tm, d), lambda i, j: (i, 0)),
                  pl.BlockSpec((None, 1, d), lambda i, j: (midx(i), 0, 0)),
                  pl.BlockSpec((None, 1, d), lambda i, j: (midx(i), 0, 1)),
                  pl.BlockSpec((None, 1, d), lambda i, j: (0, 0, 0)),
                  pl.BlockSpec((d, tn), lambda i, j: (0, j))],
        out_specs=pl.BlockSpec((tm, tn), lambda i, j: (i, j)),
        out_shape=jax.ShapeDtypeStruct((t, n), f32),
        scratch_shapes=[pltpu.VMEM((tm, d), bf16)],
        compiler_params=_cp(("arbitrary", "arbitrary")),
        name="in_proj",
    )(x, mod, mod, nw, w_in)


def _conv_body(x_ref, p_ref, n_ref, w_ref, o_ref, *, n_ctx_blk, lat_blk_per_seq, ctx_blk_per_seq):
    i = pl.program_id(0)
    tm = x_ref.shape[0]
    in_ctx = i < n_ctx_blk
    jc = i % ctx_blk_per_seq
    jl = (i - n_ctx_blk) % lat_blk_per_seq
    is_start = jnp.where(in_ctx, jc == 0, jl == 0)
    is_end = jnp.where(in_ctx, jc == ctx_blk_per_seq - 1, jl == lat_blk_per_seq - 1)
    x = x_ref[...]
    prev = jnp.where(is_start, 0.0, p_ref[7:8, :])
    nxt = jnp.where(is_end, 0.0, n_ref[0:1, :])
    row = lax.broadcasted_iota(jnp.int32, (tm, 1), 0)
    xm = jnp.where(row == 0, prev, pltpu.roll(x, 1, 0))
    xp = jnp.where(row == tm - 1, nxt, pltpu.roll(x, tm - 1, 0))
    w = w_ref[...]
    o_ref[...] = xm * w[0:1, :] + x * w[1:2, :] + xp * w[2:3, :]


def _short_conv(u, wconv, rows, col0, tm=256, ct=1280):
    t = u.shape[0]
    c = wconv.shape[1]
    cb0 = col0 // ct
    hb = tm // 8
    nhb = t // 8
    body = functools.partial(_conv_body, n_ctx_blk=rows.t_ctx // tm,
                             lat_blk_per_seq=rows.l_lat // tm, ctx_blk_per_seq=rows.l_ctx // tm)
    return pl.pallas_call(
        body,
        grid=(t // tm, c // ct),
        in_specs=[pl.BlockSpec((tm, ct), lambda i, j: (i, cb0 + j)),
                  pl.BlockSpec((8, ct), lambda i, j: (jnp.maximum(i * hb - 1, 0), cb0 + j)),
                  pl.BlockSpec((8, ct), lambda i, j: (jnp.minimum((i + 1) * hb, nhb - 1), cb0 + j)),
                  pl.BlockSpec((3, ct), lambda i, j: (0, j))],
        out_specs=pl.BlockSpec((tm, ct), lambda i, j: (i, j)),
        out_shape=jax.ShapeDtypeStruct((t, c), f32),
        compiler_params=_cp(("arbitrary", "arbitrary")),
        name="short_conv",
    )(u, u, u, wconv)


def _softmax_pv(s_list, v_list, sink_col):
    m = sink_col
    for s in s_list:
        m = jnp.maximum(m, jnp.max(s, axis=-1, keepdims=True))
    den = jnp.exp(sink_col - m)
    acc = None
    for s, v in zip(s_list, v_list):
        p = jnp.exp(s - m)
        den = den + jnp.sum(p, axis=-1, keepdims=True)
        pv = _dot(p.astype(bf16), v)
        acc = pv if acc is None else acc + pv
    return acc / den


def _attn_ctx_body(sink_ref, q_ref, k_ref, v_ref, o_ref):
    l = q_ref.shape[0]
    scale = HEAD_DIM ** -0.5
    for n in range(N_KV_HEADS):
        kn = k_ref[:, n * HEAD_DIM:(n + 1) * HEAD_DIM].astype(bf16)
        vn = v_ref[:, n * HEAD_DIM:(n + 1) * HEAD_DIM].astype(bf16)
        for g in range(ATT_GROUP):
            h = n * ATT_GROUP + g
            q = (q_ref[:, h * HEAD_DIM:(h + 1) * HEAD_DIM] * scale).astype(bf16)
            s = _dot_nt(q, kn)
            sink_col = jnp.full((l, 1), sink_ref[h], f32)
            o_ref[:, h * HEAD_DIM:(h + 1) * HEAD_DIM] = _softmax_pv([s], [vn], sink_col)


def _attn_ctx(u, sink, rows):
    b, l = rows.b_ctx, rows.l_ctx
    return pl.pallas_call(
        _attn_ctx_body,
        grid=(b,),
        in_specs=[pl.BlockSpec(memory_space=pltpu.SMEM),
                  pl.BlockSpec((l, ATT_WIDTH), lambda i: (i, 0)),
                  pl.BlockSpec((l, ATT_KV), lambda i: (i, ATT_WIDTH // ATT_KV)),
                  pl.BlockSpec((l, ATT_KV), lambda i: (i, ATT_WIDTH // ATT_KV + 1))],
        out_specs=pl.BlockSpec((l, ATT_WIDTH), lambda i: (i, 0)),
        out_shape=jax.ShapeDtypeStruct((b * l, ATT_WIDTH), f32),
        compiler_params=_cp(("arbitrary",)),
        name="attn_ctx",
    )(sink, u, u, u)


def _rope(x, cos, sin_signed):
    w = x.shape[1]
    lane = lax.broadcasted_iota(jnp.int32, (1, w), 1)
    swapped = jnp.where((lane % 32) < 16, pltpu.roll(x, w - 16, 1), pltpu.roll(x, 16, 1))
    return x * cos + swapped * sin_signed


def _attn_lat_body(sink_ref, q_ref, k_ref, v_ref, kc_ref, vc_ref, cos_ref, sin_ref, o_ref,
                   q_scr, k_scr, v_scr):
    l = q_ref.shape[0]
    blk = WINDOW
    nb = l // blk
    scale = HEAD_DIM ** -0.5
    cos = cos_ref[...]
    sin = sin_ref[...]
    q_scr[...] = (_rope(q_ref[...], cos, sin) * scale).astype(bf16)
    zpad = jnp.zeros((blk, ATT_KV), bf16)
    k_scr[0:blk, :] = zpad
    k_scr[blk + l:, :] = zpad
    v_scr[0:blk, :] = zpad
    v_scr[blk + l:, :] = zpad
    k_scr[blk:blk + l, :] = _rope(k_ref[...], cos[:, :ATT_KV], sin[:, :ATT_KV]).astype(bf16)
    v_scr[blk:blk + l, :] = v_ref[...].astype(bf16)
    r = lax.broadcasted_iota(jnp.int32, (ATT_GROUP * blk, 3 * blk), 0) % blk
    c = lax.broadcasted_iota(jnp.int32, (ATT_GROUP * blk, 3 * blk), 1)
    band = (c - r >= 0) & (c - r <= 2 * WINDOW)
    for i in range(nb):
        kpos = (i - 1) * blk + c
        mask = band & (kpos >= 0) & (kpos < l)
        for n in range(N_KV_HEADS):
            cs = slice(n * HEAD_DIM, (n + 1) * HEAD_DIM)
            kl = k_scr[i * blk:(i + 3) * blk, cs]
            vl = v_scr[i * blk:(i + 3) * blk, cs]
            kc = kc_ref[:, cs].astype(bf16)
            vc = vc_ref[:, cs].astype(bf16)
            qs = jnp.concatenate(
                [q_scr[i * blk:(i + 1) * blk, (n * ATT_GROUP + g) * HEAD_DIM:(n * ATT_GROUP + g + 1) * HEAD_DIM]
                 for g in range(ATT_GROUP)], axis=0)
            s_loc = jnp.where(mask, _dot_nt(qs, kl), NEG_INF)
            s_ctx = _dot_nt(qs, kc)
            sink_col = jnp.concatenate(
                [jnp.full((blk, 1), sink_ref[n * ATT_GROUP + g], f32) for g in range(ATT_GROUP)], axis=0)
            o = _softmax_pv([s_loc, s_ctx], [vl, vc], sink_col)
            for g in range(ATT_GROUP):
                h = n * ATT_GROUP + g
                o_ref[i * blk:(i + 1) * blk, h * HEAD_DIM:(h + 1) * HEAD_DIM] = o[g * blk:(g + 1) * blk, :]


def _attn_lat(u, kc, vc, sink, cos, sin, rows):
    b, l = rows.b_lat, rows.l_lat
    rb0 = rows.t_ctx // l
    lc = kc.shape[1]
    return pl.pallas_call(
        _attn_lat_body,
        grid=(b,),
        in_specs=[pl.BlockSpec(memory_space=pltpu.SMEM),
                  pl.BlockSpec((l, ATT_WIDTH), lambda i: (rb0 + i, 0)),
                  pl.BlockSpec((l, ATT_KV), lambda i: (rb0 + i, ATT_WIDTH // ATT_KV)),
                  pl.BlockSpec((l, ATT_KV), lambda i: (rb0 + i, ATT_WIDTH // ATT_KV + 1)),
                  pl.BlockSpec((None, lc, ATT_KV), lambda i: (i, 0, 0)),
                  pl.BlockSpec((None, lc, ATT_KV), lambda i: (i, 0, 0)),
                  pl.BlockSpec((l, ATT_WIDTH), lambda i: (0, 0)),
                  pl.BlockSpec((l, ATT_WIDTH), lambda i: (0, 0))],
        out_specs=pl.BlockSpec((l, ATT_WIDTH), lambda i: (i, 0)),
        out_shape=jax.ShapeDtypeStruct((b * l, ATT_WIDTH), f32),
        scratch_shapes=[pltpu.VMEM((l, ATT_WIDTH), bf16),
                        pltpu.VMEM((l + 2 * WINDOW, ATT_KV), bf16),
                        pltpu.VMEM((l + 2 * WINDOW, ATT_KV), bf16)],
        compiler_params=_cp(("arbitrary",)),
        name="attn_lat",
    )(sink, u, u, u, kc, vc, cos, sin)


def _rope_tables(l, grid_w):
    half = HEAD_DIM // 2
    pos = jnp.arange(l)
    rowp = (pos // grid_w).astype(f32)
    colp = (pos % grid_w).astype(f32)
    freqs = ROPE_THETA ** (-jnp.arange(0, half, 2, dtype=f32) / half)
    ar = rowp[:, None] * freqs[None, :]
    ac = colp[:, None] * freqs[None, :]
    cos = jnp.concatenate([jnp.cos(ar), jnp.cos(ar), jnp.cos(ac), jnp.cos(ac)], axis=-1)
    sin = jnp.concatenate([-jnp.sin(ar), jnp.sin(ar), -jnp.sin(ac), jnp.sin(ac)], axis=-1)
    return jnp.tile(cos, (1, N_ATT_HEADS)), jnp.tile(sin, (1, N_ATT_HEADS))


def _dft_tables(l):
    k = jnp.arange(l, dtype=jnp.int32)[:, None]
    s = jnp.arange(l, dtype=jnp.int32)[None, :]
    m = ((2 * k + 1) * s) % (4 * l)
    theta = m.astype(f32) * (math.pi / (2 * l))
    return jnp.cos(theta), jnp.sin(theta)


def _hy_feat(l):
    t = jnp.arange(l, dtype=f32)
    t01 = (t / max(l - 1, 1))[:, None]
    bands = jnp.linspace(1e-4, HY_BANDS - 1, HY_BANDS, dtype=f32)
    ang = (2.0 * math.pi / l) * t[:, None] * bands[None, :]
    feat = jnp.concatenate([t01, jnp.cos(ang), -jnp.sin(ang)], axis=-1)
    return jnp.pad(feat, ((0, 0), (0, 128 - HY_EMB))), t01


def _hy_filter_body(feat_ref, t01_ref, f1_ref, b1_ref, f2_ref, b2_ref, f3_ref, dec_ref, cm_ref, sm_ref,
                    hr_ref, hi_ref):
    h = jnp.sin(_dot_hi(feat_ref[...], f1_ref[...]) + b1_ref[...])
    h = jnp.sin(_dot_hi(h, f2_ref[...]) + b2_ref[...])
    h = _dot_hi(h, f3_ref[...])
    h = h * (jnp.exp(-t01_ref[...] * jnp.abs(dec_ref[...])) + HY_MOD_SHIFT)
    l = h.shape[0]
    fwd = h[:, :HY_CH]
    row = lax.broadcasted_iota(jnp.int32, (l, 1), 0)
    bwd = jnp.where(row == 0, 0.0, h[:, HY_CH:])
    hr_ref[...] = _dot_hi(cm_ref[...], fwd + bwd)
    hi_ref[...] = _dot_hi(sm_ref[...], bwd - fwd)


def _hy_filter(l, feat, t01, f1p, b1, f2, b2, f3, dec, cm, sm):
    args = (feat, t01, f1p, b1, f2, b2, f3, dec, cm, sm)
    return pl.pallas_call(
        _hy_filter_body,
        grid=(1,),
        in_specs=[pl.BlockSpec(a.shape, lambda i, nd=a.ndim: (0,) * nd) for a in args],
        out_specs=[pl.BlockSpec((l, HY_CH), lambda i: (0, 0))] * 2,
        out_shape=[jax.ShapeDtypeStruct((l, HY_CH), f32)] * 2,
        compiler_params=_cp(("arbitrary",)),
        name="hy_filter",
    )(*args)


def _hy_conv_body(x0_ref, x1_ref, v_ref, hr_ref, hi_ref, skip_ref, cm_ref, sm_ref, cmt_ref, smt_ref, o_ref):
    l = v_ref.shape[0]
    z = v_ref[...] * x1_ref[...]
    zb = z.astype(bf16)
    zr = _dot(cm_ref[...], zb)
    zs = _dot(sm_ref[...], zb)
    hr = hr_ref[...]
    hi = hi_ref[...]
    yr = (zr * hr + zs * hi).astype(bf16)
    yi = (zr * hi - zs * hr).astype(bf16)
    y = (_dot(cmt_ref[...], yr) - _dot(smt_ref[...], yi)) * (1.0 / l)
    o_ref[...] = (y + skip_ref[...] * z) * x0_ref[...]


def _hy_conv(uc, hr, hi, skip, cm, sm, cmt, smt, b, l, rb0):
    full = lambda a: pl.BlockSpec(a.shape, lambda i, nd=a.ndim: (0,) * nd)
    return pl.pallas_call(
        _hy_conv_body,
        grid=(b,),
        in_specs=[pl.BlockSpec((l, HY_CH), lambda i: (rb0 + i, 0)),
                  pl.BlockSpec((l, HY_CH), lambda i: (rb0 + i, 1)),
                  pl.BlockSpec((l, HY_CH), lambda i: (rb0 + i, 2)),
                  full(hr), full(hi), full(skip), full(cm), full(sm), full(cmt), full(smt)],
        out_specs=pl.BlockSpec((l, HY_CH), lambda i: (i, 0)),
        out_shape=jax.ShapeDtypeStruct((b * l, HY_CH), f32),
        compiler_params=_cp(("arbitrary",)),
        name="hy_conv",
    )(uc, uc, uc, hr, hi, skip, cm, sm, cmt, smt)


def _head_sum_matrix(scale=1.0):
    i = np.arange(RWKV_WIDTH) // HEAD_DIM
    return jnp.asarray((i[:, None] == i[None, :]).astype(np.float32) * scale)


def _rw_prep_body(lo_ref, k_ref, wl_ref, w0_ref, a0_ref, kk_ref_, hs_ref, kk_o, lw_o, a_o, g_o):
    x = lo_ref[...]
    lane = lax.broadcasted_iota(jnp.int32, x.shape, 1)
    act = jnp.where(lane < 2 * DECAY_LORA, jnp.tanh(x),
                    jnp.where(lane < 2 * DECAY_LORA + 2 * AAA_LORA, x, jax.nn.sigmoid(x)))
    lo = _dot(act.astype(bf16), wl_ref[...])
    c = RWKV_WIDTH
    for d in range(2):
        zneg = -(w0_ref[d:d + 1, :] + lo[:, d * c:(d + 1) * c])
        softplus = jnp.maximum(zneg, 0.0) + jnp.log(1.0 + jnp.exp(-jnp.abs(zneg)))
        w_log = -softplus - 0.5
        lw_o[d] = -jnp.exp(w_log)
        a_o[d] = jax.nn.sigmoid(a0_ref[d:d + 1, :] + lo[:, (2 + d) * c:(3 + d) * c])
    g_o[...] = lo[:, 4 * c:5 * c]
    kk = k_ref[...] * kk_ref_[...]
    ss = _dot_hi(kk * kk, hs_ref[...])
    kk_o[...] = kk / jnp.maximum(jnp.sqrt(ss), 1e-12)


def _rw_prep(u, uc, w_lora, w0, a0, k_k, hsum, tm=256):
    t = u.shape[0]
    c = RWKV_WIDTH
    full = lambda a: pl.BlockSpec(a.shape, lambda i, nd=a.ndim: (0,) * nd)
    lora_cb = (u.shape[1] - LORA_IN) // LORA_IN
    k_cb = (3 * HY_CH) // c + 1
    return pl.pallas_call(
        _rw_prep_body,
        grid=(t // tm,),
        in_specs=[pl.BlockSpec((tm, LORA_IN), lambda i: (i, lora_cb)),
                  pl.BlockSpec((tm, c), lambda i: (i, k_cb)),
                  full(w_lora), full(w0), full(a0), full(k_k), full(hsum)],
        out_specs=[pl.BlockSpec((tm, c), lambda i: (i, 0)),
                   pl.BlockSpec((2, tm, c), lambda i: (0, i, 0)),
                   pl.BlockSpec((2, tm, c), lambda i: (0, i, 0)),
                   pl.BlockSpec((tm, c), lambda i: (i, 0))],
        out_shape=[jax.ShapeDtypeStruct((t, c), f32),
                   jax.ShapeDtypeStruct((2, t, c), f32),
                   jax.ShapeDtypeStruct((2, t, c), f32),
                   jax.ShapeDtypeStruct((t, c), f32)],
        compiler_params=_cp(("arbitrary",)),
        name="rw_prep",
    )(u, uc, w_lora, w0, a0, k_k, hsum)


def _split3(x):
    hi = x.astype(bf16)
    r1 = x - hi.astype(f32)
    mid = r1.astype(bf16)
    lo = (r1 - mid.astype(f32)).astype(bf16)
    return hi, mid, lo


def _rw_scan_body(rb_ref, first_ref, seq_ref,
                  r_ref, k_ref, v_ref, kk_ref, lw_ref, a_ref, ka_ref, s0_ref,
                  y_ref, sf_ref, s_scr):
    d = pl.program_id(0)
    s = pl.program_id(1)
    cdim = CHUNK
    heads = range(N_RWKV_HEADS)

    @pl.when(first_ref[s] == 1)
    def _():
        s_scr[...] = s0_ref[...]

    row = lax.broadcasted_iota(jnp.int32, (cdim, cdim), 0)
    col = lax.broadcasted_iota(jnp.int32, (cdim, cdim), 1)
    lag = jnp.where(d == 0, row - col, col - row)
    incl = lag >= 0
    strict = lag > 0
    tri = incl.astype(bf16)

    lw = lw_ref[...]
    cs = sum(_dot(tri, part) for part in _split3(lw))
    tot = jnp.sum(lw, axis=0, keepdims=True)
    p_in = jnp.exp(cs)
    p_ex = jnp.exp(cs - lw)
    p_inv = jnp.exp(-cs)
    p_rest = jnp.exp(tot - cs)
    p_all = jnp.exp(tot)
    a = a_ref[...]
    kk = kk_ref[...]
    beta = kk * a
    kdir = k_ref[...] * (1.0 + (a - 1.0) * ka_ref[...])
    kap_b = (kk * p_ex).astype(bf16)
    r_b = (r_ref[...] * p_in).astype(bf16)
    beta_b = (beta * p_inv).astype(bf16)
    kdir_b = (kdir * p_inv).astype(bf16)
    beta_e = (beta * p_rest).astype(bf16)
    kdir_e = (kdir * p_rest).astype(bf16)
    v_b = v_ref[...].astype(bf16)

    def hs(h):
        return slice(h * HEAD_DIM, (h + 1) * HEAD_DIM)

    s0 = [s_scr[h] for h in heads]
    s0_b = [x.astype(bf16) for x in s0]
    m = [_dot_nt(jnp.concatenate([kap_b[:, hs(h)], r_b[:, hs(h)]], axis=0),
                 jnp.concatenate([beta_b[:, hs(h)], kdir_b[:, hs(h)]], axis=0)) for h in heads]
    x = [jnp.where(strict, -m[h][:cdim, :cdim], 0.0).astype(bf16) for h in heads]
    a_ak = [jnp.where(strict, m[h][:cdim, cdim:], 0.0).astype(bf16) for h in heads]
    m_rb = [jnp.where(incl, m[h][cdim:, :cdim], 0.0).astype(bf16) for h in heads]
    m_rk = [jnp.where(incl, m[h][cdim:, cdim:], 0.0).astype(bf16) for h in heads]
    uu = [_dot_nt(kap_b[:, hs(h)], s0_b[h]) + _dot(a_ak[h], v_b[:, hs(h)]) for h in heads]
    n_sq = int(math.log2(cdim))
    for it in range(n_sq):
        uu = [uu[h] + _dot(x[h], uu[h].astype(bf16)) for h in heads]
        if it + 1 < n_sq:
            x = [_dot(x[h], x[h]).astype(bf16) for h in heads]
    u_b = [(-uu[h]).astype(bf16) for h in heads]
    y = [_dot_nt(r_b[:, hs(h)], s0_b[h]) + _dot(m_rb[h], u_b[h]) + _dot(m_rk[h], v_b[:, hs(h)]) for h in heads]
    y_ref[...] = jnp.concatenate(y, axis=1)
    for h in heads:
        s_new = (s0[h] * p_all[:, hs(h)] + _dot_tn(u_b[h], beta_e[:, hs(h)])
                 + _dot_tn(v_b[:, hs(h)], kdir_e[:, hs(h)]))
        s_scr[h] = s_new
        sf_ref[h] = s_new


def _rw_scan(uc, kk, lw, a, k_a, h0, tabs, n_steps):
    c = RWKV_WIDTH
    t = uc.shape[0]
    nseq = h0.shape[0]
    rkv_cb = (3 * HY_CH) // c
    grid_spec = pltpu.PrefetchScalarGridSpec(
        num_scalar_prefetch=3,
        grid=(2, n_steps),
        in_specs=[pl.BlockSpec((CHUNK, c), lambda d, s, rb, fi, sq: (rb[d, s], rkv_cb)),
                  pl.BlockSpec((CHUNK, c), lambda d, s, rb, fi, sq: (rb[d, s], rkv_cb + 1)),
                  pl.BlockSpec((CHUNK, c), lambda d, s, rb, fi, sq: (rb[d, s], rkv_cb + 2)),
                  pl.BlockSpec((CHUNK, c), lambda d, s, rb, fi, sq: (rb[d, s], 0)),
                  pl.BlockSpec((None, CHUNK, c), lambda d, s, rb, fi, sq: (d, rb[d, s], 0)),
                  pl.BlockSpec((None, CHUNK, c), lambda d, s, rb, fi, sq: (d, rb[d, s], 0)),
                  pl.BlockSpec((1, c), lambda d, s, rb, fi, sq: (0, 0)),
                  pl.BlockSpec((None, None, N_RWKV_HEADS, HEAD_DIM, HEAD_DIM),
                               lambda d, s, rb, fi, sq: (sq[s], d, 0, 0, 0))],
        out_specs=[pl.BlockSpec((None, CHUNK, c), lambda d, s, rb, fi, sq: (d, rb[d, s], 0)),
                   pl.BlockSpec((None, None, N_RWKV_HEADS, HEAD_DIM, HEAD_DIM),
                                lambda d, s, rb, fi, sq: (sq[s], d, 0, 0, 0))],
        scratch_shapes=[pltpu.VMEM((N_RWKV_HEADS, HEAD_DIM, HEAD_DIM), f32)],
    )
    return pl.pallas_call(
        _rw_scan_body,
        grid_spec=grid_spec,
        out_shape=[jax.ShapeDtypeStruct((2, t, c), f32),
                   jax.ShapeDtypeStruct((nseq, 2, N_RWKV_HEADS, HEAD_DIM, HEAD_DIM), f32)],
        compiler_params=_cp(("arbitrary", "arbitrary")),
        name="rw_scan",
    )(*tabs, uc, uc, uc, kk, lw, a, k_a, h0)


def _scan_tables(rows):
    rb = [[], []]
    first, seq = [], []
    base = 0
    sidx = 0
    for b, l in ((rows.b_ctx, rows.l_ctx), (rows.b_lat, rows.l_lat)):
        nc = l // CHUNK
        for i in range(b):
            for c in range(nc):
                rb[0].append(base + c)
                rb[1].append(base + nc - 1 - c)
                first.append(1 if c == 0 else 0)
                seq.append(sidx)
            base += nc
            sidx += 1
    return (jnp.asarray(np.array(rb, np.int32)), jnp.asarray(np.array(first, np.int32)),
            jnp.asarray(np.array(seq, np.int32))), len(first)


def _rw_post_body(y_ref, r_ref, k_ref, v_ref, g_ref, rk_ref, gw_ref, gb_ref, hm_ref, hs_ref, o_ref):
    y = y_ref[0] + y_ref[1]
    mu = _dot_hi(y, hm_ref[...])
    yc = y - mu
    var = _dot_hi(yc * yc, hm_ref[...])
    yn = yc * lax.rsqrt(var + RWKV_GN_EPS) * gw_ref[...] + gb_ref[...]
    bonus = _dot_hi(r_ref[...] * k_ref[...] * rk_ref[...], hs_ref[...]) * v_ref[...]
    o_ref[...] = (yn + bonus) * g_ref[...]


def _rw_post(y, uc, g, r_k, gn_w, gn_b, hmean, hsum, tm=256):
    t = uc.shape[0]
    c = RWKV_WIDTH
    rkv_cb = (3 * HY_CH) // c
    full = lambda a: pl.BlockSpec(a.shape, lambda i, nd=a.ndim: (0,) * nd)
    return pl.pallas_call(
        _rw_post_body,
        grid=(t // tm,),
        in_specs=[pl.BlockSpec((2, tm, c), lambda i: (0, i, 0)),
                  pl.BlockSpec((tm, c), lambda i: (i, rkv_cb)),
                  pl.BlockSpec((tm, c), lambda i: (i, rkv_cb + 1)),
                  pl.BlockSpec((tm, c), lambda i: (i, rkv_cb + 2)),
                  pl.BlockSpec((tm, c), lambda i: (i, 0)),
                  full(r_k), full(gn_w), full(gn_b), full(hmean), full(hsum)],
        out_specs=pl.BlockSpec((tm, c), lambda i: (i, 0)),
        out_shape=jax.ShapeDtypeStruct((t, c), f32),
        compiler_params=_cp(("arbitrary",)),
        name="rw_post",
    )(y, uc, uc, uc, g, r_k, gn_w, gn_b, hmean, hsum)


def _outproj_body(att_ref, hy_ref, rw_ref, w_ref, x_ref, g_ref, nw_ref, o_ref):
    o = (_dot(att_ref[...].astype(bf16), w_ref[0:ATT_WIDTH, :])
         + _dot(hy_ref[...].astype(bf16), w_ref[ATT_WIDTH:ATT_WIDTH + HY_CH, :])
         + _dot(rw_ref[...].astype(bf16), w_ref[ATT_WIDTH + HY_CH:, :]))
    y = o * lax.rsqrt(jnp.mean(o * o, axis=-1, keepdims=True) + RMS_EPS) * nw_ref[...]
    o_ref[...] = x_ref[...] + g_ref[...] * y


def _out_proj(att, hy, rw, w_out, x, mod, nw, rows, tm=256):
    t, d = x.shape
    midx = rows.mod_index(tm)
    return pl.pallas_call(
        _outproj_body,
        grid=(t // tm,),
        in_specs=[pl.BlockSpec((tm, ATT_WIDTH), lambda i: (i, 0)),
                  pl.BlockSpec((tm, HY_CH), lambda i: (i, 0)),
                  pl.BlockSpec((tm, RWKV_WIDTH), lambda i: (i, 0)),
                  pl.BlockSpec(w_out.shape, lambda i: (0, 0)),
                  pl.BlockSpec((tm, d), lambda i: (i, 0)),
                  pl.BlockSpec((None, 1, d), lambda i: (midx(i), 0, 2)),
                  pl.BlockSpec((None, 1, d), lambda i: (1, 0, 0))],
        out_specs=pl.BlockSpec((tm, d), lambda i: (i, 0)),
        out_shape=jax.ShapeDtypeStruct((t, d), f32),
        compiler_params=_cp(("arbitrary",)),
        name="out_proj",
    )(att, hy, rw, w_out, x, mod, nw)


def _mlp_body(x_ref, sh_ref, sc_ref, g_ref, nw2_ref, nw3_ref, w1_ref, w2_ref, o_ref, h_scr, acc_scr):
    j = pl.program_id(1)

    @pl.when(j == 0)
    def _():
        h_scr[...] = _normmod(x_ref[...], nw2_ref[...], sc_ref[...], sh_ref[...]).astype(bf16)
        acc_scr[...] = jnp.zeros_like(acc_scr)

    a = jnp.maximum(_dot(h_scr[...], w1_ref[...]), 0.0)
    acc_scr[...] += _dot((a * a).astype(bf16), w2_ref[...])

    @pl.when(j == pl.num_programs(1) - 1)
    def _():
        f = acc_scr[...]
        y = f * lax.rsqrt(jnp.mean(f * f, axis=-1, keepdims=True) + RMS_EPS) * nw3_ref[...]
        o_ref[...] = x_ref[...] + g_ref[...] * y


def _mlp(x, mod, nw, w1, w2, rows, tm=512, tf=1024):
    t, d = x.shape
    dff = w1.shape[1]
    midx = rows.mod_index(tm)
    return pl.pallas_call(
        _mlp_body,
        grid=(t // tm, dff // tf),
        in_specs=[pl.BlockSpec((tm, d), lambda i, j: (i, 0)),
                  pl.BlockSpec((None, 1, d), lambda i, j: (midx(i), 0, 3)),
                  pl.BlockSpec((None, 1, d), lambda i, j: (midx(i), 0, 4)),
                  pl.BlockSpec((None, 1, d), lambda i, j: (midx(i), 0, 5)),
                  pl.BlockSpec((None, 1, d), lambda i, j: (2, 0, 0)),
                  pl.BlockSpec((None, 1, d), lambda i, j: (3, 0, 0)),
                  pl.BlockSpec((d, tf), lambda i, j: (0, j)),
                  pl.BlockSpec((tf, d), lambda i, j: (j, 0))],
        out_specs=pl.BlockSpec((tm, d), lambda i, j: (i, 0)),
        out_shape=jax.ShapeDtypeStruct((t, d), f32),
        scratch_shapes=[pltpu.VMEM((tm, d), bf16), pltpu.VMEM((tm, d), f32)],
        compiler_params=_cp(("arbitrary", "arbitrary")),
        name="mlp",
    )(x, mod, mod, mod, nw, nw, w1, w2)


def _lora_weight(w_up, a_up, g_up):
    c = RWKV_WIDTH
    w = jnp.zeros((LORA_IN, 5 * c), f32)
    r = 0
    for j, blk in enumerate((w_up[0], w_up[1], a_up[0], a_up[1], g_up)):
        w = w.at[r:r + blk.shape[0], j * c:(j + 1) * c].set(blk)
        r += blk.shape[0]
    return w.astype(bf16)


def kernel(x_prompt, x_sample, cache_k, cache_v, state_rwkv, c, c_ctx, ada_w, ada_b, norm_w, w_in, w_out,
           attn_sink, hy_short_w, hy_f1, hy_b1, hy_f2, hy_b2, hy_f3, hy_decay, hy_skip, rw_short_w, rw_w0,
           rw_w_up, rw_a0, rw_a_up, rw_g_up, rw_k_k, rw_k_a, rw_r_k, rw_gn_w, rw_gn_b, mlp_w1, mlp_w2):
    b_ctx, l_ctx, d = x_prompt.shape
    b_lat, l_lat, _ = x_sample.shape
    depth = ada_w.shape[0]
    grid_w = 64
    rows = _Rows(b_ctx, l_ctx, b_lat, l_lat)
    t = rows.t

    x = jnp.concatenate([x_prompt.reshape(rows.t_ctx, d), x_sample.reshape(rows.t_lat, d)], axis=0)
    cond16 = jnp.zeros((16, d), f32).at[0].set(c_ctx).at[1:1 + b_lat].set(c)
    mod_all = _modulation(cond16, ada_w, ada_b[:, None, :])

    cos_t, sin_t = _rope_tables(l_lat, grid_w)
    hsum = _head_sum_matrix()
    hmean = _head_sum_matrix(1.0 / HEAD_DIM)
    scan_tabs, n_steps = _scan_tables(rows)
    dft = {}
    for l in (l_ctx, l_lat):
        cm, sm = _dft_tables(l)
        feat, t01 = _hy_feat(l)
        dft[l] = dict(cm=cm, sm=sm, cmb=cm.astype(bf16), smb=sm.astype(bf16),
                      cmtb=cm.T.astype(bf16), smtb=sm.T.astype(bf16), feat=feat, t01=t01)

    new_k, new_v, new_s = [], [], []
    for l in range(depth):
        mod = mod_all[l].reshape(16, 1, N_MOD * d)
        nw = norm_w[l].reshape(4, 1, d)
        u = _in_proj(x, mod, nw, w_in[l].astype(bf16), rows)

        kv0 = ATT_WIDTH
        new_k.append(u[:rows.t_ctx, kv0:kv0 + ATT_KV].reshape(b_ctx, l_ctx, N_KV_HEADS, HEAD_DIM))
        new_v.append(u[:rows.t_ctx, kv0 + ATT_KV:kv0 + 2 * ATT_KV].reshape(b_ctx, l_ctx, N_KV_HEADS, HEAD_DIM))

        wconv = jnp.concatenate([hy_short_w[l], rw_short_w[l]], axis=1)
        uc = _short_conv(u, wconv, rows, ATT_WIDTH + 2 * ATT_KV)

        att_c = _attn_ctx(u, attn_sink[l], rows)
        att_l = _attn_lat(u, cache_k[:, l].reshape(b_lat, -1, ATT_KV), cache_v[:, l].reshape(b_lat, -1, ATT_KV),
                          attn_sink[l], cos_t, sin_t, rows)
        att = jnp.concatenate([att_c, att_l], axis=0)

        f1p = jnp.pad(hy_f1[l], ((0, 128 - HY_EMB), (0, 0)))
        hys = []
        for (bb, ll, rb0) in ((b_ctx, l_ctx, 0), (b_lat, l_lat, rows.t_ctx // l_lat)):
            tb = dft[ll]
            hr, hi = _hy_filter(ll, tb['feat'], tb['t01'], f1p, hy_b1[l][None], hy_f2[l], hy_b2[l][None],
                                hy_f3[l], hy_decay[l][None], tb['cm'], tb['sm'])
            hys.append(_hy_conv(uc, hr, hi, hy_skip[l], tb['cmb'], tb['smb'], tb['cmtb'], tb['smtb'],
                                bb, ll, rb0))
        hy = jnp.concatenate(hys, axis=0)

        w_lora = _lora_weight(rw_w_up[l], rw_a_up[l], rw_g_up[l])
        kk, lw, a, g = _rw_prep(u, uc, w_lora, rw_w0[l], rw_a0[l], rw_k_k[l][None], hsum)
        h0 = jnp.concatenate([jnp.zeros((b_ctx, 2, N_RWKV_HEADS, HEAD_DIM, HEAD_DIM), f32),
                              state_rwkv[:, l]], axis=0)
        y, hfin = _rw_scan(uc, kk, lw, a, rw_k_a[l][None], h0, scan_tabs, n_steps)
        new_s.append(hfin[:b_ctx])
        rw = _rw_post(y, uc, g, rw_r_k[l].reshape(1, RWKV_WIDTH), rw_gn_w[l][None], rw_gn_b[l][None],
                      hmean, hsum)

        x = _out_proj(att, hy, rw, w_out[l].astype(bf16), x, mod, nw, rows)
        x = _mlp(x, mod, nw, mlp_w1[l].astype(bf16), mlp_w2[l].astype(bf16), rows)

    y_p = x[:rows.t_ctx].reshape(b_ctx, l_ctx, d)
    y_s = x[rows.t_ctx:].reshape(b_lat, l_lat, d)
    return (y_p, y_s, jnp.stack(new_k, axis=1), jnp.stack(new_v, axis=1), jnp.stack(new_s, axis=1))
```

```python
import functools
import math

import numpy as np
import jax
import jax.numpy as jnp
from jax import lax
from jax.experimental import pallas as pl
from jax.experimental.pallas import tpu as pltpu

f32 = jnp.float32
bf16 = jnp.bfloat16

HEAD_DIM = 64
N_ATT_HEADS = 12
N_KV_HEADS = 4
ATT_GROUP = 3
ATT_WIDTH = 768
ATT_KV = 256
WINDOW = 128
ROPE_THETA = 10000.0
NEG_INF = -1e30
HY_CH = 512
HY_BANDS = 16
HY_EMB = 33
HY_FFN = 64
HY_MOD_SHIFT = 0.05
RWKV_WIDTH = 768
N_RWKV_HEADS = 12
DECAY_LORA = 96
AAA_LORA = 96
GATE_LORA = 256
RWKV_GN_EPS = 64e-5
N_MOD = 6
RMS_EPS = 1e-6
CHUNK = 64
LORA_IN = 2 * DECAY_LORA + 2 * AAA_LORA + GATE_LORA
CONV_CH = 3 * HY_CH + 3 * RWKV_WIDTH
HI = lax.Precision.HIGHEST
VMEM_LIMIT = 56 * 1024 * 1024


def _cp(sem, vmem=VMEM_LIMIT):
    return pltpu.CompilerParams(dimension_semantics=sem, vmem_limit_bytes=vmem)


def _dot(a, b):
    return jnp.dot(a, b, preferred_element_type=f32)


def _dot_nt(a, b):
    return lax.dot_general(a, b, (((1,), (1,)), ((), ())), preferred_element_type=f32)


def _dot_tn(a, b):
    return lax.dot_general(a, b, (((0,), (0,)), ((), ())), preferred_element_type=f32)


def _dot_hi(a, b):
    return jnp.dot(a, b, preferred_element_type=f32, precision=HI)


def _mod_body(c_ref, w_ref, b_ref, o_ref):
    c = c_ref[...]
    s = (c * jax.nn.sigmoid(c)).astype(bf16)
    o_ref[...] = _dot(s, w_ref[...].astype(bf16)) + b_ref[...]


def _modulation(cond16, ada_w, ada_b):
    depth, d, n = ada_w.shape
    tn = 1024
    return pl.pallas_call(
        _mod_body,
        grid=(depth, n // tn),
        in_specs=[pl.BlockSpec((16, d), lambda l, j: (0, 0)),
                  pl.BlockSpec((None, d, tn), lambda l, j: (l, 0, j)),
                  pl.BlockSpec((None, 1, tn), lambda l, j: (l, 0, j))],
        out_specs=pl.BlockSpec((None, 16, tn), lambda l, j: (l, 0, j)),
        out_shape=jax.ShapeDtypeStruct((depth, 16, n), f32),
        compiler_params=_cp(("arbitrary", "arbitrary")),
        name="modulation",
    )(cond16, ada_w, ada_b)


def _normmod(x, nw, sc, sh):
    y = x * lax.rsqrt(jnp.mean(x * x, axis=-1, keepdims=True) + RMS_EPS) * nw
    return y * (1.0 + sc) + sh


class _Rows:
    def __init__(self, b_ctx, l_ctx, b_lat, l_lat):
        self.b_ctx, self.l_ctx, self.b_lat, self.l_lat = b_ctx, l_ctx, b_lat, l_lat
        self.t_ctx = b_ctx * l_ctx
        self.t_lat = b_lat * l_lat
        self.t = self.t_ctx + self.t_lat

    def mod_index(self, tm):
        nctx = self.t_ctx // tm
        per = self.l_lat // tm

        def f(i):
            return jnp.where(i < nctx, 0, 1 + (i - nctx) // per)
        return f


def _inproj_body(x_ref, sh_ref, sc_ref, nw_ref, w_ref, o_ref, h_scr):
    @pl.when(pl.program_id(1) == 0)
    def _():
        h_scr[...] = _normmod(x_ref[...], nw_ref[...], sc_ref[...], sh_ref[...]).astype(bf16)
    o_ref[...] = _dot(h_scr[...], w_ref[...])


def _in_proj(x, mod, nw, w_in, rows, tm=1024, tn=1152):
    t, d = x.shape
    n = w_in.shape[1]
    midx = rows.mod_index(tm)
    return pl.pallas_call(
        _inproj_body,
        grid=(t // tm, n // tn),
        in_specs=[pl.BlockSpec((tm, d), lambda i, j: (i, 0)),
                  pl.BlockSpec((None, 1, d), lambda i, j: (midx(i), 0, 0)),
                  pl.BlockSpec((None, 1, d), lambda i, j: (midx(i), 0, 1)),
                  pl.BlockSpec((None, 1, d), lambda i, j: (0, 0, 0)),
                  pl.BlockSpec((d, tn), lambda i, j: (0, j))],
        out_specs=pl.BlockSpec((tm, tn), lambda i, j: (i, j)),
        out_shape=jax.ShapeDtypeStruct((t, n), f32),
        scratch_shapes=[pltpu.VMEM((tm, d), bf16)],
        compiler_params=_cp(("arbitrary", "arbitrary")),
        name="in_proj",
    )(x, mod, mod, nw, w_in)


def _conv_body(x_ref, p_ref, n_ref, w_ref, o_ref, *, n_ctx_blk, lat_blk_per_seq, ctx_blk_per_seq):
    i = pl.program_id(0)
    tm = x_ref.shape[0]
    in_ctx = i < n_ctx_blk
    jc = i % ctx_blk_per_seq
    jl = (i - n_ctx_blk) % lat_blk_per_seq
    is_start = jnp.where(in_ctx, jc == 0, jl == 0)
    is_end = jnp.where(in_ctx, jc == ctx_blk_per_seq - 1, jl == lat_blk_per_seq - 1)
    x = x_ref[...]
    prev = jnp.where(is_start, 0.0, p_ref[7:8, :])
    nxt = jnp.where(is_end, 0.0, n_ref[0:1, :])
    row = lax.broadcasted_iota(jnp.int32, (tm, 1), 0)
    xm = jnp.where(row == 0, prev, pltpu.roll(x, 1, 0))
    xp = jnp.where(row == tm - 1, nxt, pltpu.roll(x, tm - 1, 0))
    w = w_ref[...]
    o_ref[...] = xm * w[0:1, :] + x * w[1:2, :] + xp * w[2:3, :]


def _short_conv(u, wconv, rows, col0, tm=256, ct=1280):
    t = u.shape[0]
    c = wconv.shape[1]
    cb0 = col0 // ct
    hb = tm // 8
    nhb = t // 8
    body = functools.partial(_conv_body, n_ctx_blk=rows.t_ctx // tm,
                             lat_blk_per_seq=rows.l_lat // tm, ctx_blk_per_seq=rows.l_ctx // tm)
    return pl.pallas_call(
        body,
        grid=(t // tm, c // ct),
        in_specs=[pl.BlockSpec((tm, ct), lambda i, j: (i, cb0 + j)),
                  pl.BlockSpec((8, ct), lambda i, j: (jnp.maximum(i * hb - 1, 0), cb0 + j)),
                  pl.BlockSpec((8, ct), lambda i, j: (jnp.minimum((i + 1) * hb, nhb - 1), cb0 + j)),
                  pl.BlockSpec((3, ct), lambda i, j: (0, j))],
        out_specs=pl.BlockSpec((tm, ct), lambda i, j: (i, j)),
        out_shape=jax.ShapeDtypeStruct((t, c), f32),
        compiler_params=_cp(("arbitrary", "arbitrary")),
        name="short_conv",
    )(u, u, u, wconv)


def _softmax_pv(s_list, v_list, sink_col):
    m = sink_col
    for s in s_list:
        m = jnp.maximum(m, jnp.max(s, axis=-1, keepdims=True))
    den = jnp.exp(sink_col - m)
    acc = None
    for s, v in zip(s_list, v_list):
        p = jnp.exp(s - m)
        den = den + jnp.sum(p, axis=-1, keepdims=True)
        pv = _dot(p.astype(bf16), v)
        acc = pv if acc is None else acc + pv
    return acc / den


def _attn_ctx_body(sink_ref, q_ref, k_ref, v_ref, o_ref):
    l = q_ref.shape[0]
    scale = HEAD_DIM ** -0.5
    for n in range(N_KV_HEADS):
        kn = k_ref[:, n * HEAD_DIM:(n + 1) * HEAD_DIM].astype(bf16)
        vn = v_ref[:, n * HEAD_DIM:(n + 1) * HEAD_DIM].astype(bf16)
        for g in range(ATT_GROUP):
            h = n * ATT_GROUP + g
            q = (q_ref[:, h * HEAD_DIM:(h + 1) * HEAD_DIM] * scale).astype(bf16)
            s = _dot_nt(q, kn)
            sink_col = jnp.full((l, 1), sink_ref[h], f32)
            o_ref[:, h * HEAD_DIM:(h + 1) * HEAD_DIM] = _softmax_pv([s], [vn], sink_col)


def _attn_ctx(u, sink, rows):
    b, l = rows.b_ctx, rows.l_ctx
    return pl.pallas_call(
        _attn_ctx_body,
        grid=(b,),
        in_specs=[pl.BlockSpec(memory_space=pltpu.SMEM),
                  pl.BlockSpec((l, ATT_WIDTH), lambda i: (i, 0)),
                  pl.BlockSpec((l, ATT_KV), lambda i: (i, ATT_WIDTH // ATT_KV)),
                  pl.BlockSpec((l, ATT_KV), lambda i: (i, ATT_WIDTH // ATT_KV + 1))],
        out_specs=pl.BlockSpec((l, ATT_WIDTH), lambda i: (i, 0)),
        out_shape=jax.ShapeDtypeStruct((b * l, ATT_WIDTH), f32),
        compiler_params=_cp(("arbitrary",)),
        name="attn_ctx",
    )(sink, u, u, u)


def _rope(x, cos, sin_signed):
    w = x.shape[1]
    lane = lax.broadcasted_iota(jnp.int32, (1, w), 1)
    swapped = jnp.where((lane % 32) < 16, pltpu.roll(x, w - 16, 1), pltpu.roll(x, 16, 1))
    return x * cos + swapped * sin_signed


def _attn_lat_body(sink_ref, q_ref, k_ref, v_ref, kc_ref, vc_ref, cos_ref, sin_ref, o_ref,
                   q_scr, k_scr, v_scr):
    l = q_ref.shape[0]
    blk = WINDOW
    nb = l // blk
    scale = HEAD_DIM ** -0.5
    cos = cos_ref[...]
    sin = sin_ref[...]
    q_scr[...] = (_rope(q_ref[...], cos, sin) * scale).astype(bf16)
    zpad = jnp.zeros((blk, ATT_KV), bf16)
    k_scr[0:blk, :] = zpad
    k_scr[blk + l:, :] = zpad
    v_scr[0:blk, :] = zpad
    v_scr[blk + l:, :] = zpad
    k_scr[blk:blk + l, :] = _rope(k_ref[...], cos[:, :ATT_KV], sin[:, :ATT_KV]).astype(bf16)
    v_scr[blk:blk + l, :] = v_ref[...].astype(bf16)
    r = lax.broadcasted_iota(jnp.int32, (ATT_GROUP * blk, 3 * blk), 0) % blk
    c = lax.broadcasted_iota(jnp.int32, (ATT_GROUP * blk, 3 * blk), 1)
    band = (c - r >= 0) & (c - r <= 2 * WINDOW)
    for i in range(nb):
        kpos = (i - 1) * blk + c
        mask = band & (kpos >= 0) & (kpos < l)
        for n in range(N_KV_HEADS):
            cs = slice(n * HEAD_DIM, (n + 1) * HEAD_DIM)
            kl = k_scr[i * blk:(i + 3) * blk, cs]
            vl = v_scr[i * blk:(i + 3) * blk, cs]
            kc = kc_ref[:, cs].astype(bf16)
            vc = vc_ref[:, cs].astype(bf16)
            qs = jnp.concatenate(
                [q_scr[i * blk:(i + 1) * blk, (n * ATT_GROUP + g) * HEAD_DIM:(n * ATT_GROUP + g + 1) * HEAD_DIM]
                 for g in range(ATT_GROUP)], axis=0)
            s_loc = jnp.where(mask, _dot_nt(qs, kl), NEG_INF)
            s_ctx = _dot_nt(qs, kc)
            sink_col = jnp.concatenate(
                [jnp.full((blk, 1), sink_ref[n * ATT_GROUP + g], f32) for g in range(ATT_GROUP)], axis=0)
            o = _softmax_pv([s_loc, s_ctx], [vl, vc], sink_col)
            for g in range(ATT_GROUP):
                h = n * ATT_GROUP + g
                o_ref[i * blk:(i + 1) * blk, h * HEAD_DIM:(h + 1) * HEAD_DIM] = o[g * blk:(g + 1) * blk, :]


def _attn_lat(u, kc, vc, sink, cos, sin, rows):
    b, l = rows.b_lat, rows.l_lat
    rb0 = rows.t_ctx // l
    lc = kc.shape[1]
    return pl.pallas_call(
        _attn_lat_body,
        grid=(b,),
        in_specs=[pl.BlockSpec(memory_space=pltpu.SMEM),
                  pl.BlockSpec((l, ATT_WIDTH), lambda i: (rb0 + i, 0)),
                  pl.BlockSpec((l, ATT_KV), lambda i: (rb0 + i, ATT_WIDTH // ATT_KV)),
                  pl.BlockSpec((l, ATT_KV), lambda i: (rb0 + i, ATT_WIDTH // ATT_KV + 1)),
                  pl.BlockSpec((None, lc, ATT_KV), lambda i: (i, 0, 0)),
                  pl.BlockSpec((None, lc, ATT_KV), lambda i: (i, 0, 0)),
                  pl.BlockSpec((l, ATT_WIDTH), lambda i: (0, 0)),
                  pl.BlockSpec((l, ATT_WIDTH), lambda i: (0, 0))],
        out_specs=pl.BlockSpec((l, ATT_WIDTH), lambda i: (i, 0)),
        out_shape=jax.ShapeDtypeStruct((b * l, ATT_WIDTH), f32),
        scratch_shapes=[pltpu.VMEM((l, ATT_WIDTH), bf16),
                        pltpu.VMEM((l + 2 * WINDOW, ATT_KV), bf16),
                        pltpu.VMEM((l + 2 * WINDOW, ATT_KV), bf16)],
        compiler_params=_cp(("arbitrary",)),
        name="attn_lat",
    )(sink, u, u, u, kc, vc, cos, sin)


def _rope_tables(l, grid_w):
    half = HEAD_DIM // 2
    pos = jnp.arange(l)
    rowp = (pos // grid_w).astype(f32)
    colp = (pos % grid_w).astype(f32)
    freqs = ROPE_THETA ** (-jnp.arange(0, half, 2, dtype=f32) / half)
    ar = rowp[:, None] * freqs[None, :]
    ac = colp[:, None] * freqs[None, :]
    cos = jnp.concatenate([jnp.cos(ar), jnp.cos(ar), jnp.cos(ac), jnp.cos(ac)], axis=-1)
    sin = jnp.concatenate([-jnp.sin(ar), jnp.sin(ar), -jnp.sin(ac), jnp.sin(ac)], axis=-1)
    return jnp.tile(cos, (1, N_ATT_HEADS)), jnp.tile(sin, (1, N_ATT_HEADS))


def _dft_tables(l):
    k = jnp.arange(l, dtype=jnp.int32)[:, None]
    s = jnp.arange(l, dtype=jnp.int32)[None, :]
    m = ((2 * k + 1) * s) % (4 * l)
    theta = m.astype(f32) * (math.pi / (2 * l))
    return jnp.cos(theta), jnp.sin(theta)


def _hy_feat(l):
    t = jnp.arange(l, dtype=f32)
    t01 = (t / max(l - 1, 1))[:, None]
    bands = jnp.linspace(1e-4, HY_BANDS - 1, HY_BANDS, dtype=f32)
    ang = (2.0 * math.pi / l) * t[:, None] * bands[None, :]
    feat = jnp.concatenate([t01, jnp.cos(ang), -jnp.sin(ang)], axis=-1)
    return jnp.pad(feat, ((0, 0), (0, 128 - HY_EMB))), t01


def _hy_filter_body(feat_ref, t01_ref, f1_ref, b1_ref, f2_ref, b2_ref, f3_ref, dec_ref, cm_ref, sm_ref,
                    hr_ref, hi_ref):
    h = jnp.sin(_dot_hi(feat_ref[...], f1_ref[...]) + b1_ref[...])
    h = jnp.sin(_dot_hi(h, f2_ref[...]) + b2_ref[...])
    h = _dot_hi(h, f3_ref[...])
    h = h * (jnp.exp(-t01_ref[...] * jnp.abs(dec_ref[...])) + HY_MOD_SHIFT)
    l = h.shape[0]
    fwd = h[:, :HY_CH]
    row = lax.broadcasted_iota(jnp.int32, (l, 1), 0)
    bwd = jnp.where(row == 0, 0.0, h[:, HY_CH:])
    hr_ref[...] = _dot_hi(cm_ref[...], fwd + bwd)
    hi_ref[...] = _dot_hi(sm_ref[...], bwd - fwd)


def _hy_filter(l, feat, t01, f1p, b1, f2, b2, f3, dec, cm, sm):
    args = (feat, t01, f1p, b1, f2, b2, f3, dec, cm, sm)
    return pl.pallas_call(
        _hy_filter_body,
        grid=(1,),
        in_specs=[pl.BlockSpec(a.shape, lambda i, nd=a.ndim: (0,) * nd) for a in args],
        out_specs=[pl.BlockSpec((l, HY_CH), lambda i: (0, 0))] * 2,
        out_shape=[jax.ShapeDtypeStruct((l, HY_CH), f32)] * 2,
        compiler_params=_cp(("arbitrary",)),
        name="hy_filter",
    )(*args)


def _hy_conv_body(x0_ref, x1_ref, v_ref, hr_ref, hi_ref, skip_ref, cm_ref, sm_ref, cmt_ref, smt_ref, o_ref):
    l = v_ref.shape[0]
    z = v_ref[...] * x1_ref[...]
    zb = z.astype(bf16)
    zr = _dot(cm_ref[...], zb)
    zs = _dot(sm_ref[...], zb)
    hr = hr_ref[...]
    hi = hi_ref[...]
    yr = (zr * hr + zs * hi).astype(bf16)
    yi = (zr * hi - zs * hr).astype(bf16)
    y = (_dot(cmt_ref[...], yr) - _dot(smt_ref[...], yi)) * (1.0 / l)
    o_ref[...] = (y + skip_ref[...] * z) * x0_ref[...]


def _hy_conv(uc, hr, hi, skip, cm, sm, cmt, smt, b, l, rb0):
    full = lambda a: pl.BlockSpec(a.shape, lambda i, nd=a.ndim: (0,) * nd)
    return pl.pallas_call(
        _hy_conv_body,
        grid=(b,),
        in_specs=[pl.BlockSpec((l, HY_CH), lambda i: (rb0 + i, 0)),
                  pl.BlockSpec((l, HY_CH), lambda i: (rb0 + i, 1)),
                  pl.BlockSpec((l, HY_CH), lambda i: (rb0 + i, 2)),
                  full(hr), full(hi), full(skip), full(cm), full(sm), full(cmt), full(smt)],
        out_specs=pl.BlockSpec((l, HY_CH), lambda i: (i, 0)),
        out_shape=jax.ShapeDtypeStruct((b * l, HY_CH), f32),
        compiler_params=_cp(("arbitrary",)),
        name="hy_conv",
    )(uc, uc, uc, hr, hi, skip, cm, sm, cmt, smt)


def _head_sum_matrix(scale=1.0):
    i = np.arange(RWKV_WIDTH) // HEAD_DIM
    return jnp.asarray((i[:, None] == i[None, :]).astype(np.float32) * scale).astype(bf16)


def _dot2(x, m):
    hi = x.astype(bf16)
    lo = (x - hi.astype(f32)).astype(bf16)
    return _dot(hi, m) + _dot(lo, m)


def _rw_prep_body(lo_ref, k_ref, wl_ref, w0_ref, a0_ref, kk_ref_, hs_ref, kk_o, lw_o, a_o, g_o):
    x = lo_ref[...]
    lane = lax.broadcasted_iota(jnp.int32, x.shape, 1)
    act = jnp.where(lane < 2 * DECAY_LORA, jnp.tanh(x),
                    jnp.where(lane < 2 * DECAY_LORA + 2 * AAA_LORA, x, jax.nn.sigmoid(x)))
    lo = _dot(act.astype(bf16), wl_ref[...])
    c = RWKV_WIDTH
    for d in range(2):
        zneg = -(w0_ref[d:d + 1, :] + lo[:, d * c:(d + 1) * c])
        softplus = jnp.maximum(zneg, 0.0) + jnp.log(1.0 + jnp.exp(-jnp.abs(zneg)))
        w_log = -softplus - 0.5
        lw_o[d] = -jnp.exp(w_log)
        a_o[d] = jax.nn.sigmoid(a0_ref[d:d + 1, :] + lo[:, (2 + d) * c:(3 + d) * c])
    g_o[...] = lo[:, 4 * c:5 * c]
    kk = k_ref[...] * kk_ref_[...]
    ss = _dot2(kk * kk, hs_ref[...])
    kk_o[...] = kk / jnp.maximum(jnp.sqrt(ss), 1e-12)


def _rw_prep(u, uc, w_lora, w0, a0, k_k, hsum, tm=256):
    t = u.shape[0]
    c = RWKV_WIDTH
    full = lambda a: pl.BlockSpec(a.shape, lambda i, nd=a.ndim: (0,) * nd)
    lora_cb = (u.shape[1] - LORA_IN) // LORA_IN
    k_cb = (3 * HY_CH) // c + 1
    return pl.pallas_call(
        _rw_prep_body,
        grid=(t // tm,),
        in_specs=[pl.BlockSpec((tm, LORA_IN), lambda i: (i, lora_cb)),
                  pl.BlockSpec((tm, c), lambda i: (i, k_cb)),
                  full(w_lora), full(w0), full(a0), full(k_k), full(hsum)],
        out_specs=[pl.BlockSpec((tm, c), lambda i: (i, 0)),
                   pl.BlockSpec((2, tm, c), lambda i: (0, i, 0)),
                   pl.BlockSpec((2, tm, c), lambda i: (0, i, 0)),
                   pl.BlockSpec((tm, c), lambda i: (i, 0))],
        out_shape=[jax.ShapeDtypeStruct((t, c), f32),
                   jax.ShapeDtypeStruct((2, t, c), f32),
                   jax.ShapeDtypeStruct((2, t, c), f32),
                   jax.ShapeDtypeStruct((t, c), f32)],
        compiler_params=_cp(("arbitrary",)),
        name="rw_prep",
    )(u, uc, w_lora, w0, a0, k_k, hsum)


def _split3(x):
    hi = x.astype(bf16)
    r1 = x - hi.astype(f32)
    mid = r1.astype(bf16)
    lo = (r1 - mid.astype(f32)).astype(bf16)
    return hi, mid, lo


def _rw_dir_operands(d, r_ref, k_ref, v_ref, kk_ref, lw_ref, a_ref, ka):
    cdim = CHUNK
    row = lax.broadcasted_iota(jnp.int32, (cdim, cdim), 0)
    col = lax.broadcasted_iota(jnp.int32, (cdim, cdim), 1)
    lag = row - col if d == 0 else col - row
    incl = lag >= 0
    strict = lag > 0
    lw = lw_ref[...]
    cs = sum(_dot(incl.astype(bf16), part) for part in _split3(lw))
    tot = jnp.sum(lw, axis=0, keepdims=True)
    a = a_ref[...]
    kk = kk_ref[...]
    beta = kk * a
    kdir = k_ref[...] * (1.0 + (a - 1.0) * ka)
    p_inv = jnp.exp(-cs)
    p_rest = jnp.exp(tot - cs)
    return dict(
        incl=incl, strict=strict, p_all=jnp.exp(tot),
        kap=(kk * jnp.exp(cs - lw)).astype(bf16), r=(r_ref[...] * jnp.exp(cs)).astype(bf16),
        beta=(beta * p_inv).astype(bf16), kdir=(kdir * p_inv).astype(bf16),
        beta_e=(beta * p_rest).astype(bf16), kdir_e=(kdir * p_rest).astype(bf16),
        v=v_ref[...].astype(bf16))


def _rw_scan_body(rb_ref, first_ref, seq_ref,
                  r0_ref, k0_ref, v0_ref, kk0_ref, lw0_ref, a0_ref,
                  r1_ref, k1_ref, v1_ref, kk1_ref, lw1_ref, a1_ref, ka_ref, s0_ref,
                  y0_ref, y1_ref, sf_ref, s_scr):
    s = pl.program_id(0)
    cdim = CHUNK

    @pl.when(first_ref[s] == 1)
    def _():
        s_scr[...] = s0_ref[...]

    ka = ka_ref[...]
    ops = (_rw_dir_operands(0, r0_ref, k0_ref, v0_ref, kk0_ref, lw0_ref, a0_ref, ka),
           _rw_dir_operands(1, r1_ref, k1_ref, v1_ref, kk1_ref, lw1_ref, a1_ref, ka))
    chains = [(d, h) for d in range(2) for h in range(N_RWKV_HEADS)]

    def sl(name, c):
        d, h = c
        return ops[d][name][:, h * HEAD_DIM:(h + 1) * HEAD_DIM]

    s_old = [s_scr[d, h] for d, h in chains]
    s_b = [x.astype(bf16) for x in s_old]
    m = [_dot_nt(jnp.concatenate([sl('kap', c), sl('r', c)], axis=0),
                 jnp.concatenate([sl('beta', c), sl('kdir', c)], axis=0)) for c in chains]
    strict = [ops[d]['strict'] for d, _ in chains]
    incl = [ops[d]['incl'] for d, _ in chains]
    n = range(len(chains))
    x = [jnp.where(strict[i], -m[i][:cdim, :cdim], 0.0).astype(bf16) for i in n]
    a_ak = [jnp.where(strict[i], m[i][:cdim, cdim:], 0.0).astype(bf16) for i in n]
    m_rb = [jnp.where(incl[i], m[i][cdim:, :cdim], 0.0).astype(bf16) for i in n]
    m_rk = [jnp.where(incl[i], m[i][cdim:, cdim:], 0.0).astype(bf16) for i in n]
    uu = [_dot_nt(sl('kap', chains[i]), s_b[i]) + _dot(a_ak[i], sl('v', chains[i])) for i in n]
    n_sq = int(math.log2(cdim))
    for it in range(n_sq):
        uu = [uu[i] + _dot(x[i], uu[i].astype(bf16)) for i in n]
        if it + 1 < n_sq:
            x = [_dot(x[i], x[i]).astype(bf16) for i in n]
    u_b = [(-uu[i]).astype(bf16) for i in n]
    y = [_dot_nt(sl('r', chains[i]), s_b[i]) + _dot(m_rb[i], u_b[i]) + _dot(m_rk[i], sl('v', chains[i]))
         for i in n]
    y0_ref[...] = jnp.concatenate(y[:N_RWKV_HEADS], axis=1)
    y1_ref[...] = jnp.concatenate(y[N_RWKV_HEADS:], axis=1)
    for i in n:
        d, h = chains[i]
        s_new = (s_old[i] * ops[d]['p_all'][:, h * HEAD_DIM:(h + 1) * HEAD_DIM]
                 + _dot_tn(u_b[i], sl('beta_e', chains[i])) + _dot_tn(sl('v', chains[i]), sl('kdir_e', chains[i])))
        s_scr[d, h] = s_new
        sf_ref[d, h] = s_new


def _rw_scan(uc, kk, lw, a, k_a, s0, tabs, n_steps):
    c = RWKV_WIDTH
    t = uc.shape[0]
    nseq = s0.shape[0]
    rkv_cb = (3 * HY_CH) // c

    def dir_specs(d):
        return [pl.BlockSpec((CHUNK, c), lambda s, rb, fi, sq: (rb[d, s], rkv_cb)),
                pl.BlockSpec((CHUNK, c), lambda s, rb, fi, sq: (rb[d, s], rkv_cb + 1)),
                pl.BlockSpec((CHUNK, c), lambda s, rb, fi, sq: (rb[d, s], rkv_cb + 2)),
                pl.BlockSpec((CHUNK, c), lambda s, rb, fi, sq: (rb[d, s], 0)),
                pl.BlockSpec((None, CHUNK, c), lambda s, rb, fi, sq: (d, rb[d, s], 0)),
                pl.BlockSpec((None, CHUNK, c), lambda s, rb, fi, sq: (d, rb[d, s], 0))]

    state_spec = pl.BlockSpec((None, 2, N_RWKV_HEADS, HEAD_DIM, HEAD_DIM), lambda s, rb, fi, sq: (sq[s], 0, 0, 0, 0))
    grid_spec = pltpu.PrefetchScalarGridSpec(
        num_scalar_prefetch=3,
        grid=(n_steps,),
        in_specs=dir_specs(0) + dir_specs(1) + [pl.BlockSpec((1, c), lambda s, rb, fi, sq: (0, 0)), state_spec],
        out_specs=[pl.BlockSpec((CHUNK, c), lambda s, rb, fi, sq: (rb[0, s], 0)),
                   pl.BlockSpec((CHUNK, c), lambda s, rb, fi, sq: (rb[1, s], 0)),
                   state_spec],
        scratch_shapes=[pltpu.VMEM((2, N_RWKV_HEADS, HEAD_DIM, HEAD_DIM), f32)],
    )
    return pl.pallas_call(
        _rw_scan_body,
        grid_spec=grid_spec,
        out_shape=[jax.ShapeDtypeStruct((t, c), f32), jax.ShapeDtypeStruct((t, c), f32),
                   jax.ShapeDtypeStruct((nseq, 2, N_RWKV_HEADS, HEAD_DIM, HEAD_DIM), f32)],
        compiler_params=_cp(("arbitrary",)),
        name="rw_scan",
    )(*tabs, uc, uc, uc, kk, lw, a, uc, uc, uc, kk, lw, a, k_a, s0)


def _scan_tables(rows):
    rb = [[], []]
    first, seq = [], []
    base = 0
    sidx = 0
    for b, l in ((rows.b_ctx, rows.l_ctx), (rows.b_lat, rows.l_lat)):
        nc = l // CHUNK
        for i in range(b):
            for c in range(nc):
                rb[0].append(base + c)
                rb[1].append(base + nc - 1 - c)
                first.append(1 if c == 0 else 0)
                seq.append(sidx)
            base += nc
            sidx += 1
    return (jnp.asarray(np.array(rb, np.int32)), jnp.asarray(np.array(first, np.int32)),
            jnp.asarray(np.array(seq, np.int32))), len(first)


def _rw_post_body(y0_ref, y1_ref, r_ref, k_ref, v_ref, g_ref, rk_ref, gw_ref, gb_ref, hm_ref, hs_ref, o_ref):
    y = y0_ref[...] + y1_ref[...]
    mu = _dot2(y, hm_ref[...])
    yc = y - mu
    var = _dot2(yc * yc, hm_ref[...])
    yn = yc * lax.rsqrt(var + RWKV_GN_EPS) * gw_ref[...] + gb_ref[...]
    bonus = _dot2(r_ref[...] * k_ref[...] * rk_ref[...], hs_ref[...]) * v_ref[...]
    o_ref[...] = (yn + bonus) * g_ref[...]


def _rw_post(y0, y1, uc, g, r_k, gn_w, gn_b, hmean, hsum, tm=256):
    t = uc.shape[0]
    c = RWKV_WIDTH
    rkv_cb = (3 * HY_CH) // c
    full = lambda a: pl.BlockSpec(a.shape, lambda i, nd=a.ndim: (0,) * nd)
    return pl.pallas_call(
        _rw_post_body,
        grid=(t // tm,),
        in_specs=[pl.BlockSpec((tm, c), lambda i: (i, 0)),
                  pl.BlockSpec((tm, c), lambda i: (i, 0)),
                  pl.BlockSpec((tm, c), lambda i: (i, rkv_cb)),
                  pl.BlockSpec((tm, c), lambda i: (i, rkv_cb + 1)),
                  pl.BlockSpec((tm, c), lambda i: (i, rkv_cb + 2)),
                  pl.BlockSpec((tm, c), lambda i: (i, 0)),
                  full(r_k), full(gn_w), full(gn_b), full(hmean), full(hsum)],
        out_specs=pl.BlockSpec((tm, c), lambda i: (i, 0)),
        out_shape=jax.ShapeDtypeStruct((t, c), f32),
        compiler_params=_cp(("arbitrary",)),
        name="rw_post",
    )(y0, y1, uc, uc, uc, g, r_k, gn_w, gn_b, hmean, hsum)


def _outproj_body(att_ref, hy_ref, rw_ref, w_ref, x_ref, g_ref, nw_ref, o_ref):
    o = (_dot(att_ref[...].astype(bf16), w_ref[0:ATT_WIDTH, :])
         + _dot(hy_ref[...].astype(bf16), w_ref[ATT_WIDTH:ATT_WIDTH + HY_CH, :])
         + _dot(rw_ref[...].astype(bf16), w_ref[ATT_WIDTH + HY_CH:, :]))
    y = o * lax.rsqrt(jnp.mean(o * o, axis=-1, keepdims=True) + RMS_EPS) * nw_ref[...]
    o_ref[...] = x_ref[...] + g_ref[...] * y


def _out_proj(att, hy, rw, w_out, x, mod, nw, rows, tm=256):
    t, d = x.shape
    midx = rows.mod_index(tm)
    return pl.pallas_call(
        _outproj_body,
        grid=(t // tm,),
        in_specs=[pl.BlockSpec((tm, ATT_WIDTH), lambda i: (i, 0)),
                  pl.BlockSpec((tm, HY_CH), lambda i: (i, 0)),
                  pl.BlockSpec((tm, RWKV_WIDTH), lambda i: (i, 0)),
                  pl.BlockSpec(w_out.shape, lambda i: (0, 0)),
                  pl.BlockSpec((tm, d), lambda i: (i, 0)),
                  pl.BlockSpec((None, 1, d), lambda i: (midx(i), 0, 2)),
                  pl.BlockSpec((None, 1, d), lambda i: (1, 0, 0))],
        out_specs=pl.BlockSpec((tm, d), lambda i: (i, 0)),
        out_shape=jax.ShapeDtypeStruct((t, d), f32),
        compiler_params=_cp(("arbitrary",)),
        name="out_proj",
    )(att, hy, rw, w_out, x, mod, nw)


def _mlp_body(x_ref, sh_ref, sc_ref, g_ref, nw2_ref, nw3_ref, w1_ref, w2_ref, o_ref, h_scr, acc_scr):
    j = pl.program_id(1)

    @pl.when(j == 0)
    def _():
        h_scr[...] = _normmod(x_ref[...], nw2_ref[...], sc_ref[...], sh_ref[...]).astype(bf16)
        acc_scr[...] = jnp.zeros_like(acc_scr)

    a = jnp.maximum(_dot(h_scr[...], w1_ref[...]), 0.0)
    acc_scr[...] += _dot((a * a).astype(bf16), w2_ref[...])

    @pl.when(j == pl.num_programs(1) - 1)
    def _():
        f = acc_scr[...]
        y = f * lax.rsqrt(jnp.mean(f * f, axis=-1, keepdims=True) + RMS_EPS) * nw3_ref[...]
        o_ref[...] = x_ref[...] + g_ref[...] * y


def _mlp(x, mod, nw, w1, w2, rows, tm=512, tf=1024):
    t, d = x.shape
    dff = w1.shape[1]
    midx = rows.mod_index(tm)
    return pl.pallas_call(
        _mlp_body,
        grid=(t // tm, dff // tf),
        in_specs=[pl.BlockSpec((tm, d), lambda i, j: (i, 0)),
                  pl.BlockSpec((None, 1, d), lambda i, j: (midx(i), 0, 3)),
                  pl.BlockSpec((None, 1, d), lambda i, j: (midx(i), 0, 4)),
                  pl.BlockSpec((None, 1, d), lambda i, j: (midx(i), 0, 5)),
                  pl.BlockSpec((None, 1, d), lambda i, j: (2, 0, 0)),
                  pl.BlockSpec((None, 1, d), lambda i, j: (3, 0, 0)),
                  pl.BlockSpec((d, tf), lambda i, j: (0, j)),
                  pl.BlockSpec((tf, d), lambda i, j: (j, 0))],
        out_specs=pl.BlockSpec((tm, d), lambda i, j: (i, 0)),
        out_shape=jax.ShapeDtypeStruct((t, d), f32),
        scratch_shapes=[pltpu.VMEM((tm, d), bf16), pltpu.VMEM((tm, d), f32)],
        compiler_params=_cp(("arbitrary", "arbitrary")),
        name="mlp",
    )(x, mod, mod, mod, nw, nw, w1, w2)


def _lora_weight(w_up, a_up, g_up):
    c = RWKV_WIDTH
    w = jnp.zeros((LORA_IN, 5 * c), f32)
    r = 0
    for j, blk in enumerate((w_up[0], w_up[1], a_up[0], a_up[1], g_up)):
        w = w.at[r:r + blk.shape[0], j * c:(j + 1) * c].set(blk)
        r += blk.shape[0]
    return w.astype(bf16)


def kernel(x_prompt, x_sample, cache_k, cache_v, state_rwkv, c, c_ctx, ada_w, ada_b, norm_w, w_in, w_out,
           attn_sink, hy_short_w, hy_f1, hy_b1, hy_f2, hy_b2, hy_f3, hy_decay, hy_skip, rw_short_w, rw_w0,
           rw_w_up, rw_a0, rw_a_up, rw_g_up, rw_k_k, rw_k_a, rw_r_k, rw_gn_w, rw_gn_b, mlp_w1, mlp_w2):
    b_ctx, l_ctx, d = x_prompt.shape
    b_lat, l_lat, _ = x_sample.shape
    depth = ada_w.shape[0]
    grid_w = 64
    rows = _Rows(b_ctx, l_ctx, b_lat, l_lat)
    t = rows.t

    x = jnp.concatenate([x_prompt.reshape(rows.t_ctx, d), x_sample.reshape(rows.t_lat, d)], axis=0)
    cond16 = jnp.zeros((16, d), f32).at[0].set(c_ctx).at[1:1 + b_lat].set(c)
    mod_all = _modulation(cond16, ada_w, ada_b[:, None, :])

    cos_t, sin_t = _rope_tables(l_lat, grid_w)
    hsum = _head_sum_matrix()
    hmean = _head_sum_matrix(1.0 / HEAD_DIM)
    scan_tabs, n_steps = _scan_tables(rows)
    dft = {}
    for l in (l_ctx, l_lat):
        cm, sm = _dft_tables(l)
        feat, t01 = _hy_feat(l)
        dft[l] = dict(cm=cm, sm=sm, cmb=cm.astype(bf16), smb=sm.astype(bf16),
                      cmtb=cm.T.astype(bf16), smtb=sm.T.astype(bf16), feat=feat, t01=t01)

    new_k, new_v, new_s = [], [], []
    for l in range(depth):
        mod = mod_all[l].reshape(16, 1, N_MOD * d)
        nw = norm_w[l].reshape(4, 1, d)
        u = _in_proj(x, mod, nw, w_in[l].astype(bf16), rows)

        kv0 = ATT_WIDTH
        new_k.append(u[:rows.t_ctx, kv0:kv0 + ATT_KV].reshape(b_ctx, l_ctx, N_KV_HEADS, HEAD_DIM))
        new_v.append(u[:rows.t_ctx, kv0 + ATT_KV:kv0 + 2 * ATT_KV].reshape(b_ctx, l_ctx, N_KV_HEADS, HEAD_DIM))

        wconv = jnp.concatenate([hy_short_w[l], rw_short_w[l]], axis=1)
        uc = _short_conv(u, wconv, rows, ATT_WIDTH + 2 * ATT_KV)

        att_c = _attn_ctx(u, attn_sink[l], rows)
        att_l = _attn_lat(u, cache_k[:, l].reshape(b_lat, -1, ATT_KV), cache_v[:, l].reshape(b_lat, -1, ATT_KV),
                          attn_sink[l], cos_t, sin_t, rows)
        att = jnp.concatenate([att_c, att_l], axis=0)

        f1p = jnp.pad(hy_f1[l], ((0, 128 - HY_EMB), (0, 0)))
        hys = []
        for (bb, ll, rb0) in ((b_ctx, l_ctx, 0), (b_lat, l_lat, rows.t_ctx // l_lat)):
            tb = dft[ll]
            hr, hi = _hy_filter(ll, tb['feat'], tb['t01'], f1p, hy_b1[l][None], hy_f2[l], hy_b2[l][None],
                                hy_f3[l], hy_decay[l][None], tb['cm'], tb['sm'])
            hys.append(_hy_conv(uc, hr, hi, hy_skip[l], tb['cmb'], tb['smb'], tb['cmtb'], tb['smtb'],
                                bb, ll, rb0))
        hy = jnp.concatenate(hys, axis=0)

        w_lora = _lora_weight(rw_w_up[l], rw_a_up[l], rw_g_up[l])
        kk, lw, a, g = _rw_prep(u, uc, w_lora, rw_w0[l], rw_a0[l], rw_k_k[l][None], hsum)
        h0 = jnp.concatenate([jnp.zeros((b_ctx, 2, N_RWKV_HEADS, HEAD_DIM, HEAD_DIM), f32),
                              state_rwkv[:, l]], axis=0)
        y0, y1, hfin = _rw_scan(uc, kk, lw, a, rw_k_a[l][None], h0, scan_tabs, n_steps)
        new_s.append(hfin[:b_ctx])
        rw = _rw_post(y0, y1, uc, g, rw_r_k[l].reshape(1, RWKV_WIDTH), rw_gn_w[l][None], rw_gn_b[l][None],
                      hmean, hsum)

        x = _out_proj(att, hy, rw, w_out[l].astype(bf16), x, mod, nw, rows)
        x = _mlp(x, mod, nw, mlp_w1[l].astype(bf16), mlp_w2[l].astype(bf16), rows)

    y_p = x[:rows.t_ctx].reshape(b_ctx, l_ctx, d)
    y_s = x[rows.t_ctx:].reshape(b_lat, l_lat, d)
    return (y_p, y_s, jnp.stack(new_k, axis=1), jnp.stack(new_v, axis=1), jnp.stack(new_s, axis=1))
```

```python
import functools
import math

import numpy as np
import jax
import jax.numpy as jnp
from jax import lax
from jax.experimental import pallas as pl
from jax.experimental.pallas import tpu as pltpu

f32 = jnp.float32
bf16 = jnp.bfloat16

HEAD_DIM = 64
N_ATT_HEADS = 12
N_KV_HEADS = 4
ATT_GROUP = 3
ATT_WIDTH = 768
ATT_KV = 256
WINDOW = 128
ROPE_THETA = 10000.0
NEG_INF = -1e30
HY_CH = 512
HY_BANDS = 16
HY_EMB = 33
HY_FFN = 64
HY_MOD_SHIFT = 0.05
RWKV_WIDTH = 768
N_RWKV_HEADS = 12
DECAY_LORA = 96
AAA_LORA = 96
GATE_LORA = 256
RWKV_GN_EPS = 64e-5
N_MOD = 6
RMS_EPS = 1e-6
CHUNK = 64
CONV_ROWS = 256
LORA_IN = 2 * DECAY_LORA + 2 * AAA_LORA + GATE_LORA
COL_Q = 0
COL_RKV = COL_Q + ATT_WIDTH
COL_HY = COL_RKV + 3 * RWKV_WIDTH
COL_KV = COL_HY + 3 * HY_CH
COL_LORA = COL_KV + 2 * ATT_KV
IN_COLS = COL_LORA + LORA_IN
HI = lax.Precision.HIGHEST
VMEM_LIMIT = 56 * 1024 * 1024


def _cp(sem, vmem=VMEM_LIMIT):
    return pltpu.CompilerParams(dimension_semantics=sem, vmem_limit_bytes=vmem)


def _dot(a, b):
    return jnp.dot(a, b, preferred_element_type=f32)


def _dot_nt(a, b):
    return lax.dot_general(a, b, (((1,), (1,)), ((), ())), preferred_element_type=f32)


def _dot_tn(a, b):
    return lax.dot_general(a, b, (((0,), (0,)), ((), ())), preferred_element_type=f32)


def _dot_hi(a, b):
    return jnp.dot(a, b, preferred_element_type=f32, precision=HI)


def _full(a):
    return pl.BlockSpec(a.shape, lambda *_: (0,) * a.ndim)


def _mod_body(c_ref, w_ref, b_ref, o_ref):
    c = c_ref[...]
    s = (c * jax.nn.sigmoid(c)).astype(bf16)
    o_ref[...] = _dot(s, w_ref[...].astype(bf16)) + b_ref[...]


def _modulation(cond16, ada_w, ada_b):
    depth, d, n = ada_w.shape
    tn = 1024
    return pl.pallas_call(
        _mod_body,
        grid=(depth, n // tn),
        in_specs=[pl.BlockSpec((16, d), lambda l, j: (0, 0)),
                  pl.BlockSpec((None, d, tn), lambda l, j: (l, 0, j)),
                  pl.BlockSpec((None, 1, tn), lambda l, j: (l, 0, j))],
        out_specs=pl.BlockSpec((None, 16, tn), lambda l, j: (l, 0, j)),
        out_shape=jax.ShapeDtypeStruct((depth, 16, n), f32),
        compiler_params=_cp(("arbitrary", "arbitrary")),
        name="modulation",
    )(cond16, ada_w, ada_b)


def _normmod(x, nw, sc, sh):
    y = x * lax.rsqrt(jnp.mean(x * x, axis=-1, keepdims=True) + RMS_EPS) * nw
    return y * (1.0 + sc) + sh


class _Rows:
    def __init__(self, b_ctx, l_ctx, b_lat, l_lat):
        self.b_ctx, self.l_ctx, self.b_lat, self.l_lat = b_ctx, l_ctx, b_lat, l_lat
        self.t_ctx = b_ctx * l_ctx
        self.t_lat = b_lat * l_lat
        self.t = self.t_ctx + self.t_lat

    def mod_index(self, tm):
        nctx = self.t_ctx // tm
        per = self.l_lat // tm

        def f(i):
            return jnp.where(i < nctx, 0, 1 + (i - nctx) // per)
        return f


def _pair_specs(arrs, rows, tm, single_buffer=False):
    w = arrs[0].shape[1]
    if len(arrs) == 1:
        return [pl.BlockSpec((tm, w), lambda i, *_: (i, 0))]
    nctx = rows.t_ctx // tm
    mode = dict(pipeline_mode=pl.Buffered(1)) if single_buffer else {}
    return [pl.BlockSpec((tm, w), lambda i, *_: (jnp.minimum(i, nctx - 1), 0), **mode),
            pl.BlockSpec((tm, w), lambda i, *_: (jnp.maximum(i - nctx, 0), 0), **mode)]


def _pair_load(refs, nctx):
    if len(refs) == 1:
        return refs[0][...]
    return jnp.where(pl.program_id(0) < nctx, refs[0][...], refs[1][...])


def _inproj_body(*refs, nx, nctx, lmask_ctx, lmask_lat):
    x_refs = refs[:nx]
    sh_ref, sc_ref, nw_ref, w_ref, wc_ref, o_ref, h_scr, u_scr = refs[nx:]
    i = pl.program_id(0)

    @pl.when(pl.program_id(1) == 0)
    def _():
        h_scr[...] = _normmod(_pair_load(x_refs, nctx), nw_ref[...], sc_ref[...], sh_ref[...]).astype(bf16)

    u_scr[...] = _dot(h_scr[...], w_ref[...])
    tm = u_scr.shape[0]
    ch = CONV_ROWS
    lmask = jnp.where(i < nctx, lmask_ctx, lmask_lat)
    row = lax.broadcasted_iota(jnp.int32, (ch, 1), 0)
    wc = wc_ref[...]
    for r0 in range(0, tm, ch):
        x = u_scr[r0:r0 + ch, :]
        at_start = (r0 & lmask) == 0
        at_end = ((r0 + ch - 1) & lmask) == lmask
        prev = jnp.where(at_start, 0.0, u_scr[max(r0 - 1, 0):max(r0 - 1, 0) + 1, :])
        nxt = jnp.where(at_end, 0.0, u_scr[min(r0 + ch, tm - 1):min(r0 + ch, tm - 1) + 1, :])
        xm = jnp.where(row == 0, prev, pltpu.roll(x, 1, 0))
        xp = jnp.where(row == ch - 1, nxt, pltpu.roll(x, ch - 1, 0))
        o_ref[r0:r0 + ch, :] = xm * wc[0:1, :] + x * wc[1:2, :] + xp * wc[2:3, :]


def _in_proj(xs, mod, nw, w_in, wconv, rows, tm=1024, tn=1152):
    t = rows.t
    d, n = w_in.shape
    for l in (rows.l_ctx, rows.l_lat):
        assert l & (l - 1) == 0 and tm % l == 0 and l % CONV_ROWS == 0
    midx = rows.mod_index(tm)
    body = functools.partial(_inproj_body, nx=len(xs), nctx=rows.t_ctx // tm,
                             lmask_ctx=rows.l_ctx - 1, lmask_lat=rows.l_lat - 1)
    return pl.pallas_call(
        body,
        grid=(t // tm, n // tn),
        in_specs=_pair_specs(xs, rows, tm, single_buffer=True) + [
            pl.BlockSpec((None, 1, d), lambda i, j: (midx(i), 0, 0)),
            pl.BlockSpec((None, 1, d), lambda i, j: (midx(i), 0, 1)),
            pl.BlockSpec((None, 1, d), lambda i, j: (0, 0, 0)),
            pl.BlockSpec((d, tn), lambda i, j: (0, j)),
            pl.BlockSpec((3, tn), lambda i, j: (0, j))],
        out_specs=pl.BlockSpec((tm, tn), lambda i, j: (i, j)),
        out_shape=jax.ShapeDtypeStruct((t, n), f32),
        scratch_shapes=[pltpu.VMEM((tm, d), bf16), pltpu.VMEM((tm, tn), f32)],
        compiler_params=_cp(("arbitrary", "arbitrary")),
        name="in_proj",
    )(*xs, mod, mod, nw, w_in, wconv)


def _softmax_pv(s_list, v_list, sink_col):
    m = sink_col
    for s in s_list:
        m = jnp.maximum(m, jnp.max(s, axis=-1, keepdims=True))
    den = jnp.exp(sink_col - m)
    acc = None
    for s, v in zip(s_list, v_list):
        p = jnp.exp(s - m)
        den = den + jnp.sum(p, axis=-1, keepdims=True)
        pv = _dot(p.astype(bf16), v)
        acc = pv if acc is None else acc + pv
    return acc / den


def _attn_ctx_body(sink_ref, q_ref, k_ref, v_ref, o_ref):
    l = q_ref.shape[0]
    scale = HEAD_DIM ** -0.5
    for n in range(N_KV_HEADS):
        kn = k_ref[:, n * HEAD_DIM:(n + 1) * HEAD_DIM].astype(bf16)
        vn = v_ref[:, n * HEAD_DIM:(n + 1) * HEAD_DIM].astype(bf16)
        for g in range(ATT_GROUP):
            h = n * ATT_GROUP + g
            q = (q_ref[:, h * HEAD_DIM:(h + 1) * HEAD_DIM] * scale).astype(bf16)
            s = _dot_nt(q, kn)
            sink_col = jnp.full((l, 1), sink_ref[h], f32)
            o_ref[:, h * HEAD_DIM:(h + 1) * HEAD_DIM] = _softmax_pv([s], [vn], sink_col)


def _attn_ctx(u, sink, rows):
    b, l = rows.b_ctx, rows.l_ctx
    return pl.pallas_call(
        _attn_ctx_body,
        grid=(b,),
        in_specs=[pl.BlockSpec(memory_space=pltpu.SMEM),
                  pl.BlockSpec((l, ATT_WIDTH), lambda i: (i, COL_Q // ATT_WIDTH)),
                  pl.BlockSpec((l, ATT_KV), lambda i: (i, COL_KV // ATT_KV)),
                  pl.BlockSpec((l, ATT_KV), lambda i: (i, COL_KV // ATT_KV + 1))],
        out_specs=pl.BlockSpec((l, ATT_WIDTH), lambda i: (i, 0)),
        out_shape=jax.ShapeDtypeStruct((b * l, ATT_WIDTH), f32),
        compiler_params=_cp(("arbitrary",)),
        name="attn_ctx",
    )(sink, u, u, u)


def _rope(x, cos, sin_signed):
    w = x.shape[1]
    lane = lax.broadcasted_iota(jnp.int32, (1, w), 1)
    swapped = jnp.where((lane % 32) < 16, pltpu.roll(x, w - 16, 1), pltpu.roll(x, 16, 1))
    return x * cos + swapped * sin_signed


def _attn_lat_body(sink_ref, q_ref, k_ref, v_ref, kc_ref, vc_ref, cos_ref, sin_ref, o_ref,
                   q_scr, k_scr, v_scr):
    l = q_ref.shape[0]
    blk = WINDOW
    nb = l // blk
    scale = HEAD_DIM ** -0.5
    cos = cos_ref[...]
    sin = sin_ref[...]
    q_scr[...] = (_rope(q_ref[...], cos, sin) * scale).astype(bf16)
    zpad = jnp.zeros((blk, ATT_KV), bf16)
    k_scr[0:blk, :] = zpad
    k_scr[blk + l:, :] = zpad
    v_scr[0:blk, :] = zpad
    v_scr[blk + l:, :] = zpad
    k_scr[blk:blk + l, :] = _rope(k_ref[...], cos[:, :ATT_KV], sin[:, :ATT_KV]).astype(bf16)
    v_scr[blk:blk + l, :] = v_ref[...].astype(bf16)
    r = lax.broadcasted_iota(jnp.int32, (ATT_GROUP * blk, 3 * blk), 0) % blk
    c = lax.broadcasted_iota(jnp.int32, (ATT_GROUP * blk, 3 * blk), 1)
    band = (c - r >= 0) & (c - r <= 2 * WINDOW)
    for i in range(nb):
        kpos = (i - 1) * blk + c
        mask = band & (kpos >= 0) & (kpos < l)
        for n in range(N_KV_HEADS):
            cs = slice(n * HEAD_DIM, (n + 1) * HEAD_DIM)
            kl = k_scr[i * blk:(i + 3) * blk, cs]
            vl = v_scr[i * blk:(i + 3) * blk, cs]
            kc = kc_ref[:, cs].astype(bf16)
            vc = vc_ref[:, cs].astype(bf16)
            qs = jnp.concatenate(
                [q_scr[i * blk:(i + 1) * blk, (n * ATT_GROUP + g) * HEAD_DIM:(n * ATT_GROUP + g + 1) * HEAD_DIM]
                 for g in range(ATT_GROUP)], axis=0)
            s_loc = jnp.where(mask, _dot_nt(qs, kl), NEG_INF)
            s_ctx = _dot_nt(qs, kc)
            sink_col = jnp.concatenate(
                [jnp.full((blk, 1), sink_ref[n * ATT_GROUP + g], f32) for g in range(ATT_GROUP)], axis=0)
            o = _softmax_pv([s_loc, s_ctx], [vl, vc], sink_col)
            for g in range(ATT_GROUP):
                h = n * ATT_GROUP + g
                o_ref[i * blk:(i + 1) * blk, h * HEAD_DIM:(h + 1) * HEAD_DIM] = o[g * blk:(g + 1) * blk, :]


def _attn_lat(u, kc, vc, sink, cos, sin, rows):
    b, l = rows.b_lat, rows.l_lat
    assert rows.t_ctx % l == 0
    rb0 = rows.t_ctx // l
    lc = kc.shape[1]
    return pl.pallas_call(
        _attn_lat_body,
        grid=(b,),
        in_specs=[pl.BlockSpec(memory_space=pltpu.SMEM),
                  pl.BlockSpec((l, ATT_WIDTH), lambda i: (rb0 + i, COL_Q // ATT_WIDTH)),
                  pl.BlockSpec((l, ATT_KV), lambda i: (rb0 + i, COL_KV // ATT_KV)),
                  pl.BlockSpec((l, ATT_KV), lambda i: (rb0 + i, COL_KV // ATT_KV + 1)),
                  pl.BlockSpec((None, lc, ATT_KV), lambda i: (i, 0, 0)),
                  pl.BlockSpec((None, lc, ATT_KV), lambda i: (i, 0, 0)),
                  pl.BlockSpec((l, ATT_WIDTH), lambda i: (0, 0)),
                  pl.BlockSpec((l, ATT_WIDTH), lambda i: (0, 0))],
        out_specs=pl.BlockSpec((l, ATT_WIDTH), lambda i: (i, 0)),
        out_shape=jax.ShapeDtypeStruct((b * l, ATT_WIDTH), f32),
        scratch_shapes=[pltpu.VMEM((l, ATT_WIDTH), bf16),
                        pltpu.VMEM((l + 2 * WINDOW, ATT_KV), bf16),
                        pltpu.VMEM((l + 2 * WINDOW, ATT_KV), bf16)],
        compiler_params=_cp(("arbitrary",)),
        name="attn_lat",
    )(sink, u, u, u, kc, vc, cos, sin)


def _rope_tables(l, grid_w):
    half = HEAD_DIM // 2
    pos = jnp.arange(l)
    rowp = (pos // grid_w).astype(f32)
    colp = (pos % grid_w).astype(f32)
    freqs = ROPE_THETA ** (-jnp.arange(0, half, 2, dtype=f32) / half)
    ar = rowp[:, None] * freqs[None, :]
    ac = colp[:, None] * freqs[None, :]
    cos = jnp.concatenate([jnp.cos(ar), jnp.cos(ar), jnp.cos(ac), jnp.cos(ac)], axis=-1)
    sin = jnp.concatenate([-jnp.sin(ar), jnp.sin(ar), -jnp.sin(ac), jnp.sin(ac)], axis=-1)
    return jnp.tile(cos, (1, N_ATT_HEADS)), jnp.tile(sin, (1, N_ATT_HEADS))


def _dft_tables(l):
    k = jnp.arange(l, dtype=jnp.int32)[:, None]
    s = jnp.arange(l, dtype=jnp.int32)[None, :]
    m = ((2 * k + 1) * s) % (4 * l)
    theta = m.astype(f32) * (math.pi / (2 * l))
    return jnp.cos(theta), jnp.sin(theta)


def _hy_feat(l):
    t = jnp.arange(l, dtype=f32)
    t01 = (t / max(l - 1, 1))[:, None]
    bands = jnp.linspace(1e-4, HY_BANDS - 1, HY_BANDS, dtype=f32)
    ang = (2.0 * math.pi / l) * t[:, None] * bands[None, :]
    feat = jnp.concatenate([t01, jnp.cos(ang), -jnp.sin(ang)], axis=-1)
    return jnp.pad(feat, ((0, 0), (0, 128 - HY_EMB))), t01


def _hy_filter_body(feat_ref, t01_ref, f1_ref, b1_ref, f2_ref, b2_ref, f3_ref, dec_ref, cm_ref, sm_ref,
                    hr_ref, hi_ref):
    h = jnp.sin(_dot_hi(feat_ref[...], f1_ref[...]) + b1_ref[...])
    h = jnp.sin(_dot_hi(h, f2_ref[...]) + b2_ref[...])
    h = _dot_hi(h, f3_ref[...])
    h = h * (jnp.exp(-t01_ref[...] * jnp.abs(dec_ref[...])) + HY_MOD_SHIFT)
    l = h.shape[0]
    fwd = h[:, :HY_CH]
    row = lax.broadcasted_iota(jnp.int32, (l, 1), 0)
    bwd = jnp.where(row == 0, 0.0, h[:, HY_CH:])
    hr_ref[...] = _dot_hi(cm_ref[...], fwd + bwd)
    hi_ref[...] = _dot_hi(sm_ref[...], bwd - fwd)


def _hy_filter(l, feat, t01, f1p, b1, f2, b2, f3, dec, cm, sm):
    args = (feat, t01, f1p, b1, f2, b2, f3, dec, cm, sm)
    return pl.pallas_call(
        _hy_filter_body,
        grid=(1,),
        in_specs=[_full(a) for a in args],
        out_specs=[pl.BlockSpec((l, HY_CH), lambda i: (0, 0))] * 2,
        out_shape=[jax.ShapeDtypeStruct((l, HY_CH), f32)] * 2,
        compiler_params=_cp(("arbitrary",)),
        name="hy_filter",
    )(*args)


def _hy_conv_body(x0_ref, x1_ref, v_ref, hr_ref, hi_ref, skip_ref, cm_ref, sm_ref, cmt_ref, smt_ref, o_ref):
    l = v_ref.shape[0]
    z = v_ref[...] * x1_ref[...]
    zb = z.astype(bf16)
    zr = _dot(cm_ref[...], zb)
    zs = _dot(sm_ref[...], zb)
    hr = hr_ref[...]
    hi = hi_ref[...]
    yr = (zr * hr + zs * hi).astype(bf16)
    yi = (zr * hi - zs * hr).astype(bf16)
    y = (_dot(cmt_ref[...], yr) - _dot(smt_ref[...], yi)) * (1.0 / l)
    o_ref[...] = (y + skip_ref[...] * z) * x0_ref[...]


def _hy_conv(u, hr, hi, skip, cm, sm, cmt, smt, b, l, rb0):
    cb = COL_HY // HY_CH
    return pl.pallas_call(
        _hy_conv_body,
        grid=(b,),
        in_specs=[pl.BlockSpec((l, HY_CH), lambda i: (rb0 + i, cb)),
                  pl.BlockSpec((l, HY_CH), lambda i: (rb0 + i, cb + 1)),
                  pl.BlockSpec((l, HY_CH), lambda i: (rb0 + i, cb + 2)),
                  _full(hr), _full(hi), _full(skip), _full(cm), _full(sm), _full(cmt), _full(smt)],
        out_specs=pl.BlockSpec((l, HY_CH), lambda i: (i, 0)),
        out_shape=jax.ShapeDtypeStruct((b * l, HY_CH), f32),
        compiler_params=_cp(("arbitrary",)),
        name="hy_conv",
    )(u, u, u, hr, hi, skip, cm, sm, cmt, smt)


def _head_sum_matrix(scale=1.0):
    i = np.arange(RWKV_WIDTH) // HEAD_DIM
    return jnp.asarray((i[:, None] == i[None, :]).astype(np.float32) * scale).astype(bf16)


def _dot2(x, m):
    hi = x.astype(bf16)
    lo = (x - hi.astype(f32)).astype(bf16)
    return _dot(hi, m) + _dot(lo, m)


def _rw_prep_body(lo_ref, k_ref, wl_ref, w0_ref, a0_ref, kk_ref_, hs_ref, kk_o, lw_o, a_o, g_o):
    x = lo_ref[...]
    lane = lax.broadcasted_iota(jnp.int32, x.shape, 1)
    act = jnp.where(lane < 2 * DECAY_LORA, jnp.tanh(x),
                    jnp.where(lane < 2 * DECAY_LORA + 2 * AAA_LORA, x, jax.nn.sigmoid(x)))
    lo = _dot(act.astype(bf16), wl_ref[...])
    c = RWKV_WIDTH
    for d in range(2):
        zneg = -(w0_ref[d:d + 1, :] + lo[:, d * c:(d + 1) * c])
        softplus = jnp.maximum(zneg, 0.0) + jnp.log(1.0 + jnp.exp(-jnp.abs(zneg)))
        w_log = -softplus - 0.5
        lw_o[d] = -jnp.exp(w_log)
        a_o[d] = jax.nn.sigmoid(a0_ref[d:d + 1, :] + lo[:, (2 + d) * c:(3 + d) * c])
    g_o[...] = lo[:, 4 * c:5 * c]
    kk = k_ref[...] * kk_ref_[...]
    ss = _dot2(kk * kk, hs_ref[...])
    kk_o[...] = kk / jnp.maximum(jnp.sqrt(ss), 1e-12)


def _rw_prep(u, w_lora, w0, a0, k_k, hsum, tm=256):
    t = u.shape[0]
    c = RWKV_WIDTH
    return pl.pallas_call(
        _rw_prep_body,
        grid=(t // tm,),
        in_specs=[pl.BlockSpec((tm, LORA_IN), lambda i: (i, COL_LORA // LORA_IN)),
                  pl.BlockSpec((tm, c), lambda i: (i, COL_RKV // c + 1)),
                  _full(w_lora), _full(w0), _full(a0), _full(k_k), _full(hsum)],
        out_specs=[pl.BlockSpec((tm, c), lambda i: (i, 0)),
                   pl.BlockSpec((2, tm, c), lambda i: (0, i, 0)),
                   pl.BlockSpec((2, tm, c), lambda i: (0, i, 0)),
                   pl.BlockSpec((tm, c), lambda i: (i, 0))],
        out_shape=[jax.ShapeDtypeStruct((t, c), f32),
                   jax.ShapeDtypeStruct((2, t, c), f32),
                   jax.ShapeDtypeStruct((2, t, c), f32),
                   jax.ShapeDtypeStruct((t, c), f32)],
        compiler_params=_cp(("arbitrary",)),
        name="rw_prep",
    )(u, u, w_lora, w0, a0, k_k, hsum)


def _split3(x):
    hi = x.astype(bf16)
    r1 = x - hi.astype(f32)
    mid = r1.astype(bf16)
    lo = (r1 - mid.astype(f32)).astype(bf16)
    return hi, mid, lo


def _rw_dir_operands(d, r_ref, k_ref, v_ref, kk_ref, lw_ref, a_ref, ka):
    cdim = CHUNK
    row = lax.broadcasted_iota(jnp.int32, (2 * cdim, 2 * cdim), 0)
    col = lax.broadcasted_iota(jnp.int32, (2 * cdim, 2 * cdim), 1) % cdim
    lag = (row % cdim) - col if d == 0 else col - (row % cdim)
    mask = lag >= jnp.where(row < cdim, 1, 0)
    row1 = lax.broadcasted_iota(jnp.int32, (cdim, cdim), 0)
    col1 = lax.broadcasted_iota(jnp.int32, (cdim, cdim), 1)
    incl = (col1 <= row1) if d == 0 else (col1 >= row1)
    lw = lw_ref[...]
    cs = sum(_dot(incl.astype(bf16), part) for part in _split3(lw))
    tot = jnp.sum(lw, axis=0, keepdims=True)
    a = a_ref[...]
    kk = kk_ref[...]
    beta = kk * a
    kdir = k_ref[...] * (1.0 + (a - 1.0) * ka)
    p_inv = jnp.exp(-cs)
    p_rest = jnp.exp(tot - cs)
    return dict(
        mask=mask, p_all=jnp.exp(tot),
        kap=(kk * jnp.exp(cs - lw)).astype(bf16), r=(r_ref[...] * jnp.exp(cs)).astype(bf16),
        beta=(beta * p_inv).astype(bf16), kdir=(kdir * p_inv).astype(bf16),
        beta_e=(beta * p_rest).astype(bf16), kdir_e=(kdir * p_rest).astype(bf16),
        v=v_ref[...].astype(bf16))


def _rw_scan_body(rb_ref, first_ref, seq_ref,
                  r0_ref, k0_ref, v0_ref, kk0_ref, lw0_ref, a0_ref,
                  r1_ref, k1_ref, v1_ref, kk1_ref, lw1_ref, a1_ref, ka_ref, s0_ref,
                  y0_ref, y1_ref, sf_ref, s_scr):
    s = pl.program_id(0)
    cdim = CHUNK

    @pl.when(first_ref[s] == 1)
    def _():
        s_scr[...] = s0_ref[...]

    ka = ka_ref[...]
    ops = (_rw_dir_operands(0, r0_ref, k0_ref, v0_ref, kk0_ref, lw0_ref, a0_ref, ka),
           _rw_dir_operands(1, r1_ref, k1_ref, v1_ref, kk1_ref, lw1_ref, a1_ref, ka))
    chains = [(d, h) for d in range(2) for h in range(N_RWKV_HEADS)]
    n = range(len(chains))

    def sl(name, i):
        d, h = chains[i]
        return ops[d][name][:, h * HEAD_DIM:(h + 1) * HEAD_DIM]

    s_old = [s_scr[d, h] for d, h in chains]
    s_b = [x.astype(bf16) for x in s_old]
    lhs = [jnp.concatenate([sl('kap', i), sl('r', i)], axis=0) for i in n]
    rhs = [jnp.concatenate([sl('beta', i), sl('kdir', i)], axis=0) for i in n]
    mm = [jnp.where(ops[chains[i][0]]['mask'], _dot_nt(lhs[i], rhs[i]), 0.0).astype(bf16) for i in n]
    sk = [_dot_nt(lhs[i], s_b[i]) for i in n]
    av = [_dot(mm[i][:, cdim:], sl('v', i)) for i in n]
    x = [-mm[i][:cdim, :cdim] for i in n]
    m_rb = [mm[i][cdim:, :cdim] for i in n]
    uu = [sk[i][:cdim] + av[i][:cdim] for i in n]
    n_sq = int(math.log2(cdim))
    for it in range(n_sq):
        uu = [uu[i] + _dot(x[i], uu[i].astype(bf16)) for i in n]
        if it + 1 < n_sq:
            x = [_dot(x[i], x[i]).astype(bf16) for i in n]
    u_b = [(-uu[i]).astype(bf16) for i in n]
    y = [sk[i][cdim:] + _dot(m_rb[i], u_b[i]) + av[i][cdim:] for i in n]
    y0_ref[...] = jnp.concatenate(y[:N_RWKV_HEADS], axis=1)
    y1_ref[...] = jnp.concatenate(y[N_RWKV_HEADS:], axis=1)
    for i in n:
        d, h = chains[i]
        s_new = (s_old[i] * ops[d]['p_all'][:, h * HEAD_DIM:(h + 1) * HEAD_DIM]
                 + _dot_tn(jnp.concatenate([u_b[i], sl('v', i)], axis=0),
                           jnp.concatenate([sl('beta_e', i), sl('kdir_e', i)], axis=0)))
        s_scr[d, h] = s_new
        sf_ref[d, h] = s_new


def _rw_scan(u, kk, lw, a, k_a, s0, tabs, n_steps):
    c = RWKV_WIDTH
    t = u.shape[0]
    nseq = s0.shape[0]
    cb = COL_RKV // c

    def dir_specs(d):
        return [pl.BlockSpec((CHUNK, c), lambda s, rb, fi, sq: (rb[d, s], cb)),
                pl.BlockSpec((CHUNK, c), lambda s, rb, fi, sq: (rb[d, s], cb + 1)),
                pl.BlockSpec((CHUNK, c), lambda s, rb, fi, sq: (rb[d, s], cb + 2)),
                pl.BlockSpec((CHUNK, c), lambda s, rb, fi, sq: (rb[d, s], 0)),
                pl.BlockSpec((None, CHUNK, c), lambda s, rb, fi, sq: (d, rb[d, s], 0)),
                pl.BlockSpec((None, CHUNK, c), lambda s, rb, fi, sq: (d, rb[d, s], 0))]

    state_spec = pl.BlockSpec((None, 2, N_RWKV_HEADS, HEAD_DIM, HEAD_DIM), lambda s, rb, fi, sq: (sq[s], 0, 0, 0, 0))
    grid_spec = pltpu.PrefetchScalarGridSpec(
        num_scalar_prefetch=3,
        grid=(n_steps,),
        in_specs=dir_specs(0) + dir_specs(1) + [pl.BlockSpec((1, c), lambda s, rb, fi, sq: (0, 0)), state_spec],
        out_specs=[pl.BlockSpec((CHUNK, c), lambda s, rb, fi, sq: (rb[0, s], 0)),
                   pl.BlockSpec((CHUNK, c), lambda s, rb, fi, sq: (rb[1, s], 0)),
                   state_spec],
        scratch_shapes=[pltpu.VMEM((2, N_RWKV_HEADS, HEAD_DIM, HEAD_DIM), f32)],
    )
    return pl.pallas_call(
        _rw_scan_body,
        grid_spec=grid_spec,
        out_shape=[jax.ShapeDtypeStruct((t, c), f32), jax.ShapeDtypeStruct((t, c), f32),
                   jax.ShapeDtypeStruct((nseq, 2, N_RWKV_HEADS, HEAD_DIM, HEAD_DIM), f32)],
        compiler_params=_cp(("arbitrary",)),
        name="rw_scan",
    )(*tabs, u, u, u, kk, lw, a, u, u, u, kk, lw, a, k_a, s0)


def _scan_tables(rows):
    rb = [[], []]
    first, seq = [], []
    base = 0
    sidx = 0
    for b, l in ((rows.b_ctx, rows.l_ctx), (rows.b_lat, rows.l_lat)):
        nc = l // CHUNK
        for i in range(b):
            for c in range(nc):
                rb[0].append(base + c)
                rb[1].append(base + nc - 1 - c)
                first.append(1 if c == 0 else 0)
                seq.append(sidx)
            base += nc
            sidx += 1
    return (jnp.asarray(np.array(rb, np.int32)), jnp.asarray(np.array(first, np.int32)),
            jnp.asarray(np.array(seq, np.int32))), len(first)


def _rw_post_body(y0_ref, y1_ref, r_ref, k_ref, v_ref, g_ref, rk_ref, gw_ref, gb_ref, hm_ref, hs_ref, o_ref):
    y = y0_ref[...] + y1_ref[...]
    mu = _dot2(y, hm_ref[...])
    yc = y - mu
    var = _dot2(yc * yc, hm_ref[...])
    yn = yc * lax.rsqrt(var + RWKV_GN_EPS) * gw_ref[...] + gb_ref[...]
    bonus = _dot2(r_ref[...] * k_ref[...] * rk_ref[...], hs_ref[...]) * v_ref[...]
    o_ref[...] = (yn + bonus) * g_ref[...]


def _rw_post(y0, y1, u, g, r_k, gn_w, gn_b, hmean, hsum, tm=256):
    t = u.shape[0]
    c = RWKV_WIDTH
    cb = COL_RKV // c
    return pl.pallas_call(
        _rw_post_body,
        grid=(t // tm,),
        in_specs=[pl.BlockSpec((tm, c), lambda i: (i, 0)),
                  pl.BlockSpec((tm, c), lambda i: (i, 0)),
                  pl.BlockSpec((tm, c), lambda i: (i, cb)),
                  pl.BlockSpec((tm, c), lambda i: (i, cb + 1)),
                  pl.BlockSpec((tm, c), lambda i: (i, cb + 2)),
                  pl.BlockSpec((tm, c), lambda i: (i, 0)),
                  _full(r_k), _full(gn_w), _full(gn_b), _full(hmean), _full(hsum)],
        out_specs=pl.BlockSpec((tm, c), lambda i: (i, 0)),
        out_shape=jax.ShapeDtypeStruct((t, c), f32),
        compiler_params=_cp(("arbitrary",)),
        name="rw_post",
    )(y0, y1, u, u, u, g, r_k, gn_w, gn_b, hmean, hsum)


def _outproj_body(*refs, nx, nctx):
    att_refs, hy_refs = refs[0:2], refs[2:4]
    rw_ref, w_ref = refs[4:6]
    x_refs = refs[6:6 + nx]
    g_ref, nw_ref, o_ref = refs[6 + nx:]
    o = (_dot(_pair_load(att_refs, nctx).astype(bf16), w_ref[0:ATT_WIDTH, :])
         + _dot(_pair_load(hy_refs, nctx).astype(bf16), w_ref[ATT_WIDTH:ATT_WIDTH + HY_CH, :])
         + _dot(rw_ref[...].astype(bf16), w_ref[ATT_WIDTH + HY_CH:, :]))
    y = o * lax.rsqrt(jnp.mean(o * o, axis=-1, keepdims=True) + RMS_EPS) * nw_ref[...]
    o_ref[...] = _pair_load(x_refs, nctx) + g_ref[...] * y


def _out_proj(atts, hys, rw, w_out, xs, mod, nw, rows, tm=256):
    t = rows.t
    d = w_out.shape[1]
    midx = rows.mod_index(tm)
    body = functools.partial(_outproj_body, nx=len(xs), nctx=rows.t_ctx // tm)
    return pl.pallas_call(
        body,
        grid=(t // tm,),
        in_specs=_pair_specs(atts, rows, tm) + _pair_specs(hys, rows, tm) + [
            pl.BlockSpec((tm, RWKV_WIDTH), lambda i: (i, 0)),
            _full(w_out)] + _pair_specs(xs, rows, tm) + [
            pl.BlockSpec((None, 1, d), lambda i: (midx(i), 0, 2)),
            pl.BlockSpec((None, 1, d), lambda i: (1, 0, 0))],
        out_specs=pl.BlockSpec((tm, d), lambda i: (i, 0)),
        out_shape=jax.ShapeDtypeStruct((t, d), f32),
        compiler_params=_cp(("arbitrary",)),
        name="out_proj",
    )(*atts, *hys, rw, w_out, *xs, mod, nw)


def _mlp_body(x_ref, sh_ref, sc_ref, g_ref, nw2_ref, nw3_ref, w1_ref, w2_ref, *rest, nctx):
    o_refs, (h_scr, acc_scr) = rest[:-2], rest[-2:]
    i = pl.program_id(0)
    j = pl.program_id(1)

    @pl.when(j == 0)
    def _():
        h_scr[...] = _normmod(x_ref[...], nw2_ref[...], sc_ref[...], sh_ref[...]).astype(bf16)
        acc_scr[...] = jnp.zeros_like(acc_scr)

    a = jnp.maximum(_dot(h_scr[...], w1_ref[...]), 0.0)
    acc_scr[...] += _dot((a * a).astype(bf16), w2_ref[...])

    def result():
        f = acc_scr[...]
        y = f * lax.rsqrt(jnp.mean(f * f, axis=-1, keepdims=True) + RMS_EPS) * nw3_ref[...]
        return x_ref[...] + g_ref[...] * y

    last = j == pl.num_programs(1) - 1
    if len(o_refs) == 1:
        @pl.when(last)
        def _():
            o_refs[0][...] = result()
    else:
        @pl.when(last & (i < nctx))
        def _():
            o_refs[0][...] = result()

        @pl.when(last & (i >= nctx))
        def _():
            o_refs[1][...] = result()


def _mlp(x, mod, nw, w1, w2, rows, split_out, tm=512, tf=1024):
    t, d = x.shape
    dff = w1.shape[1]
    midx = rows.mod_index(tm)
    nctx = rows.t_ctx // tm
    if split_out:
        out_specs = [pl.BlockSpec((tm, d), lambda i, j: (jnp.minimum(i, nctx - 1), 0)),
                     pl.BlockSpec((tm, d), lambda i, j: (jnp.maximum(i - nctx, 0), 0))]
        out_shape = [jax.ShapeDtypeStruct((rows.t_ctx, d), f32), jax.ShapeDtypeStruct((rows.t_lat, d), f32)]
    else:
        out_specs = pl.BlockSpec((tm, d), lambda i, j: (i, 0))
        out_shape = jax.ShapeDtypeStruct((t, d), f32)
    return pl.pallas_call(
        functools.partial(_mlp_body, nctx=nctx),
        grid=(t // tm, dff // tf),
        in_specs=[pl.BlockSpec((tm, d), lambda i, j: (i, 0)),
                  pl.BlockSpec((None, 1, d), lambda i, j: (midx(i), 0, 3)),
                  pl.BlockSpec((None, 1, d), lambda i, j: (midx(i), 0, 4)),
                  pl.BlockSpec((None, 1, d), lambda i, j: (midx(i), 0, 5)),
                  pl.BlockSpec((None, 1, d), lambda i, j: (2, 0, 0)),
                  pl.BlockSpec((None, 1, d), lambda i, j: (3, 0, 0)),
                  pl.BlockSpec((d, tf), lambda i, j: (0, j)),
                  pl.BlockSpec((tf, d), lambda i, j: (j, 0))],
        out_specs=out_specs,
        out_shape=out_shape,
        scratch_shapes=[pltpu.VMEM((tm, d), bf16), pltpu.VMEM((tm, d), f32)],
        compiler_params=_cp(("arbitrary", "arbitrary")),
        name="mlp",
    )(x, mod, mod, mod, nw, nw, w1, w2)


def _lora_weight(w_up, a_up, g_up):
    c = RWKV_WIDTH
    w = jnp.zeros((LORA_IN, 5 * c), f32)
    r = 0
    for j, blk in enumerate((w_up[0], w_up[1], a_up[0], a_up[1], g_up)):
        w = w.at[r:r + blk.shape[0], j * c:(j + 1) * c].set(blk)
        r += blk.shape[0]
    return w.astype(bf16)


def _regroup_in_cols(w):
    kv0 = ATT_WIDTH
    hy0 = kv0 + 2 * ATT_KV
    rw0 = hy0 + 3 * HY_CH
    lo0 = rw0 + 3 * RWKV_WIDTH
    return jnp.concatenate([w[..., :kv0], w[..., rw0:lo0], w[..., hy0:rw0], w[..., kv0:hy0], w[..., lo0:]], axis=-1)


def _conv_taps(hy_short_w, rw_short_w):
    def ident(n):
        return jnp.zeros((3, n), f32).at[1].set(1.0)
    return jnp.concatenate([ident(ATT_WIDTH), rw_short_w, hy_short_w, ident(2 * ATT_KV + LORA_IN)], axis=1)


def kernel(x_prompt, x_sample, cache_k, cache_v, state_rwkv, c, c_ctx, ada_w, ada_b, norm_w, w_in, w_out,
           attn_sink, hy_short_w, hy_f1, hy_b1, hy_f2, hy_b2, hy_f3, hy_decay, hy_skip, rw_short_w, rw_w0,
           rw_w_up, rw_a0, rw_a_up, rw_g_up, rw_k_k, rw_k_a, rw_r_k, rw_gn_w, rw_gn_b, mlp_w1, mlp_w2):
    b_ctx, l_ctx, d = x_prompt.shape
    b_lat, l_lat, _ = x_sample.shape
    depth = ada_w.shape[0]
    grid_w = 64
    rows = _Rows(b_ctx, l_ctx, b_lat, l_lat)

    xs = (x_prompt.reshape(rows.t_ctx, d), x_sample.reshape(rows.t_lat, d))
    cond16 = jnp.zeros((16, d), f32).at[0].set(c_ctx).at[1:1 + b_lat].set(c)
    mod_all = _modulation(cond16, ada_w, ada_b[:, None, :])

    cos_t, sin_t = _rope_tables(l_lat, grid_w)
    hsum = _head_sum_matrix()
    hmean = _head_sum_matrix(1.0 / HEAD_DIM)
    scan_tabs, n_steps = _scan_tables(rows)
    dft = {}
    for l in (l_ctx, l_lat):
        cm, sm = _dft_tables(l)
        feat, t01 = _hy_feat(l)
        dft[l] = dict(cm=cm, sm=sm, cmb=cm.astype(bf16), smb=sm.astype(bf16),
                      cmtb=cm.T.astype(bf16), smtb=sm.T.astype(bf16), feat=feat, t01=t01)

    new_k, new_v, new_s = [], [], []
    for l in range(depth):
        mod = mod_all[l].reshape(16, 1, N_MOD * d)
        nw = norm_w[l].reshape(4, 1, d)
        u = _in_proj(xs, mod, nw, _regroup_in_cols(w_in[l]).astype(bf16),
                     _conv_taps(hy_short_w[l], rw_short_w[l]), rows)

        new_k.append(u[:rows.t_ctx, COL_KV:COL_KV + ATT_KV].reshape(b_ctx, l_ctx, N_KV_HEADS, HEAD_DIM))
        new_v.append(u[:rows.t_ctx, COL_KV + ATT_KV:COL_KV + 2 * ATT_KV].reshape(b_ctx, l_ctx, N_KV_HEADS, HEAD_DIM))

        att_c = _attn_ctx(u, attn_sink[l], rows)
        att_l = _attn_lat(u, cache_k[:, l].reshape(b_lat, -1, ATT_KV), cache_v[:, l].reshape(b_lat, -1, ATT_KV),
                          attn_sink[l], cos_t, sin_t, rows)

        f1p = jnp.pad(hy_f1[l], ((0, 128 - HY_EMB), (0, 0)))
        hys = []
        for (bb, ll, rb0) in ((b_ctx, l_ctx, 0), (b_lat, l_lat, rows.t_ctx // l_lat)):
            tb = dft[ll]
            hr, hi = _hy_filter(ll, tb['feat'], tb['t01'], f1p, hy_b1[l][None], hy_f2[l], hy_b2[l][None],
                                hy_f3[l], hy_decay[l][None], tb['cm'], tb['sm'])
            hys.append(_hy_conv(u, hr, hi, hy_skip[l], tb['cmb'], tb['smb'], tb['cmtb'], tb['smtb'],
                                bb, ll, rb0))

        w_lora = _lora_weight(rw_w_up[l], rw_a_up[l], rw_g_up[l])
        kk, lw, a, g = _rw_prep(u, w_lora, rw_w0[l], rw_a0[l], rw_k_k[l][None], hsum)
        s0 = jnp.concatenate([jnp.zeros((b_ctx, 2, N_RWKV_HEADS, HEAD_DIM, HEAD_DIM), f32),
                              state_rwkv[:, l]], axis=0)
        y0, y1, sfin = _rw_scan(u, kk, lw, a, rw_k_a[l][None], s0, scan_tabs, n_steps)
        new_s.append(sfin[:b_ctx])
        rw = _rw_post(y0, y1, u, g, rw_r_k[l].reshape(1, RWKV_WIDTH), rw_gn_w[l][None], rw_gn_b[l][None],
                      hmean, hsum)

        x = _out_proj((att_c, att_l), tuple(hys), rw, w_out[l].astype(bf16), xs, mod, nw, rows)
        x = _mlp(x, mod, nw, mlp_w1[l].astype(bf16), mlp_w2[l].astype(bf16), rows, split_out=(l == depth - 1))
        xs = (x,)

    y_p, y_s = x
    return (y_p.reshape(b_ctx, l_ctx, d), y_s.reshape(b_lat, l_lat, d),
            jnp.stack(new_k, axis=1), jnp.stack(new_v, axis=1), jnp.stack(new_s, axis=1))
```

```python
import functools
import math

import numpy as np
import jax
import jax.numpy as jnp
from jax import lax
from jax.experimental import pallas as pl
from jax.experimental.pallas import tpu as pltpu

f32 = jnp.float32
bf16 = jnp.bfloat16

HEAD_DIM = 64
N_ATT_HEADS = 12
N_KV_HEADS = 4
ATT_GROUP = 3
ATT_WIDTH = 768
ATT_KV = 256
WINDOW = 128
ROPE_THETA = 10000.0
NEG_INF = -1e30
HY_CH = 512
HY_BANDS = 16
HY_EMB = 33
HY_FFN = 64
HY_MOD_SHIFT = 0.05
RWKV_WIDTH = 768
N_RWKV_HEADS = 12
DECAY_LORA = 96
AAA_LORA = 96
GATE_LORA = 256
RWKV_GN_EPS = 64e-5
N_MOD = 6
RMS_EPS = 1e-6
CHUNK = 64
CONV_ROWS = 256
LORA_IN = 2 * DECAY_LORA + 2 * AAA_LORA + GATE_LORA
COL_Q = 0
COL_RKV = COL_Q + ATT_WIDTH
COL_HY = COL_RKV + 3 * RWKV_WIDTH
COL_KV = COL_HY + 3 * HY_CH
COL_LORA = COL_KV + 2 * ATT_KV
IN_COLS = COL_LORA + LORA_IN
HI = lax.Precision.HIGHEST
VMEM_LIMIT = 56 * 1024 * 1024


def _cp(sem, vmem=VMEM_LIMIT):
    return pltpu.CompilerParams(dimension_semantics=sem, vmem_limit_bytes=vmem)


def _dot(a, b):
    return jnp.dot(a, b, preferred_element_type=f32)


def _dot_nt(a, b):
    return lax.dot_general(a, b, (((1,), (1,)), ((), ())), preferred_element_type=f32)


def _dot_tn(a, b):
    return lax.dot_general(a, b, (((0,), (0,)), ((), ())), preferred_element_type=f32)


def _dot_hi(a, b):
    return jnp.dot(a, b, preferred_element_type=f32, precision=HI)


def _full(a):
    return pl.BlockSpec(a.shape, lambda *_: (0,) * a.ndim)


def _mod_body(c_ref, w_ref, b_ref, o_ref):
    c = c_ref[...]
    s = (c * jax.nn.sigmoid(c)).astype(bf16)
    o_ref[...] = _dot(s, w_ref[...].astype(bf16)) + b_ref[...]


def _modulation(cond16, ada_w, ada_b):
    depth, d, n = ada_w.shape
    tn = 1024
    return pl.pallas_call(
        _mod_body,
        grid=(depth, n // tn),
        in_specs=[pl.BlockSpec((16, d), lambda l, j: (0, 0)),
                  pl.BlockSpec((None, d, tn), lambda l, j: (l, 0, j)),
                  pl.BlockSpec((None, 1, tn), lambda l, j: (l, 0, j))],
        out_specs=pl.BlockSpec((None, 16, tn), lambda l, j: (l, 0, j)),
        out_shape=jax.ShapeDtypeStruct((depth, 16, n), f32),
        compiler_params=_cp(("arbitrary", "arbitrary")),
        name="modulation",
    )(cond16, ada_w, ada_b)


def _normmod(x, nw, sc, sh):
    y = x * lax.rsqrt(jnp.mean(x * x, axis=-1, keepdims=True) + RMS_EPS) * nw
    return y * (1.0 + sc) + sh


class _Rows:
    def __init__(self, b_ctx, l_ctx, b_lat, l_lat):
        self.b_ctx, self.l_ctx, self.b_lat, self.l_lat = b_ctx, l_ctx, b_lat, l_lat
        self.t_ctx = b_ctx * l_ctx
        self.t_lat = b_lat * l_lat
        self.t = self.t_ctx + self.t_lat

    def mod_index(self, tm):
        nctx = self.t_ctx // tm
        per = self.l_lat // tm

        def f(i):
            return jnp.where(i < nctx, 0, 1 + (i - nctx) // per)
        return f


def _pair_specs(arrs, rows, tm, single_buffer=False):
    w = arrs[0].shape[1]
    if len(arrs) == 1:
        return [pl.BlockSpec((tm, w), lambda i, *_: (i, 0))]
    nctx = rows.t_ctx // tm
    mode = dict(pipeline_mode=pl.Buffered(1)) if single_buffer else {}
    return [pl.BlockSpec((tm, w), lambda i, *_: (jnp.minimum(i, nctx - 1), 0), **mode),
            pl.BlockSpec((tm, w), lambda i, *_: (jnp.maximum(i - nctx, 0), 0), **mode)]


def _pair_load(refs, nctx):
    if len(refs) == 1:
        return refs[0][...]
    return jnp.where(pl.program_id(0) < nctx, refs[0][...], refs[1][...])


def _inproj_body(*refs, nx, nctx, lmask_ctx, lmask_lat):
    x_refs = refs[:nx]
    sh_ref, sc_ref, nw_ref, w_ref, wc_ref, o_ref, h_scr, u_scr = refs[nx:]
    i = pl.program_id(0)

    @pl.when(pl.program_id(1) == 0)
    def _():
        h_scr[...] = _normmod(_pair_load(x_refs, nctx), nw_ref[...], sc_ref[...], sh_ref[...]).astype(bf16)

    u_scr[...] = _dot(h_scr[...], w_ref[...])
    tm = u_scr.shape[0]
    ch = CONV_ROWS
    lmask = jnp.where(i < nctx, lmask_ctx, lmask_lat)
    row = lax.broadcasted_iota(jnp.int32, (ch, 1), 0)
    wc = wc_ref[...]
    for r0 in range(0, tm, ch):
        x = u_scr[r0:r0 + ch, :]
        at_start = (r0 & lmask) == 0
        at_end = ((r0 + ch - 1) & lmask) == lmask
        prev = jnp.where(at_start, 0.0, u_scr[max(r0 - 1, 0):max(r0 - 1, 0) + 1, :])
        nxt = jnp.where(at_end, 0.0, u_scr[min(r0 + ch, tm - 1):min(r0 + ch, tm - 1) + 1, :])
        xm = jnp.where(row == 0, prev, pltpu.roll(x, 1, 0))
        xp = jnp.where(row == ch - 1, nxt, pltpu.roll(x, ch - 1, 0))
        o_ref[r0:r0 + ch, :] = xm * wc[0:1, :] + x * wc[1:2, :] + xp * wc[2:3, :]


def _in_proj(xs, mod, nw, w_in, layer, wconv, rows, tm=1024, tn=1152):
    t = rows.t
    _, d, n = w_in.shape
    for l in (rows.l_ctx, rows.l_lat):
        assert l & (l - 1) == 0 and tm % l == 0 and l % CONV_ROWS == 0
    midx = rows.mod_index(tm)
    body = functools.partial(_inproj_body, nx=len(xs), nctx=rows.t_ctx // tm,
                             lmask_ctx=rows.l_ctx - 1, lmask_lat=rows.l_lat - 1)
    return pl.pallas_call(
        body,
        grid=(t // tm, n // tn),
        in_specs=_pair_specs(xs, rows, tm, single_buffer=True) + [
            pl.BlockSpec((None, 1, d), lambda i, j: (midx(i), 0, 0)),
            pl.BlockSpec((None, 1, d), lambda i, j: (midx(i), 0, 1)),
            pl.BlockSpec((None, 1, d), lambda i, j: (0, 0, 0)),
            pl.BlockSpec((None, d, tn), lambda i, j: (layer, 0, j)),
            pl.BlockSpec((3, tn), lambda i, j: (0, j))],
        out_specs=pl.BlockSpec((tm, tn), lambda i, j: (i, j)),
        out_shape=jax.ShapeDtypeStruct((t, n), f32),
        scratch_shapes=[pltpu.VMEM((tm, d), bf16), pltpu.VMEM((tm, tn), f32)],
        compiler_params=_cp(("arbitrary", "arbitrary")),
        name="in_proj",
    )(*xs, mod, mod, nw, w_in, wconv)


def _softmax_pv(s_list, v_list, sink_col):
    m = sink_col
    for s in s_list:
        m = jnp.maximum(m, jnp.max(s, axis=-1, keepdims=True))
    den = jnp.exp(sink_col - m)
    acc = None
    for s, v in zip(s_list, v_list):
        p = jnp.exp(s - m)
        den = den + jnp.sum(p, axis=-1, keepdims=True)
        pv = _dot(p.astype(bf16), v)
        acc = pv if acc is None else acc + pv
    return acc / den


def _attn_ctx_body(sink_ref, q_ref, k_ref, v_ref, o_ref):
    l = q_ref.shape[0]
    scale = HEAD_DIM ** -0.5
    for n in range(N_KV_HEADS):
        kn = k_ref[:, n * HEAD_DIM:(n + 1) * HEAD_DIM].astype(bf16)
        vn = v_ref[:, n * HEAD_DIM:(n + 1) * HEAD_DIM].astype(bf16)
        for g in range(ATT_GROUP):
            h = n * ATT_GROUP + g
            q = (q_ref[:, h * HEAD_DIM:(h + 1) * HEAD_DIM] * scale).astype(bf16)
            s = _dot_nt(q, kn)
            sink_col = jnp.full((l, 1), sink_ref[h], f32)
            o_ref[:, h * HEAD_DIM:(h + 1) * HEAD_DIM] = _softmax_pv([s], [vn], sink_col)


def _attn_ctx(u, sink, rows):
    b, l = rows.b_ctx, rows.l_ctx
    return pl.pallas_call(
        _attn_ctx_body,
        grid=(b,),
        in_specs=[pl.BlockSpec(memory_space=pltpu.SMEM),
                  pl.BlockSpec((l, ATT_WIDTH), lambda i: (i, COL_Q // ATT_WIDTH)),
                  pl.BlockSpec((l, ATT_KV), lambda i: (i, COL_KV // ATT_KV)),
                  pl.BlockSpec((l, ATT_KV), lambda i: (i, COL_KV // ATT_KV + 1))],
        out_specs=pl.BlockSpec((l, ATT_WIDTH), lambda i: (i, 0)),
        out_shape=jax.ShapeDtypeStruct((b * l, ATT_WIDTH), f32),
        compiler_params=_cp(("arbitrary",)),
        name="attn_ctx",
    )(sink, u, u, u)


def _rope(x, cos, sin_signed):
    w = x.shape[1]
    lane = lax.broadcasted_iota(jnp.int32, (1, w), 1)
    swapped = jnp.where((lane % 32) < 16, pltpu.roll(x, w - 16, 1), pltpu.roll(x, 16, 1))
    return x * cos + swapped * sin_signed


def _attn_lat_body(sink_ref, q_ref, k_ref, v_ref, kc_ref, vc_ref, cos_ref, sin_ref, o_ref,
                   q_scr, k_scr, v_scr):
    l = q_ref.shape[0]
    blk = WINDOW
    nb = l // blk
    scale = HEAD_DIM ** -0.5
    cos = cos_ref[...]
    sin = sin_ref[...]
    q_scr[...] = (_rope(q_ref[...], cos, sin) * scale).astype(bf16)
    zpad = jnp.zeros((blk, ATT_KV), bf16)
    k_scr[0:blk, :] = zpad
    k_scr[blk + l:, :] = zpad
    v_scr[0:blk, :] = zpad
    v_scr[blk + l:, :] = zpad
    k_scr[blk:blk + l, :] = _rope(k_ref[...], cos[:, :ATT_KV], sin[:, :ATT_KV]).astype(bf16)
    v_scr[blk:blk + l, :] = v_ref[...].astype(bf16)
    r = lax.broadcasted_iota(jnp.int32, (ATT_GROUP * blk, 3 * blk), 0) % blk
    c = lax.broadcasted_iota(jnp.int32, (ATT_GROUP * blk, 3 * blk), 1)
    band = (c - r >= 0) & (c - r <= 2 * WINDOW)
    for i in range(nb):
        kpos = (i - 1) * blk + c
        mask = band & (kpos >= 0) & (kpos < l)
        for n in range(N_KV_HEADS):
            cs = slice(n * HEAD_DIM, (n + 1) * HEAD_DIM)
            kl = k_scr[i * blk:(i + 3) * blk, cs]
            vl = v_scr[i * blk:(i + 3) * blk, cs]
            kc = kc_ref[:, cs].astype(bf16)
            vc = vc_ref[:, cs].astype(bf16)
            qs = jnp.concatenate(
                [q_scr[i * blk:(i + 1) * blk, (n * ATT_GROUP + g) * HEAD_DIM:(n * ATT_GROUP + g + 1) * HEAD_DIM]
                 for g in range(ATT_GROUP)], axis=0)
            s_loc = jnp.where(mask, _dot_nt(qs, kl), NEG_INF)
            s_ctx = _dot_nt(qs, kc)
            sink_col = jnp.concatenate(
                [jnp.full((blk, 1), sink_ref[n * ATT_GROUP + g], f32) for g in range(ATT_GROUP)], axis=0)
            o = _softmax_pv([s_loc, s_ctx], [vl, vc], sink_col)
            for g in range(ATT_GROUP):
                h = n * ATT_GROUP + g
                o_ref[i * blk:(i + 1) * blk, h * HEAD_DIM:(h + 1) * HEAD_DIM] = o[g * blk:(g + 1) * blk, :]


def _attn_lat(u, kc, vc, sink, cos, sin, rows):
    b, l = rows.b_lat, rows.l_lat
    assert rows.t_ctx % l == 0
    rb0 = rows.t_ctx // l
    lc = kc.shape[1]
    return pl.pallas_call(
        _attn_lat_body,
        grid=(b,),
        in_specs=[pl.BlockSpec(memory_space=pltpu.SMEM),
                  pl.BlockSpec((l, ATT_WIDTH), lambda i: (rb0 + i, COL_Q // ATT_WIDTH)),
                  pl.BlockSpec((l, ATT_KV), lambda i: (rb0 + i, COL_KV // ATT_KV)),
                  pl.BlockSpec((l, ATT_KV), lambda i: (rb0 + i, COL_KV // ATT_KV + 1)),
                  pl.BlockSpec((None, lc, ATT_KV), lambda i: (i, 0, 0)),
                  pl.BlockSpec((None, lc, ATT_KV), lambda i: (i, 0, 0)),
                  pl.BlockSpec((l, ATT_WIDTH), lambda i: (0, 0)),
                  pl.BlockSpec((l, ATT_WIDTH), lambda i: (0, 0))],
        out_specs=pl.BlockSpec((l, ATT_WIDTH), lambda i: (i, 0)),
        out_shape=jax.ShapeDtypeStruct((b * l, ATT_WIDTH), f32),
        scratch_shapes=[pltpu.VMEM((l, ATT_WIDTH), bf16),
                        pltpu.VMEM((l + 2 * WINDOW, ATT_KV), bf16),
                        pltpu.VMEM((l + 2 * WINDOW, ATT_KV), bf16)],
        compiler_params=_cp(("arbitrary",)),
        name="attn_lat",
    )(sink, u, u, u, kc, vc, cos, sin)


def _rope_tables(l, grid_w):
    half = HEAD_DIM // 2
    pos = jnp.arange(l)
    rowp = (pos // grid_w).astype(f32)
    colp = (pos % grid_w).astype(f32)
    freqs = ROPE_THETA ** (-jnp.arange(0, half, 2, dtype=f32) / half)
    ar = rowp[:, None] * freqs[None, :]
    ac = colp[:, None] * freqs[None, :]
    cos = jnp.concatenate([jnp.cos(ar), jnp.cos(ar), jnp.cos(ac), jnp.cos(ac)], axis=-1)
    sin = jnp.concatenate([-jnp.sin(ar), jnp.sin(ar), -jnp.sin(ac), jnp.sin(ac)], axis=-1)
    return jnp.tile(cos, (1, N_ATT_HEADS)), jnp.tile(sin, (1, N_ATT_HEADS))


def _dft_tables(l):
    k = jnp.arange(l, dtype=jnp.int32)[:, None]
    s = jnp.arange(l, dtype=jnp.int32)[None, :]
    m = ((2 * k + 1) * s) % (4 * l)
    theta = m.astype(f32) * (math.pi / (2 * l))
    return jnp.cos(theta), jnp.sin(theta)


def _hy_feat(l):
    t = jnp.arange(l, dtype=f32)
    t01 = (t / max(l - 1, 1))[:, None]
    bands = jnp.linspace(1e-4, HY_BANDS - 1, HY_BANDS, dtype=f32)
    ang = (2.0 * math.pi / l) * t[:, None] * bands[None, :]
    feat = jnp.concatenate([t01, jnp.cos(ang), -jnp.sin(ang)], axis=-1)
    return jnp.pad(feat, ((0, 0), (0, 128 - HY_EMB))), t01


def _hy_filter_body(feat_ref, t01_ref, f1_ref, b1_ref, f2_ref, b2_ref, f3_ref, dec_ref, cm_ref, sm_ref,
                    hr_ref, hi_ref):
    h = jnp.sin(_dot_hi(feat_ref[...], f1_ref[...]) + b1_ref[...])
    h = jnp.sin(_dot_hi(h, f2_ref[...]) + b2_ref[...])
    h = _dot_hi(h, f3_ref[...])
    h = h * (jnp.exp(-t01_ref[...] * jnp.abs(dec_ref[...])) + HY_MOD_SHIFT)
    l = h.shape[0]
    fwd = h[:, :HY_CH]
    row = lax.broadcasted_iota(jnp.int32, (l, 1), 0)
    bwd = jnp.where(row == 0, 0.0, h[:, HY_CH:])
    hr_ref[...] = _dot_hi(cm_ref[...], fwd + bwd)
    hi_ref[...] = _dot_hi(sm_ref[...], bwd - fwd)


def _hy_filter(l, feat, t01, f1p, b1, f2, b2, f3, dec, cm, sm):
    args = (feat, t01, f1p, b1, f2, b2, f3, dec, cm, sm)
    return pl.pallas_call(
        _hy_filter_body,
        grid=(1,),
        in_specs=[_full(a) for a in args],
        out_specs=[pl.BlockSpec((l, HY_CH), lambda i: (0, 0))] * 2,
        out_shape=[jax.ShapeDtypeStruct((l, HY_CH), f32)] * 2,
        compiler_params=_cp(("arbitrary",)),
        name="hy_filter",
    )(*args)


def _hy_conv_body(x0_ref, x1_ref, v_ref, hr_ref, hi_ref, skip_ref, cm_ref, sm_ref, cmt_ref, smt_ref, o_ref):
    l = v_ref.shape[0]
    z = v_ref[...] * x1_ref[...]
    zb = z.astype(bf16)
    zr = _dot(cm_ref[...], zb)
    zs = _dot(sm_ref[...], zb)
    hr = hr_ref[...]
    hi = hi_ref[...]
    yr = (zr * hr + zs * hi).astype(bf16)
    yi = (zr * hi - zs * hr).astype(bf16)
    y = (_dot(cmt_ref[...], yr) - _dot(smt_ref[...], yi)) * (1.0 / l)
    o_ref[...] = (y + skip_ref[...] * z) * x0_ref[...]


def _hy_conv(u, hr, hi, skip, cm, sm, cmt, smt, b, l, rb0):
    cb = COL_HY // HY_CH
    return pl.pallas_call(
        _hy_conv_body,
        grid=(b,),
        in_specs=[pl.BlockSpec((l, HY_CH), lambda i: (rb0 + i, cb)),
                  pl.BlockSpec((l, HY_CH), lambda i: (rb0 + i, cb + 1)),
                  pl.BlockSpec((l, HY_CH), lambda i: (rb0 + i, cb + 2)),
                  _full(hr), _full(hi), _full(skip), _full(cm), _full(sm), _full(cmt), _full(smt)],
        out_specs=pl.BlockSpec((l, HY_CH), lambda i: (i, 0)),
        out_shape=jax.ShapeDtypeStruct((b * l, HY_CH), f32),
        compiler_params=_cp(("arbitrary",)),
        name="hy_conv",
    )(u, u, u, hr, hi, skip, cm, sm, cmt, smt)


def _head_sum_matrix(scale=1.0):
    i = np.arange(RWKV_WIDTH) // HEAD_DIM
    return jnp.asarray((i[:, None] == i[None, :]).astype(np.float32) * scale).astype(bf16)


def _dot2(x, m):
    hi = x.astype(bf16)
    lo = (x - hi.astype(f32)).astype(bf16)
    return _dot(hi, m) + _dot(lo, m)


def _rw_prep_body(lo_ref, k_ref, wl_ref, w0_ref, a0_ref, kk_ref_, hs_ref, kk_o, lw_o, a_o, g_o):
    x = lo_ref[...]
    lane = lax.broadcasted_iota(jnp.int32, x.shape, 1)
    act = jnp.where(lane < 2 * DECAY_LORA, jnp.tanh(x),
                    jnp.where(lane < 2 * DECAY_LORA + 2 * AAA_LORA, x, jax.nn.sigmoid(x)))
    lo = _dot(act.astype(bf16), wl_ref[...])
    c = RWKV_WIDTH
    for d in range(2):
        zneg = -(w0_ref[d:d + 1, :] + lo[:, d * c:(d + 1) * c])
        softplus = jnp.maximum(zneg, 0.0) + jnp.log(1.0 + jnp.exp(-jnp.abs(zneg)))
        w_log = -softplus - 0.5
        lw_o[d] = -jnp.exp(w_log)
        a_o[d] = jax.nn.sigmoid(a0_ref[d:d + 1, :] + lo[:, (2 + d) * c:(3 + d) * c])
    g_o[...] = lo[:, 4 * c:5 * c]
    kk = k_ref[...] * kk_ref_[...]
    ss = _dot2(kk * kk, hs_ref[...])
    kk_o[...] = kk / jnp.maximum(jnp.sqrt(ss), 1e-12)


def _rw_prep(u, w_lora, w0, a0, k_k, hsum, tm=256):
    t = u.shape[0]
    c = RWKV_WIDTH
    return pl.pallas_call(
        _rw_prep_body,
        grid=(t // tm,),
        in_specs=[pl.BlockSpec((tm, LORA_IN), lambda i: (i, COL_LORA // LORA_IN)),
                  pl.BlockSpec((tm, c), lambda i: (i, COL_RKV // c + 1)),
                  _full(w_lora), _full(w0), _full(a0), _full(k_k), _full(hsum)],
        out_specs=[pl.BlockSpec((tm, c), lambda i: (i, 0)),
                   pl.BlockSpec((2, tm, c), lambda i: (0, i, 0)),
                   pl.BlockSpec((2, tm, c), lambda i: (0, i, 0)),
                   pl.BlockSpec((tm, c), lambda i: (i, 0))],
        out_shape=[jax.ShapeDtypeStruct((t, c), f32),
                   jax.ShapeDtypeStruct((2, t, c), f32),
                   jax.ShapeDtypeStruct((2, t, c), f32),
                   jax.ShapeDtypeStruct((t, c), f32)],
        compiler_params=_cp(("arbitrary",)),
        name="rw_prep",
    )(u, u, w_lora, w0, a0, k_k, hsum)


def _split3(x):
    hi = x.astype(bf16)
    r1 = x - hi.astype(f32)
    mid = r1.astype(bf16)
    lo = (r1 - mid.astype(f32)).astype(bf16)
    return hi, mid, lo


def _rw_dir_operands(d, r_ref, k_ref, v_ref, kk_ref, lw_ref, a_ref, ka):
    cdim = CHUNK
    row = lax.broadcasted_iota(jnp.int32, (2 * cdim, 4 * cdim), 0)
    col = lax.broadcasted_iota(jnp.int32, (2 * cdim, 4 * cdim), 1) % cdim
    lag = (row % cdim) - col if d == 0 else col - (row % cdim)
    mask = lag >= jnp.where(row < cdim, 1, 0)
    row1 = lax.broadcasted_iota(jnp.int32, (cdim, cdim), 0)
    col1 = lax.broadcasted_iota(jnp.int32, (cdim, cdim), 1)
    incl = (col1 <= row1) if d == 0 else (col1 >= row1)
    lw = lw_ref[...]
    cs = sum(_dot(incl.astype(bf16), part) for part in _split3(lw))
    tot = jnp.sum(lw, axis=0, keepdims=True)
    a = a_ref[...]
    kk = kk_ref[...]
    beta = kk * a
    kdir = k_ref[...] * (1.0 + (a - 1.0) * ka)
    p_inv = jnp.exp(-cs)
    p_rest = jnp.exp(tot - cs)
    return dict(
        mask=mask, p_all=jnp.exp(tot),
        kap=(kk * jnp.exp(cs - lw)).astype(bf16), r=(r_ref[...] * jnp.exp(cs)).astype(bf16),
        beta=(beta * p_inv).astype(bf16), kdir=(kdir * p_inv).astype(bf16),
        beta_e=(beta * p_rest).astype(bf16), kdir_e=(kdir * p_rest).astype(bf16),
        v=v_ref[...].astype(bf16))


def _bd(x):
    lane = lax.broadcasted_iota(jnp.int32, x.shape, 1)
    zero = jnp.zeros_like(x)
    return jnp.concatenate([jnp.where(lane < HEAD_DIM, x, zero), jnp.where(lane >= HEAD_DIM, x, zero)], axis=0)


def _rw_scan_body(rb_ref, flag_ref, seq_ref,
                  r0_ref, k0_ref, v0_ref, kk0_ref, lw0_ref, a0_ref,
                  r1_ref, k1_ref, v1_ref, kk1_ref, lw1_ref, a1_ref, ka_ref, s0_ref,
                  y0_ref, y1_ref, sf_ref, s_scr):
    s = pl.program_id(0)
    cdim = CHUNK
    pw = 2 * HEAD_DIM
    n_pairs = N_RWKV_HEADS // 2
    chains = [(d, p) for d in range(2) for p in range(n_pairs)]
    n = range(len(chains))
    flag = flag_ref[s]

    @pl.when((flag & 1) == 1)
    def _():
        z = jnp.zeros((HEAD_DIM, HEAD_DIM), f32)
        for d, p in chains:
            s_scr[d, p] = jnp.concatenate(
                [jnp.concatenate([s0_ref[d, 2 * p], z], axis=1),
                 jnp.concatenate([z, s0_ref[d, 2 * p + 1]], axis=1)], axis=0)

    ka = ka_ref[...]
    ops = (_rw_dir_operands(0, r0_ref, k0_ref, v0_ref, kk0_ref, lw0_ref, a0_ref, ka),
           _rw_dir_operands(1, r1_ref, k1_ref, v1_ref, kk1_ref, lw1_ref, a1_ref, ka))

    def sl(name, i):
        d, p = chains[i]
        return ops[d][name][:, p * pw:(p + 1) * pw]

    s_old = [s_scr[d, p] for d, p in chains]
    s_b = [x.astype(bf16) for x in s_old]
    lhs = [jnp.concatenate([sl('kap', i), sl('r', i)], axis=0) for i in n]
    rhs = [jnp.concatenate([_bd(sl('beta', i)), _bd(sl('kdir', i))], axis=0) for i in n]
    mm = [jnp.where(ops[chains[i][0]]['mask'], _dot_nt(lhs[i], rhs[i]), 0.0).astype(bf16) for i in n]
    sk = [_dot_nt(lhs[i], s_b[i]) for i in n]
    av = [_dot(mm[i][:, 2 * cdim:], _bd(sl('v', i))) for i in n]
    x = [-mm[i][:cdim, :2 * cdim] for i in n]
    m_rb = [mm[i][cdim:, :2 * cdim] for i in n]
    uu = [sk[i][:cdim] + av[i][:cdim] for i in n]
    n_sq = int(math.log2(cdim))
    for it in range(n_sq):
        uu = [uu[i] + _dot(x[i], _bd(uu[i].astype(bf16))) for i in n]
        if it + 1 < n_sq:
            x = [_dot(x[i], _bd(x[i])).astype(bf16) for i in n]
    u_b = [(-uu[i]).astype(bf16) for i in n]
    y = [sk[i][cdim:] + _dot(m_rb[i], _bd(u_b[i])) + av[i][cdim:] for i in n]
    y0_ref[...] = jnp.concatenate(y[:n_pairs], axis=1)
    y1_ref[...] = jnp.concatenate(y[n_pairs:], axis=1)
    row = lax.broadcasted_iota(jnp.int32, (pw, pw), 0)
    col = lax.broadcasted_iota(jnp.int32, (pw, pw), 1)
    same_head = (row // HEAD_DIM) == (col // HEAD_DIM)
    s_new = []
    for i in n:
        d, p = chains[i]
        upd = _dot_tn(jnp.concatenate([u_b[i], sl('v', i)], axis=0),
                      jnp.concatenate([sl('beta_e', i), sl('kdir_e', i)], axis=0))
        s_new.append(s_old[i] * ops[d]['p_all'][:, p * pw:(p + 1) * pw] + jnp.where(same_head, upd, 0.0))
        s_scr[d, p] = s_new[i]

    @pl.when((flag & 2) == 2)
    def _():
        for i in n:
            d, p = chains[i]
            sf_ref[d, 2 * p] = s_new[i][:HEAD_DIM, :HEAD_DIM]
            sf_ref[d, 2 * p + 1] = s_new[i][HEAD_DIM:, HEAD_DIM:]


def _rw_scan(u, kk, lw, a, k_a, s0, tabs, n_steps):
    c = RWKV_WIDTH
    t = u.shape[0]
    nseq = s0.shape[0]
    cb = COL_RKV // c

    def dir_specs(d):
        return [pl.BlockSpec((CHUNK, c), lambda s, rb, fi, sq: (rb[d, s], cb)),
                pl.BlockSpec((CHUNK, c), lambda s, rb, fi, sq: (rb[d, s], cb + 1)),
                pl.BlockSpec((CHUNK, c), lambda s, rb, fi, sq: (rb[d, s], cb + 2)),
                pl.BlockSpec((CHUNK, c), lambda s, rb, fi, sq: (rb[d, s], 0)),
                pl.BlockSpec((None, CHUNK, c), lambda s, rb, fi, sq: (d, rb[d, s], 0)),
                pl.BlockSpec((None, CHUNK, c), lambda s, rb, fi, sq: (d, rb[d, s], 0))]

    state_spec = pl.BlockSpec((None, 2, N_RWKV_HEADS, HEAD_DIM, HEAD_DIM), lambda s, rb, fi, sq: (sq[s], 0, 0, 0, 0))
    grid_spec = pltpu.PrefetchScalarGridSpec(
        num_scalar_prefetch=3,
        grid=(n_steps,),
        in_specs=dir_specs(0) + dir_specs(1) + [pl.BlockSpec((1, c), lambda s, rb, fi, sq: (0, 0)), state_spec],
        out_specs=[pl.BlockSpec((CHUNK, c), lambda s, rb, fi, sq: (rb[0, s], 0)),
                   pl.BlockSpec((CHUNK, c), lambda s, rb, fi, sq: (rb[1, s], 0)),
                   state_spec],
        scratch_shapes=[pltpu.VMEM((2, N_RWKV_HEADS // 2, 2 * HEAD_DIM, 2 * HEAD_DIM), f32)],
    )
    return pl.pallas_call(
        _rw_scan_body,
        grid_spec=grid_spec,
        out_shape=[jax.ShapeDtypeStruct((t, c), f32), jax.ShapeDtypeStruct((t, c), f32),
                   jax.ShapeDtypeStruct((nseq, 2, N_RWKV_HEADS, HEAD_DIM, HEAD_DIM), f32)],
        compiler_params=_cp(("arbitrary",)),
        name="rw_scan",
    )(*tabs, u, u, u, kk, lw, a, u, u, u, kk, lw, a, k_a, s0)


def _scan_tables(rows):
    rb = [[], []]
    first, seq = [], []
    base = 0
    sidx = 0
    for b, l in ((rows.b_ctx, rows.l_ctx), (rows.b_lat, rows.l_lat)):
        nc = l // CHUNK
        for i in range(b):
            for c in range(nc):
                rb[0].append(base + c)
                rb[1].append(base + nc - 1 - c)
                first.append((1 if c == 0 else 0) | (2 if c == nc - 1 else 0))
                seq.append(sidx)
            base += nc
            sidx += 1
    return (jnp.asarray(np.array(rb, np.int32)), jnp.asarray(np.array(first, np.int32)),
            jnp.asarray(np.array(seq, np.int32))), len(first)


def _rw_post_body(y0_ref, y1_ref, r_ref, k_ref, v_ref, g_ref, rk_ref, gw_ref, gb_ref, hm_ref, hs_ref, o_ref):
    y = y0_ref[...] + y1_ref[...]
    mu = _dot2(y, hm_ref[...])
    yc = y - mu
    var = _dot2(yc * yc, hm_ref[...])
    yn = yc * lax.rsqrt(var + RWKV_GN_EPS) * gw_ref[...] + gb_ref[...]
    bonus = _dot2(r_ref[...] * k_ref[...] * rk_ref[...], hs_ref[...]) * v_ref[...]
    o_ref[...] = (yn + bonus) * g_ref[...]


def _rw_post(y0, y1, u, g, r_k, gn_w, gn_b, hmean, hsum, tm=256):
    t = u.shape[0]
    c = RWKV_WIDTH
    cb = COL_RKV // c
    return pl.pallas_call(
        _rw_post_body,
        grid=(t // tm,),
        in_specs=[pl.BlockSpec((tm, c), lambda i: (i, 0)),
                  pl.BlockSpec((tm, c), lambda i: (i, 0)),
                  pl.BlockSpec((tm, c), lambda i: (i, cb)),
                  pl.BlockSpec((tm, c), lambda i: (i, cb + 1)),
                  pl.BlockSpec((tm, c), lambda i: (i, cb + 2)),
                  pl.BlockSpec((tm, c), lambda i: (i, 0)),
                  _full(r_k), _full(gn_w), _full(gn_b), _full(hmean), _full(hsum)],
        out_specs=pl.BlockSpec((tm, c), lambda i: (i, 0)),
        out_shape=jax.ShapeDtypeStruct((t, c), f32),
        compiler_params=_cp(("arbitrary",)),
        name="rw_post",
    )(y0, y1, u, u, u, g, r_k, gn_w, gn_b, hmean, hsum)


def _outproj_body(*refs, nx, nctx):
    att_refs, hy_refs = refs[0:2], refs[2:4]
    rw_ref, w_ref = refs[4:6]
    x_refs = refs[6:6 + nx]
    g_ref, nw_ref, o_ref = refs[6 + nx:]
    o = (_dot(_pair_load(att_refs, nctx).astype(bf16), w_ref[0:ATT_WIDTH, :])
         + _dot(_pair_load(hy_refs, nctx).astype(bf16), w_ref[ATT_WIDTH:ATT_WIDTH + HY_CH, :])
         + _dot(rw_ref[...].astype(bf16), w_ref[ATT_WIDTH + HY_CH:, :]))
    y = o * lax.rsqrt(jnp.mean(o * o, axis=-1, keepdims=True) + RMS_EPS) * nw_ref[...]
    o_ref[...] = _pair_load(x_refs, nctx) + g_ref[...] * y


def _out_proj(atts, hys, rw, w_out, layer, xs, mod, nw, rows, tm=256):
    t = rows.t
    _, mix, d = w_out.shape
    midx = rows.mod_index(tm)
    body = functools.partial(_outproj_body, nx=len(xs), nctx=rows.t_ctx // tm)
    return pl.pallas_call(
        body,
        grid=(t // tm,),
        in_specs=_pair_specs(atts, rows, tm) + _pair_specs(hys, rows, tm) + [
            pl.BlockSpec((tm, RWKV_WIDTH), lambda i: (i, 0)),
            pl.BlockSpec((None, mix, d), lambda i: (layer, 0, 0))] + _pair_specs(xs, rows, tm) + [
            pl.BlockSpec((None, 1, d), lambda i: (midx(i), 0, 2)),
            pl.BlockSpec((None, 1, d), lambda i: (1, 0, 0))],
        out_specs=pl.BlockSpec((tm, d), lambda i: (i, 0)),
        out_shape=jax.ShapeDtypeStruct((t, d), f32),
        compiler_params=_cp(("arbitrary",)),
        name="out_proj",
    )(*atts, *hys, rw, w_out, *xs, mod, nw)


def _mlp_body(x_ref, sh_ref, sc_ref, g_ref, nw2_ref, nw3_ref, w1_ref, w2_ref, *rest, nctx):
    o_refs, (h_scr, acc_scr) = rest[:-2], rest[-2:]
    i = pl.program_id(0)
    j = pl.program_id(1)

    @pl.when(j == 0)
    def _():
        h_scr[...] = _normmod(x_ref[...], nw2_ref[...], sc_ref[...], sh_ref[...]).astype(bf16)
        acc_scr[...] = jnp.zeros_like(acc_scr)

    a = jnp.maximum(_dot(h_scr[...], w1_ref[...]), 0.0)
    acc_scr[...] += _dot((a * a).astype(bf16), w2_ref[...])

    def result():
        f = acc_scr[...]
        y = f * lax.rsqrt(jnp.mean(f * f, axis=-1, keepdims=True) + RMS_EPS) * nw3_ref[...]
        return x_ref[...] + g_ref[...] * y

    last = j == pl.num_programs(1) - 1
    if len(o_refs) == 1:
        @pl.when(last)
        def _():
            o_refs[0][...] = result()
    else:
        @pl.when(last & (i < nctx))
        def _():
            o_refs[0][...] = result()

        @pl.when(last & (i >= nctx))
        def _():
            o_refs[1][...] = result()


def _mlp(x, mod, nw, w1, w2, layer, rows, split_out, tm=512, tf=1024):
    t, d = x.shape
    dff = w1.shape[2]
    midx = rows.mod_index(tm)
    nctx = rows.t_ctx // tm
    if split_out:
        out_specs = [pl.BlockSpec((tm, d), lambda i, j: (jnp.minimum(i, nctx - 1), 0)),
                     pl.BlockSpec((tm, d), lambda i, j: (jnp.maximum(i - nctx, 0), 0))]
        out_shape = [jax.ShapeDtypeStruct((rows.t_ctx, d), f32), jax.ShapeDtypeStruct((rows.t_lat, d), f32)]
    else:
        out_specs = pl.BlockSpec((tm, d), lambda i, j: (i, 0))
        out_shape = jax.ShapeDtypeStruct((t, d), f32)
    return pl.pallas_call(
        functools.partial(_mlp_body, nctx=nctx),
        grid=(t // tm, dff // tf),
        in_specs=[pl.BlockSpec((tm, d), lambda i, j: (i, 0)),
                  pl.BlockSpec((None, 1, d), lambda i, j: (midx(i), 0, 3)),
                  pl.BlockSpec((None, 1, d), lambda i, j: (midx(i), 0, 4)),
                  pl.BlockSpec((None, 1, d), lambda i, j: (midx(i), 0, 5)),
                  pl.BlockSpec((None, 1, d), lambda i, j: (2, 0, 0)),
                  pl.BlockSpec((None, 1, d), lambda i, j: (3, 0, 0)),
                  pl.BlockSpec((None, d, tf), lambda i, j: (layer, 0, j)),
                  pl.BlockSpec((None, tf, d), lambda i, j: (layer, j, 0))],
        out_specs=out_specs,
        out_shape=out_shape,
        scratch_shapes=[pltpu.VMEM((tm, d), bf16), pltpu.VMEM((tm, d), f32)],
        compiler_params=_cp(("arbitrary", "arbitrary")),
        name="mlp",
    )(x, mod, mod, mod, nw, nw, w1, w2)


def _lora_weight(w_up, a_up, g_up):
    c = RWKV_WIDTH
    w = jnp.zeros((LORA_IN, 5 * c), f32)
    r = 0
    for j, blk in enumerate((w_up[0], w_up[1], a_up[0], a_up[1], g_up)):
        w = w.at[r:r + blk.shape[0], j * c:(j + 1) * c].set(blk)
        r += blk.shape[0]
    return w.astype(bf16)


def _regroup_in_cols(w):
    kv0 = ATT_WIDTH
    hy0 = kv0 + 2 * ATT_KV
    rw0 = hy0 + 3 * HY_CH
    lo0 = rw0 + 3 * RWKV_WIDTH
    return jnp.concatenate([w[..., :kv0], w[..., rw0:lo0], w[..., hy0:rw0], w[..., kv0:hy0], w[..., lo0:]], axis=-1)


def _conv_taps(hy_short_w, rw_short_w):
    def ident(n):
        return jnp.zeros((3, n), f32).at[1].set(1.0)
    return jnp.concatenate([ident(ATT_WIDTH), rw_short_w, hy_short_w, ident(2 * ATT_KV + LORA_IN)], axis=1)


def kernel(x_prompt, x_sample, cache_k, cache_v, state_rwkv, c, c_ctx, ada_w, ada_b, norm_w, w_in, w_out,
           attn_sink, hy_short_w, hy_f1, hy_b1, hy_f2, hy_b2, hy_f3, hy_decay, hy_skip, rw_short_w, rw_w0,
           rw_w_up, rw_a0, rw_a_up, rw_g_up, rw_k_k, rw_k_a, rw_r_k, rw_gn_w, rw_gn_b, mlp_w1, mlp_w2):
    b_ctx, l_ctx, d = x_prompt.shape
    b_lat, l_lat, _ = x_sample.shape
    depth = ada_w.shape[0]
    grid_w = 64
    rows = _Rows(b_ctx, l_ctx, b_lat, l_lat)

    xs = (x_prompt.reshape(rows.t_ctx, d), x_sample.reshape(rows.t_lat, d))
    cond16 = jnp.zeros((16, d), f32).at[0].set(c_ctx).at[1:1 + b_lat].set(c)
    mod_all = _modulation(cond16, ada_w, ada_b[:, None, :])

    w_in_b = _regroup_in_cols(w_in).astype(bf16)
    w_out_b = w_out.astype(bf16)
    w1_b = mlp_w1.astype(bf16)
    w2_b = mlp_w2.astype(bf16)

    cos_t, sin_t = _rope_tables(l_lat, grid_w)
    hsum = _head_sum_matrix()
    hmean = _head_sum_matrix(1.0 / HEAD_DIM)
    scan_tabs, n_steps = _scan_tables(rows)
    dft = {}
    for l in (l_ctx, l_lat):
        cm, sm = _dft_tables(l)
        feat, t01 = _hy_feat(l)
        dft[l] = dict(cm=cm, sm=sm, cmb=cm.astype(bf16), smb=sm.astype(bf16),
                      cmtb=cm.T.astype(bf16), smtb=sm.T.astype(bf16), feat=feat, t01=t01)

    new_k, new_v, new_s = [], [], []
    for l in range(depth):
        mod = mod_all[l].reshape(16, 1, N_MOD * d)
        nw = norm_w[l].reshape(4, 1, d)
        u = _in_proj(xs, mod, nw, w_in_b, l, _conv_taps(hy_short_w[l], rw_short_w[l]), rows)

        new_k.append(u[:rows.t_ctx, COL_KV:COL_KV + ATT_KV].reshape(b_ctx, l_ctx, N_KV_HEADS, HEAD_DIM))
        new_v.append(u[:rows.t_ctx, COL_KV + ATT_KV:COL_KV + 2 * ATT_KV].reshape(b_ctx, l_ctx, N_KV_HEADS, HEAD_DIM))

        att_c = _attn_ctx(u, attn_sink[l], rows)
        att_l = _attn_lat(u, cache_k[:, l].reshape(b_lat, -1, ATT_KV), cache_v[:, l].reshape(b_lat, -1, ATT_KV),
                          attn_sink[l], cos_t, sin_t, rows)

        f1p = jnp.pad(hy_f1[l], ((0, 128 - HY_EMB), (0, 0)))
        hys = []
        for (bb, ll, rb0) in ((b_ctx, l_ctx, 0), (b_lat, l_lat, rows.t_ctx // l_lat)):
            tb = dft[ll]
            hr, hi = _hy_filter(ll, tb['feat'], tb['t01'], f1p, hy_b1[l][None], hy_f2[l], hy_b2[l][None],
                                hy_f3[l], hy_decay[l][None], tb['cm'], tb['sm'])
            hys.append(_hy_conv(u, hr, hi, hy_skip[l], tb['cmb'], tb['smb'], tb['cmtb'], tb['smtb'],
                                bb, ll, rb0))

        w_lora = _lora_weight(rw_w_up[l], rw_a_up[l], rw_g_up[l])
        kk, lw, a, g = _rw_prep(u, w_lora, rw_w0[l], rw_a0[l], rw_k_k[l][None], hsum)
        s0 = jnp.concatenate([jnp.zeros((b_ctx, 2, N_RWKV_HEADS, HEAD_DIM, HEAD_DIM), f32),
                              state_rwkv[:, l]], axis=0)
        y0, y1, sfin = _rw_scan(u, kk, lw, a, rw_k_a[l][None], s0, scan_tabs, n_steps)
        new_s.append(sfin[:b_ctx])
        rw = _rw_post(y0, y1, u, g, rw_r_k[l].reshape(1, RWKV_WIDTH), rw_gn_w[l][None], rw_gn_b[l][None],
                      hmean, hsum)

        x = _out_proj((att_c, att_l), tuple(hys), rw, w_out_b, l, xs, mod, nw, rows)
        x = _mlp(x, mod, nw, w1_b, w2_b, l, rows, split_out=(l == depth - 1))
        xs = (x,)

    y_p, y_s = x
    return (y_p.reshape(b_ctx, l_ctx, d), y_s.reshape(b_lat, l_lat, d),
            jnp.stack(new_k, axis=1), jnp.stack(new_v, axis=1), jnp.stack(new_s, axis=1))
```

```python
import functools
import math

import numpy as np
import jax
import jax.numpy as jnp
from jax import lax
from jax.experimental import pallas as pl
from jax.experimental.pallas import tpu as pltpu

f32 = jnp.float32
bf16 = jnp.bfloat16

HEAD_DIM = 64
N_ATT_HEADS = 12
N_KV_HEADS = 4
ATT_GROUP = 3
ATT_WIDTH = 768
ATT_KV = 256
WINDOW = 128
ROPE_THETA = 10000.0
NEG_INF = -1e30
HY_CH = 512
HY_BANDS = 16
HY_EMB = 33
HY_FFN = 64
HY_MOD_SHIFT = 0.05
RWKV_WIDTH = 768
N_RWKV_HEADS = 12
DECAY_LORA = 96
AAA_LORA = 96
GATE_LORA = 256
RWKV_GN_EPS = 64e-5
N_MOD = 6
RMS_EPS = 1e-6
CHUNK = 64
SCAN_SEQS = 2
CONV_ROWS = 256
LORA_IN = 2 * DECAY_LORA + 2 * AAA_LORA + GATE_LORA
COL_Q = 0
COL_RKV = COL_Q + ATT_WIDTH
COL_HY = COL_RKV + 3 * RWKV_WIDTH
COL_KV = COL_HY + 3 * HY_CH
COL_LORA = COL_KV + 2 * ATT_KV
IN_COLS = COL_LORA + LORA_IN
HI = lax.Precision.HIGHEST
VMEM_LIMIT = 56 * 1024 * 1024


def _cp(sem, vmem=VMEM_LIMIT):
    return pltpu.CompilerParams(dimension_semantics=sem, vmem_limit_bytes=vmem)


def _dot(a, b):
    return jnp.dot(a, b, preferred_element_type=f32)


def _dot_nt(a, b):
    return lax.dot_general(a, b, (((1,), (1,)), ((), ())), preferred_element_type=f32)


def _dot_tn(a, b):
    return lax.dot_general(a, b, (((0,), (0,)), ((), ())), preferred_element_type=f32)


def _dot_hi(a, b):
    return jnp.dot(a, b, preferred_element_type=f32, precision=HI)


def _full(a):
    return pl.BlockSpec(a.shape, lambda *_: (0,) * a.ndim)


def _mod_body(c_ref, w_ref, b_ref, o_ref):
    c = c_ref[...]
    s = (c * jax.nn.sigmoid(c)).astype(bf16)
    o_ref[...] = _dot(s, w_ref[...].astype(bf16)) + b_ref[...]


def _modulation(cond16, ada_w, ada_b):
    depth, d, n = ada_w.shape
    tn = 1024
    return pl.pallas_call(
        _mod_body,
        grid=(depth, n // tn),
        in_specs=[pl.BlockSpec((16, d), lambda l, j: (0, 0)),
                  pl.BlockSpec((None, d, tn), lambda l, j: (l, 0, j)),
                  pl.BlockSpec((None, 1, tn), lambda l, j: (l, 0, j))],
        out_specs=pl.BlockSpec((None, 16, tn), lambda l, j: (l, 0, j)),
        out_shape=jax.ShapeDtypeStruct((depth, 16, n), f32),
        compiler_params=_cp(("arbitrary", "arbitrary")),
        name="modulation",
    )(cond16, ada_w, ada_b)


def _normmod(x, nw, sc, sh):
    y = x * lax.rsqrt(jnp.mean(x * x, axis=-1, keepdims=True) + RMS_EPS) * nw
    return y * (1.0 + sc) + sh


class _Rows:
    def __init__(self, b_ctx, l_ctx, b_lat, l_lat):
        self.b_ctx, self.l_ctx, self.b_lat, self.l_lat = b_ctx, l_ctx, b_lat, l_lat
        self.t_ctx = b_ctx * l_ctx
        self.t_lat = b_lat * l_lat
        self.t = self.t_ctx + self.t_lat

    def mod_index(self, tm):
        nctx = self.t_ctx // tm
        per = self.l_lat // tm

        def f(i):
            return jnp.where(i < nctx, 0, 1 + (i - nctx) // per)
        return f


def _pair_specs(arrs, rows, tm, single_buffer=False):
    w = arrs[0].shape[1]
    if len(arrs) == 1:
        return [pl.BlockSpec((tm, w), lambda i, *_: (i, 0))]
    nctx = rows.t_ctx // tm
    mode = dict(pipeline_mode=pl.Buffered(1)) if single_buffer else {}
    return [pl.BlockSpec((tm, w), lambda i, *_: (jnp.minimum(i, nctx - 1), 0), **mode),
            pl.BlockSpec((tm, w), lambda i, *_: (jnp.maximum(i - nctx, 0), 0), **mode)]


def _pair_load(refs, nctx):
    if len(refs) == 1:
        return refs[0][...]
    return jnp.where(pl.program_id(0) < nctx, refs[0][...], refs[1][...])


def _inproj_body(*refs, nx, nctx, lmask_ctx, lmask_lat):
    x_refs = refs[:nx]
    sh_ref, sc_ref, nw_ref, w_ref, wc_ref, o_ref, h_scr, u_scr = refs[nx:]
    i = pl.program_id(0)

    @pl.when(pl.program_id(1) == 0)
    def _():
        h_scr[...] = _normmod(_pair_load(x_refs, nctx), nw_ref[...], sc_ref[...], sh_ref[...]).astype(bf16)

    u_scr[...] = _dot(h_scr[...], w_ref[...])
    tm = u_scr.shape[0]
    ch = CONV_ROWS
    lmask = jnp.where(i < nctx, lmask_ctx, lmask_lat)
    row = lax.broadcasted_iota(jnp.int32, (ch, 1), 0)
    wc = wc_ref[...]
    for r0 in range(0, tm, ch):
        x = u_scr[r0:r0 + ch, :]
        at_start = (r0 & lmask) == 0
        at_end = ((r0 + ch - 1) & lmask) == lmask
        prev = jnp.where(at_start, 0.0, u_scr[max(r0 - 1, 0):max(r0 - 1, 0) + 1, :])
        nxt = jnp.where(at_end, 0.0, u_scr[min(r0 + ch, tm - 1):min(r0 + ch, tm - 1) + 1, :])
        xm = jnp.where(row == 0, prev, pltpu.roll(x, 1, 0))
        xp = jnp.where(row == ch - 1, nxt, pltpu.roll(x, ch - 1, 0))
        o_ref[r0:r0 + ch, :] = xm * wc[0:1, :] + x * wc[1:2, :] + xp * wc[2:3, :]


def _in_proj(xs, mod, nw, w_in, layer, wconv, rows, tm=1024, tn=1152):
    t = rows.t
    _, d, n = w_in.shape
    for l in (rows.l_ctx, rows.l_lat):
        assert l & (l - 1) == 0 and tm % l == 0 and l % CONV_ROWS == 0
    midx = rows.mod_index(tm)
    body = functools.partial(_inproj_body, nx=len(xs), nctx=rows.t_ctx // tm,
                             lmask_ctx=rows.l_ctx - 1, lmask_lat=rows.l_lat - 1)
    return pl.pallas_call(
        body,
        grid=(t // tm, n // tn),
        in_specs=_pair_specs(xs, rows, tm, single_buffer=True) + [
            pl.BlockSpec((None, 1, d), lambda i, j: (midx(i), 0, 0)),
            pl.BlockSpec((None, 1, d), lambda i, j: (midx(i), 0, 1)),
            pl.BlockSpec((None, 1, d), lambda i, j: (0, 0, 0)),
            pl.BlockSpec((None, d, tn), lambda i, j: (layer, 0, j)),
            pl.BlockSpec((3, tn), lambda i, j: (0, j))],
        out_specs=pl.BlockSpec((tm, tn), lambda i, j: (i, j)),
        out_shape=jax.ShapeDtypeStruct((t, n), f32),
        scratch_shapes=[pltpu.VMEM((tm, d), bf16), pltpu.VMEM((tm, tn), f32)],
        compiler_params=_cp(("arbitrary", "arbitrary")),
        name="in_proj",
    )(*xs, mod, mod, nw, w_in, wconv)


def _softmax_pv(s_list, v_list, sink_col):
    m = sink_col
    for s in s_list:
        m = jnp.maximum(m, jnp.max(s, axis=-1, keepdims=True))
    den = jnp.exp(sink_col - m)
    acc = None
    for s, v in zip(s_list, v_list):
        p = jnp.exp(s - m)
        den = den + jnp.sum(p, axis=-1, keepdims=True)
        pv = _dot(p.astype(bf16), v)
        acc = pv if acc is None else acc + pv
    return acc / den


def _attn_ctx_body(sink_ref, q_ref, k_ref, v_ref, o_ref):
    l = q_ref.shape[0]
    scale = HEAD_DIM ** -0.5
    for n in range(N_KV_HEADS):
        kn = k_ref[:, n * HEAD_DIM:(n + 1) * HEAD_DIM].astype(bf16)
        vn = v_ref[:, n * HEAD_DIM:(n + 1) * HEAD_DIM].astype(bf16)
        for g in range(ATT_GROUP):
            h = n * ATT_GROUP + g
            q = (q_ref[:, h * HEAD_DIM:(h + 1) * HEAD_DIM] * scale).astype(bf16)
            s = _dot_nt(q, kn)
            sink_col = jnp.full((l, 1), sink_ref[h], f32)
            o_ref[:, h * HEAD_DIM:(h + 1) * HEAD_DIM] = _softmax_pv([s], [vn], sink_col)


def _attn_ctx(u, sink, rows):
    b, l = rows.b_ctx, rows.l_ctx
    return pl.pallas_call(
        _attn_ctx_body,
        grid=(b,),
        in_specs=[pl.BlockSpec(memory_space=pltpu.SMEM),
                  pl.BlockSpec((l, ATT_WIDTH), lambda i: (i, COL_Q // ATT_WIDTH)),
                  pl.BlockSpec((l, ATT_KV), lambda i: (i, COL_KV // ATT_KV)),
                  pl.BlockSpec((l, ATT_KV), lambda i: (i, COL_KV // ATT_KV + 1))],
        out_specs=pl.BlockSpec((l, ATT_WIDTH), lambda i: (i, 0)),
        out_shape=jax.ShapeDtypeStruct((b * l, ATT_WIDTH), f32),
        compiler_params=_cp(("arbitrary",)),
        name="attn_ctx",
    )(sink, u, u, u)


def _rope(x, cos, sin_signed):
    w = x.shape[1]
    lane = lax.broadcasted_iota(jnp.int32, (1, w), 1)
    swapped = jnp.where((lane % 32) < 16, pltpu.roll(x, w - 16, 1), pltpu.roll(x, 16, 1))
    return x * cos + swapped * sin_signed


def _attn_lat_body(sink_ref, q_ref, k_ref, v_ref, kc_ref, vc_ref, cos_ref, sin_ref, o_ref,
                   q_scr, k_scr, v_scr, kc_scr, vc_scr):
    l = q_ref.shape[0]
    blk = WINDOW
    nb = l // blk
    scale = HEAD_DIM ** -0.5
    cos = cos_ref[...]
    sin = sin_ref[...]
    q_scr[...] = (_rope(q_ref[...], cos, sin) * scale).astype(bf16)
    zpad = jnp.zeros((blk, ATT_KV), bf16)
    k_scr[0:blk, :] = zpad
    k_scr[blk + l:, :] = zpad
    v_scr[0:blk, :] = zpad
    v_scr[blk + l:, :] = zpad
    k_scr[blk:blk + l, :] = _rope(k_ref[...], cos[:, :ATT_KV], sin[:, :ATT_KV]).astype(bf16)
    v_scr[blk:blk + l, :] = v_ref[...].astype(bf16)
    r = lax.broadcasted_iota(jnp.int32, (ATT_GROUP * blk, 3 * blk), 0) % blk
    c = lax.broadcasted_iota(jnp.int32, (ATT_GROUP * blk, 3 * blk), 1)
    band = (c - r >= 0) & (c - r <= 2 * WINDOW)
    kvh = range(N_KV_HEADS)
    hcols = [slice(n * HEAD_DIM, (n + 1) * HEAD_DIM) for n in kvh]
    kc_scr[...] = kc_ref[...].astype(bf16)
    vc_scr[...] = vc_ref[...].astype(bf16)
    sink_col = [jnp.concatenate([jnp.full((blk, 1), sink_ref[n * ATT_GROUP + g], f32) for g in range(ATT_GROUP)],
                                axis=0) for n in kvh]
    for i in range(nb):
        kpos = (i - 1) * blk + c
        mask = band & (kpos >= 0) & (kpos < l)
        rows_q = slice(i * blk, (i + 1) * blk)
        rows_k = slice(i * blk, (i + 3) * blk)
        qs = [jnp.concatenate([q_scr[rows_q, (n * ATT_GROUP + g) * HEAD_DIM:(n * ATT_GROUP + g + 1) * HEAD_DIM]
                               for g in range(ATT_GROUP)], axis=0) for n in kvh]
        s_loc = [jnp.where(mask, _dot_nt(qs[n], k_scr[rows_k, hcols[n]]), NEG_INF) for n in kvh]
        s_ctx = [_dot_nt(qs[n], kc_scr[:, hcols[n]]) for n in kvh]
        m = [jnp.maximum(jnp.maximum(jnp.max(s_loc[n], axis=-1, keepdims=True),
                                     jnp.max(s_ctx[n], axis=-1, keepdims=True)), sink_col[n]) for n in kvh]
        p_loc = [jnp.exp(s_loc[n] - m[n]) for n in kvh]
        p_ctx = [jnp.exp(s_ctx[n] - m[n]) for n in kvh]
        den = [jnp.exp(sink_col[n] - m[n]) + jnp.sum(p_loc[n], axis=-1, keepdims=True)
               + jnp.sum(p_ctx[n], axis=-1, keepdims=True) for n in kvh]
        o = [(_dot(p_loc[n].astype(bf16), v_scr[rows_k, hcols[n]])
              + _dot(p_ctx[n].astype(bf16), vc_scr[:, hcols[n]])) / den[n] for n in kvh]
        for n in kvh:
            for g in range(ATT_GROUP):
                h = n * ATT_GROUP + g
                o_ref[rows_q, h * HEAD_DIM:(h + 1) * HEAD_DIM] = o[n][g * blk:(g + 1) * blk, :]


def _attn_lat(u, kc, vc, sink, cos, sin, rows):
    b, l = rows.b_lat, rows.l_lat
    assert rows.t_ctx % l == 0
    rb0 = rows.t_ctx // l
    lc = kc.shape[1]
    return pl.pallas_call(
        _attn_lat_body,
        grid=(b,),
        in_specs=[pl.BlockSpec(memory_space=pltpu.SMEM),
                  pl.BlockSpec((l, ATT_WIDTH), lambda i: (rb0 + i, COL_Q // ATT_WIDTH)),
                  pl.BlockSpec((l, ATT_KV), lambda i: (rb0 + i, COL_KV // ATT_KV)),
                  pl.BlockSpec((l, ATT_KV), lambda i: (rb0 + i, COL_KV // ATT_KV + 1)),
                  pl.BlockSpec((None, lc, ATT_KV), lambda i: (i, 0, 0)),
                  pl.BlockSpec((None, lc, ATT_KV), lambda i: (i, 0, 0)),
                  pl.BlockSpec((l, ATT_WIDTH), lambda i: (0, 0)),
                  pl.BlockSpec((l, ATT_WIDTH), lambda i: (0, 0))],
        out_specs=pl.BlockSpec((l, ATT_WIDTH), lambda i: (i, 0)),
        out_shape=jax.ShapeDtypeStruct((b * l, ATT_WIDTH), f32),
        scratch_shapes=[pltpu.VMEM((l, ATT_WIDTH), bf16),
                        pltpu.VMEM((l + 2 * WINDOW, ATT_KV), bf16),
                        pltpu.VMEM((l + 2 * WINDOW, ATT_KV), bf16),
                        pltpu.VMEM((lc, ATT_KV), bf16),
                        pltpu.VMEM((lc, ATT_KV), bf16)],
        compiler_params=_cp(("arbitrary",)),
        name="attn_lat",
    )(sink, u, u, u, kc, vc, cos, sin)


def _rope_tables(l, grid_w):
    half = HEAD_DIM // 2
    pos = jnp.arange(l)
    rowp = (pos // grid_w).astype(f32)
    colp = (pos % grid_w).astype(f32)
    freqs = ROPE_THETA ** (-jnp.arange(0, half, 2, dtype=f32) / half)
    ar = rowp[:, None] * freqs[None, :]
    ac = colp[:, None] * freqs[None, :]
    cos = jnp.concatenate([jnp.cos(ar), jnp.cos(ar), jnp.cos(ac), jnp.cos(ac)], axis=-1)
    sin = jnp.concatenate([-jnp.sin(ar), jnp.sin(ar), -jnp.sin(ac), jnp.sin(ac)], axis=-1)
    return jnp.tile(cos, (1, N_ATT_HEADS)), jnp.tile(sin, (1, N_ATT_HEADS))


def _dft_tables(l):
    f = 32
    k = jnp.arange(l, dtype=jnp.int32)[:, None]

    def narrow(svals):
        m = ((2 * k + 1) * svals[None, :]) % (4 * l)
        ang = m.astype(f32) * (math.pi / (2 * l))
        return jnp.cos(ang), jnp.sin(ang)

    c1, s1 = narrow(jnp.arange(l // f, dtype=jnp.int32) * f)
    c0, s0 = narrow(jnp.arange(f, dtype=jnp.int32))
    cm = (c1[:, :, None] * c0[:, None, :] - s1[:, :, None] * s0[:, None, :]).reshape(l, l)
    sm = (s1[:, :, None] * c0[:, None, :] + c1[:, :, None] * s0[:, None, :]).reshape(l, l)
    c1t, s1t, c0t, s0t = c1.T, s1.T, c0.T, s0.T
    cmt = (c1t[:, None, :] * c0t[None, :, :] - s1t[:, None, :] * s0t[None, :, :]).reshape(l, l)
    smt = (s1t[:, None, :] * c0t[None, :, :] + c1t[:, None, :] * s0t[None, :, :]).reshape(l, l)
    return cm, sm, cmt, smt


def _hy_feat(l):
    t = jnp.arange(l, dtype=f32)
    t01 = (t / max(l - 1, 1))[:, None]
    bands = jnp.linspace(1e-4, HY_BANDS - 1, HY_BANDS, dtype=f32)
    ang = (2.0 * math.pi / l) * t[:, None] * bands[None, :]
    feat = jnp.concatenate([t01, jnp.cos(ang), -jnp.sin(ang)], axis=-1)
    return jnp.pad(feat, ((0, 0), (0, 128 - HY_EMB))), t01


def _hy_filter_body(feat_ref, t01_ref, f1_ref, b1_ref, f2_ref, b2_ref, f3_ref, dec_ref, cm_ref, sm_ref,
                    hr_ref, hi_ref):
    h = jnp.sin(_dot_hi(feat_ref[...], f1_ref[...]) + b1_ref[...])
    h = jnp.sin(_dot_hi(h, f2_ref[...]) + b2_ref[...])
    h = _dot_hi(h, f3_ref[...])
    h = h * (jnp.exp(-t01_ref[...] * jnp.abs(dec_ref[...])) + HY_MOD_SHIFT)
    l = h.shape[0]
    fwd = h[:, :HY_CH]
    row = lax.broadcasted_iota(jnp.int32, (l, 1), 0)
    bwd = jnp.where(row == 0, 0.0, h[:, HY_CH:])
    hr_ref[...] = _dot_hi(cm_ref[...], fwd + bwd)
    hi_ref[...] = _dot_hi(sm_ref[...], bwd - fwd)


def _hy_filter(l, feat, t01, f1p, b1, f2, b2, f3, dec, cm, sm):
    args = (feat, t01, f1p, b1, f2, b2, f3, dec, cm, sm)
    return pl.pallas_call(
        _hy_filter_body,
        grid=(1,),
        in_specs=[_full(a) for a in args],
        out_specs=[pl.BlockSpec((l, HY_CH), lambda i: (0, 0))] * 2,
        out_shape=[jax.ShapeDtypeStruct((l, HY_CH), f32)] * 2,
        compiler_params=_cp(("arbitrary",)),
        name="hy_filter",
    )(*args)


def _hy_conv_body(x0_ref, x1_ref, v_ref, hr_ref, hi_ref, skip_ref, cm_ref, sm_ref, cmt_ref, smt_ref, o_ref):
    l = v_ref.shape[0]
    z = v_ref[...] * x1_ref[...]
    zb = z.astype(bf16)
    zr = _dot(cm_ref[...], zb)
    zs = _dot(sm_ref[...], zb)
    hr = hr_ref[...]
    hi = hi_ref[...]
    yr = (zr * hr + zs * hi).astype(bf16)
    yi = (zr * hi - zs * hr).astype(bf16)
    y = (_dot(cmt_ref[...], yr) - _dot(smt_ref[...], yi)) * (1.0 / l)
    o_ref[...] = (y + skip_ref[...] * z) * x0_ref[...]


def _hy_conv(u, hr, hi, skip, cm, sm, cmt, smt, b, l, rb0):
    cb = COL_HY // HY_CH
    return pl.pallas_call(
        _hy_conv_body,
        grid=(b,),
        in_specs=[pl.BlockSpec((l, HY_CH), lambda i: (rb0 + i, cb)),
                  pl.BlockSpec((l, HY_CH), lambda i: (rb0 + i, cb + 1)),
                  pl.BlockSpec((l, HY_CH), lambda i: (rb0 + i, cb + 2)),
                  _full(hr), _full(hi), _full(skip), _full(cm), _full(sm), _full(cmt), _full(smt)],
        out_specs=pl.BlockSpec((l, HY_CH), lambda i: (i, 0)),
        out_shape=jax.ShapeDtypeStruct((b * l, HY_CH), f32),
        compiler_params=_cp(("arbitrary",)),
        name="hy_conv",
    )(u, u, u, hr, hi, skip, cm, sm, cmt, smt)


def _head_sum_matrix(scale=1.0):
    i = np.arange(RWKV_WIDTH) // HEAD_DIM
    return jnp.asarray((i[:, None] == i[None, :]).astype(np.float32) * scale).astype(bf16)


def _dot2(x, m):
    hi = x.astype(bf16)
    lo = (x - hi.astype(f32)).astype(bf16)
    return _dot(hi, m) + _dot(lo, m)


def _rw_prep_body(lo_ref, k_ref, wl_ref, w0_ref, a0_ref, kk_ref_, hs_ref, kk_o, lw_o, a_o, g_o):
    x = lo_ref[...]
    lane = lax.broadcasted_iota(jnp.int32, x.shape, 1)
    act = jnp.where(lane < 2 * DECAY_LORA, jnp.tanh(x),
                    jnp.where(lane < 2 * DECAY_LORA + 2 * AAA_LORA, x, jax.nn.sigmoid(x)))
    lo = _dot(act.astype(bf16), wl_ref[...])
    c = RWKV_WIDTH
    for d in range(2):
        zneg = -(w0_ref[d:d + 1, :] + lo[:, d * c:(d + 1) * c])
        softplus = jnp.maximum(zneg, 0.0) + jnp.log(1.0 + jnp.exp(-jnp.abs(zneg)))
        w_log = -softplus - 0.5
        lw_o[d] = -jnp.exp(w_log)
        a_o[d] = jax.nn.sigmoid(a0_ref[d:d + 1, :] + lo[:, (2 + d) * c:(3 + d) * c])
    g_o[...] = lo[:, 4 * c:5 * c]
    kk = k_ref[...] * kk_ref_[...]
    ss = _dot2(kk * kk, hs_ref[...])
    kk_o[...] = kk / jnp.maximum(jnp.sqrt(ss), 1e-12)


def _rw_prep(u, w_lora, w0, a0, k_k, hsum, tm=256):
    t = u.shape[0]
    c = RWKV_WIDTH
    return pl.pallas_call(
        _rw_prep_body,
        grid=(t // tm,),
        in_specs=[pl.BlockSpec((tm, LORA_IN), lambda i: (i, COL_LORA // LORA_IN)),
                  pl.BlockSpec((tm, c), lambda i: (i, COL_RKV // c + 1)),
                  _full(w_lora), _full(w0), _full(a0), _full(k_k), _full(hsum)],
        out_specs=[pl.BlockSpec((tm, c), lambda i: (i, 0)),
                   pl.BlockSpec((2, tm, c), lambda i: (0, i, 0)),
                   pl.BlockSpec((2, tm, c), lambda i: (0, i, 0)),
                   pl.BlockSpec((tm, c), lambda i: (i, 0))],
        out_shape=[jax.ShapeDtypeStruct((t, c), f32),
                   jax.ShapeDtypeStruct((2, t, c), f32),
                   jax.ShapeDtypeStruct((2, t, c), f32),
                   jax.ShapeDtypeStruct((t, c), f32)],
        compiler_params=_cp(("arbitrary",)),
        name="rw_prep",
    )(u, u, w_lora, w0, a0, k_k, hsum)


def _split3(x):
    hi = x.astype(bf16)
    r1 = x - hi.astype(f32)
    mid = r1.astype(bf16)
    lo = (r1 - mid.astype(f32)).astype(bf16)
    return hi, mid, lo


def _rw_dir_operands(d, r_ref, k_ref, v_ref, kk_ref, lw_ref, a_ref, ka):
    cdim = CHUNK
    row = lax.broadcasted_iota(jnp.int32, (2 * cdim, 4 * cdim), 0)
    col = lax.broadcasted_iota(jnp.int32, (2 * cdim, 4 * cdim), 1) % cdim
    lag = (row % cdim) - col if d == 0 else col - (row % cdim)
    mask = lag >= jnp.where(row < cdim, 1, 0)
    row1 = lax.broadcasted_iota(jnp.int32, (cdim, cdim), 0)
    col1 = lax.broadcasted_iota(jnp.int32, (cdim, cdim), 1)
    incl = (col1 <= row1) if d == 0 else (col1 >= row1)
    lw = lw_ref[...]
    cs = sum(_dot(incl.astype(bf16), part) for part in _split3(lw))
    tot = jnp.sum(lw, axis=0, keepdims=True)
    a = a_ref[...]
    kk = kk_ref[...]
    beta = kk * a
    kdir = k_ref[...] * (1.0 + (a - 1.0) * ka)
    p_inv = jnp.exp(-cs)
    p_rest = jnp.exp(tot - cs)
    return dict(
        mask=mask, p_all=jnp.exp(tot),
        kap=(kk * jnp.exp(cs - lw)).astype(bf16), r=(r_ref[...] * jnp.exp(cs)).astype(bf16),
        beta=(beta * p_inv).astype(bf16), kdir=(kdir * p_inv).astype(bf16),
        beta_e=(beta * p_rest).astype(bf16), kdir_e=(kdir * p_rest).astype(bf16),
        v=v_ref[...].astype(bf16))


def _bd(x):
    lane = lax.broadcasted_iota(jnp.int32, x.shape, 1)
    zero = jnp.zeros_like(x)
    return jnp.concatenate([jnp.where(lane < HEAD_DIM, x, zero), jnp.where(lane >= HEAD_DIM, x, zero)], axis=0)


def _rw_scan_body(rb_ref, flag_ref, seq_ref, *refs):
    nj = 2 * SCAN_SEQS
    in_refs = [refs[6 * j:6 * j + 6] for j in range(nj)]
    ka_ref, s0_ref = refs[6 * nj:6 * nj + 2]
    y_refs = refs[6 * nj + 2:7 * nj + 2]
    sf_ref, s_scr = refs[7 * nj + 2:]
    s = pl.program_id(0)
    cdim = CHUNK
    pw = 2 * HEAD_DIM
    n_pairs = N_RWKV_HEADS // 2
    chains = [(j, p) for j in range(nj) for p in range(n_pairs)]
    n = range(len(chains))
    flag = flag_ref[s]

    @pl.when((flag & 1) == 1)
    def _():
        z = jnp.zeros((HEAD_DIM, HEAD_DIM), f32)
        for j, p in chains:
            s_scr[j, p] = jnp.concatenate(
                [jnp.concatenate([s0_ref[j // 2, j % 2, 2 * p], z], axis=1),
                 jnp.concatenate([z, s0_ref[j // 2, j % 2, 2 * p + 1]], axis=1)], axis=0)

    ka = ka_ref[...]
    ops = [_rw_dir_operands(j % 2, *in_refs[j], ka) for j in range(nj)]

    def sl(name, i):
        j, p = chains[i]
        return ops[j][name][:, p * pw:(p + 1) * pw]

    s_old = [s_scr[j, p] for j, p in chains]
    s_b = [x.astype(bf16) for x in s_old]
    lhs = [jnp.concatenate([sl('kap', i), sl('r', i)], axis=0) for i in n]
    rhs = [jnp.concatenate([_bd(sl('beta', i)), _bd(sl('kdir', i))], axis=0) for i in n]
    mm = [jnp.where(ops[chains[i][0]]['mask'], _dot_nt(lhs[i], rhs[i]), 0.0).astype(bf16) for i in n]
    sk = [_dot_nt(lhs[i], s_b[i]) for i in n]
    av = [_dot(mm[i][:, 2 * cdim:], _bd(sl('v', i))) for i in n]
    x = [-mm[i][:cdim, :2 * cdim] for i in n]
    m_rb = [mm[i][cdim:, :2 * cdim] for i in n]
    uu = [sk[i][:cdim] + av[i][:cdim] for i in n]
    n_sq = int(math.log2(cdim))
    for it in range(n_sq):
        uu = [uu[i] + _dot(x[i], _bd(uu[i].astype(bf16))) for i in n]
        if it + 1 < n_sq:
            x = [_dot(x[i], _bd(x[i])).astype(bf16) for i in n]
    u_b = [(-uu[i]).astype(bf16) for i in n]
    y = [sk[i][cdim:] + _dot(m_rb[i], _bd(u_b[i])) + av[i][cdim:] for i in n]
    for j in range(nj):
        y_refs[j][...] = jnp.concatenate(y[j * n_pairs:(j + 1) * n_pairs], axis=1)
    row = lax.broadcasted_iota(jnp.int32, (pw, pw), 0)
    col = lax.broadcasted_iota(jnp.int32, (pw, pw), 1)
    same_head = (row // HEAD_DIM) == (col // HEAD_DIM)
    s_new = []
    for i in n:
        j, p = chains[i]
        upd = _dot_tn(jnp.concatenate([u_b[i], sl('v', i)], axis=0),
                      jnp.concatenate([sl('beta_e', i), sl('kdir_e', i)], axis=0))
        s_new.append(s_old[i] * ops[j]['p_all'][:, p * pw:(p + 1) * pw] + jnp.where(same_head, upd, 0.0))
        s_scr[j, p] = s_new[i]

    @pl.when((flag & 2) == 2)
    def _():
        for i in n:
            j, p = chains[i]
            sf_ref[j // 2, j % 2, 2 * p] = s_new[i][:HEAD_DIM, :HEAD_DIM]
            sf_ref[j // 2, j % 2, 2 * p + 1] = s_new[i][HEAD_DIM:, HEAD_DIM:]


def _rw_scan(u, kk, lw, a, k_a, s0, tabs, n_steps):
    c = RWKV_WIDTH
    t = u.shape[0]
    nseq = s0.shape[0]
    cb = COL_RKV // c
    nj = 2 * SCAN_SEQS

    def stream_specs(j):
        d = j % 2
        return [pl.BlockSpec((CHUNK, c), lambda s, rb, fl, sq: (rb[j, s], cb)),
                pl.BlockSpec((CHUNK, c), lambda s, rb, fl, sq: (rb[j, s], cb + 1)),
                pl.BlockSpec((CHUNK, c), lambda s, rb, fl, sq: (rb[j, s], cb + 2)),
                pl.BlockSpec((CHUNK, c), lambda s, rb, fl, sq: (rb[j, s], 0)),
                pl.BlockSpec((None, CHUNK, c), lambda s, rb, fl, sq: (d, rb[j, s], 0)),
                pl.BlockSpec((None, CHUNK, c), lambda s, rb, fl, sq: (d, rb[j, s], 0))]

    state_spec = pl.BlockSpec((SCAN_SEQS, 2, N_RWKV_HEADS, HEAD_DIM, HEAD_DIM),
                              lambda s, rb, fl, sq: (sq[s], 0, 0, 0, 0))
    in_specs = []
    operands = []
    for j in range(nj):
        in_specs += stream_specs(j)
        operands += [u, u, u, kk, lw, a]
    grid_spec = pltpu.PrefetchScalarGridSpec(
        num_scalar_prefetch=3,
        grid=(n_steps,),
        in_specs=in_specs + [pl.BlockSpec((1, c), lambda s, rb, fl, sq: (0, 0)), state_spec],
        out_specs=[pl.BlockSpec((CHUNK, c), lambda s, rb, fl, sq, j=j: (rb[j, s], 0)) for j in range(nj)]
        + [state_spec],
        scratch_shapes=[pltpu.VMEM((nj, N_RWKV_HEADS // 2, 2 * HEAD_DIM, 2 * HEAD_DIM), f32)],
    )
    outs = pl.pallas_call(
        _rw_scan_body,
        grid_spec=grid_spec,
        out_shape=[jax.ShapeDtypeStruct((t, c), f32)] * nj
        + [jax.ShapeDtypeStruct((nseq, 2, N_RWKV_HEADS, HEAD_DIM, HEAD_DIM), f32)],
        compiler_params=_cp(("arbitrary",)),
        name="rw_scan",
    )(*tabs, *operands, k_a, s0)
    return outs[:nj], outs[nj]


def _scan_tables(rows):
    rb = [[] for _ in range(2 * SCAN_SEQS)]
    flags, group = [], []
    base = 0
    gidx = 0
    for b, l in ((rows.b_ctx, rows.l_ctx), (rows.b_lat, rows.l_lat)):
        nc = l // CHUNK
        assert b % SCAN_SEQS == 0
        for q in range(b // SCAN_SEQS):
            for c in range(nc):
                for m in range(SCAN_SEQS):
                    seq_base = base + (q * SCAN_SEQS + m) * nc
                    rb[2 * m].append(seq_base + c)
                    rb[2 * m + 1].append(seq_base + nc - 1 - c)
                flags.append((1 if c == 0 else 0) | (2 if c == nc - 1 else 0))
                group.append(gidx)
            gidx += 1
        base += b * nc
    return (jnp.asarray(np.array(rb, np.int32)), jnp.asarray(np.array(flags, np.int32)),
            jnp.asarray(np.array(group, np.int32))), len(flags)


def _member_of_block(i, rows, tm):
    nctx = rows.t_ctx // tm
    in_ctx = i < nctx
    per = jnp.where(in_ctx, rows.l_ctx // tm, rows.l_lat // tm)
    base = jnp.where(in_ctx, 0, nctx)
    seq = (i - base) // per
    start = base + seq * per
    return seq % SCAN_SEQS, jnp.where(start > 0, start - 1, per)


def _rw_post_body(*refs, rows, tm):
    y_refs = refs[:2 * SCAN_SEQS]
    r_ref, k_ref, v_ref, g_ref, rk_ref, gw_ref, gb_ref, hm_ref, hs_ref, o_ref = refs[2 * SCAN_SEQS:]
    mem, _ = _member_of_block(pl.program_id(0), rows, tm)
    y = y_refs[0][...] + y_refs[1][...]
    for m in range(1, SCAN_SEQS):
        y = jnp.where(mem == m, y_refs[2 * m][...] + y_refs[2 * m + 1][...], y)
    mu = _dot2(y, hm_ref[...])
    yc = y - mu
    var = _dot2(yc * yc, hm_ref[...])
    yn = yc * lax.rsqrt(var + RWKV_GN_EPS) * gw_ref[...] + gb_ref[...]
    bonus = _dot2(r_ref[...] * k_ref[...] * rk_ref[...], hs_ref[...]) * v_ref[...]
    o_ref[...] = (yn + bonus) * g_ref[...]


def _rw_post(ys, u, g, r_k, gn_w, gn_b, hmean, hsum, rows, tm=256):
    t = u.shape[0]
    c = RWKV_WIDTH
    cb = COL_RKV // c

    def y_spec(m):
        def index(i):
            mem, other = _member_of_block(i, rows, tm)
            return jnp.where(mem == m, i, other), 0
        return pl.BlockSpec((tm, c), index)

    return pl.pallas_call(
        functools.partial(_rw_post_body, rows=rows, tm=tm),
        grid=(t // tm,),
        in_specs=[y_spec(j // 2) for j in range(2 * SCAN_SEQS)] + [
            pl.BlockSpec((tm, c), lambda i: (i, cb)),
            pl.BlockSpec((tm, c), lambda i: (i, cb + 1)),
            pl.BlockSpec((tm, c), lambda i: (i, cb + 2)),
            pl.BlockSpec((tm, c), lambda i: (i, 0)),
            _full(r_k), _full(gn_w), _full(gn_b), _full(hmean), _full(hsum)],
        out_specs=pl.BlockSpec((tm, c), lambda i: (i, 0)),
        out_shape=jax.ShapeDtypeStruct((t, c), f32),
        compiler_params=_cp(("arbitrary",)),
        name="rw_post",
    )(*ys, u, u, u, g, r_k, gn_w, gn_b, hmean, hsum)


def _outproj_body(*refs, nx, nctx):
    att_refs, hy_refs = refs[0:2], refs[2:4]
    rw_ref, w_ref = refs[4:6]
    x_refs = refs[6:6 + nx]
    g_ref, nw_ref, o_ref = refs[6 + nx:]
    o = (_dot(_pair_load(att_refs, nctx).astype(bf16), w_ref[0:ATT_WIDTH, :])
         + _dot(_pair_load(hy_refs, nctx).astype(bf16), w_ref[ATT_WIDTH:ATT_WIDTH + HY_CH, :])
         + _dot(rw_ref[...].astype(bf16), w_ref[ATT_WIDTH + HY_CH:, :]))
    y = o * lax.rsqrt(jnp.mean(o * o, axis=-1, keepdims=True) + RMS_EPS) * nw_ref[...]
    o_ref[...] = _pair_load(x_refs, nctx) + g_ref[...] * y


def _out_proj(atts, hys, rw, w_out, layer, xs, mod, nw, rows, tm=256):
    t = rows.t
    _, mix, d = w_out.shape
    midx = rows.mod_index(tm)
    body = functools.partial(_outproj_body, nx=len(xs), nctx=rows.t_ctx // tm)
    return pl.pallas_call(
        body,
        grid=(t // tm,),
        in_specs=_pair_specs(atts, rows, tm) + _pair_specs(hys, rows, tm) + [
            pl.BlockSpec((tm, RWKV_WIDTH), lambda i: (i, 0)),
            pl.BlockSpec((None, mix, d), lambda i: (layer, 0, 0))] + _pair_specs(xs, rows, tm) + [
            pl.BlockSpec((None, 1, d), lambda i: (midx(i), 0, 2)),
            pl.BlockSpec((None, 1, d), lambda i: (1, 0, 0))],
        out_specs=pl.BlockSpec((tm, d), lambda i: (i, 0)),
        out_shape=jax.ShapeDtypeStruct((t, d), f32),
        compiler_params=_cp(("arbitrary",)),
        name="out_proj",
    )(*atts, *hys, rw, w_out, *xs, mod, nw)


def _mlp_body(x_ref, sh_ref, sc_ref, g_ref, nw2_ref, nw3_ref, w1_ref, w2_ref, *rest, nctx):
    o_refs, (h_scr, acc_scr) = rest[:-2], rest[-2:]
    i = pl.program_id(0)
    j = pl.program_id(1)

    @pl.when(j == 0)
    def _():
        h_scr[...] = _normmod(x_ref[...], nw2_ref[...], sc_ref[...], sh_ref[...]).astype(bf16)
        acc_scr[...] = jnp.zeros_like(acc_scr)

    a = jnp.maximum(_dot(h_scr[...], w1_ref[...]), 0.0)
    acc_scr[...] += _dot((a * a).astype(bf16), w2_ref[...])

    def result():
        f = acc_scr[...]
        y = f * lax.rsqrt(jnp.mean(f * f, axis=-1, keepdims=True) + RMS_EPS) * nw3_ref[...]
        return x_ref[...] + g_ref[...] * y

    last = j == pl.num_programs(1) - 1
    if len(o_refs) == 1:
        @pl.when(last)
        def _():
            o_refs[0][...] = result()
    else:
        @pl.when(last & (i < nctx))
        def _():
            o_refs[0][...] = result()

        @pl.when(last & (i >= nctx))
        def _():
            o_refs[1][...] = result()


def _mlp(x, mod, nw, w1, w2, layer, rows, split_out, tm=512, tf=1024):
    t, d = x.shape
    dff = w1.shape[2]
    midx = rows.mod_index(tm)
    nctx = rows.t_ctx // tm
    if split_out:
        out_specs = [pl.BlockSpec((tm, d), lambda i, j: (jnp.minimum(i, nctx - 1), 0)),
                     pl.BlockSpec((tm, d), lambda i, j: (jnp.maximum(i - nctx, 0), 0))]
        out_shape = [jax.ShapeDtypeStruct((rows.t_ctx, d), f32), jax.ShapeDtypeStruct((rows.t_lat, d), f32)]
    else:
        out_specs = pl.BlockSpec((tm, d), lambda i, j: (i, 0))
        out_shape = jax.ShapeDtypeStruct((t, d), f32)
    return pl.pallas_call(
        functools.partial(_mlp_body, nctx=nctx),
        grid=(t // tm, dff // tf),
        in_specs=[pl.BlockSpec((tm, d), lambda i, j: (i, 0)),
                  pl.BlockSpec((None, 1, d), lambda i, j: (midx(i), 0, 3)),
                  pl.BlockSpec((None, 1, d), lambda i, j: (midx(i), 0, 4)),
                  pl.BlockSpec((None, 1, d), lambda i, j: (midx(i), 0, 5)),
                  pl.BlockSpec((None, 1, d), lambda i, j: (2, 0, 0)),
                  pl.BlockSpec((None, 1, d), lambda i, j: (3, 0, 0)),
                  pl.BlockSpec((None, d, tf), lambda i, j: (layer, 0, j)),
                  pl.BlockSpec((None, tf, d), lambda i, j: (layer, j, 0))],
        out_specs=out_specs,
        out_shape=out_shape,
        scratch_shapes=[pltpu.VMEM((tm, d), bf16), pltpu.VMEM((tm, d), f32)],
        compiler_params=_cp(("arbitrary", "arbitrary")),
        name="mlp",
    )(x, mod, mod, mod, nw, nw, w1, w2)


def _lora_weight(w_up, a_up, g_up):
    c = RWKV_WIDTH
    w = jnp.zeros((LORA_IN, 5 * c), f32)
    r = 0
    for j, blk in enumerate((w_up[0], w_up[1], a_up[0], a_up[1], g_up)):
        w = w.at[r:r + blk.shape[0], j * c:(j + 1) * c].set(blk)
        r += blk.shape[0]
    return w.astype(bf16)


def _regroup_in_cols(w):
    kv0 = ATT_WIDTH
    hy0 = kv0 + 2 * ATT_KV
    rw0 = hy0 + 3 * HY_CH
    lo0 = rw0 + 3 * RWKV_WIDTH
    return jnp.concatenate([w[..., :kv0], w[..., rw0:lo0], w[..., hy0:rw0], w[..., kv0:hy0], w[..., lo0:]], axis=-1)


def _conv_taps(hy_short_w, rw_short_w):
    def ident(n):
        return jnp.zeros((3, n), f32).at[1].set(1.0)
    return jnp.concatenate([ident(ATT_WIDTH), rw_short_w, hy_short_w, ident(2 * ATT_KV + LORA_IN)], axis=1)


def kernel(x_prompt, x_sample, cache_k, cache_v, state_rwkv, c, c_ctx, ada_w, ada_b, norm_w, w_in, w_out,
           attn_sink, hy_short_w, hy_f1, hy_b1, hy_f2, hy_b2, hy_f3, hy_decay, hy_skip, rw_short_w, rw_w0,
           rw_w_up, rw_a0, rw_a_up, rw_g_up, rw_k_k, rw_k_a, rw_r_k, rw_gn_w, rw_gn_b, mlp_w1, mlp_w2):
    b_ctx, l_ctx, d = x_prompt.shape
    b_lat, l_lat, _ = x_sample.shape
    depth = ada_w.shape[0]
    grid_w = 64
    rows = _Rows(b_ctx, l_ctx, b_lat, l_lat)

    xs = (x_prompt.reshape(rows.t_ctx, d), x_sample.reshape(rows.t_lat, d))
    cond16 = jnp.zeros((16, d), f32).at[0].set(c_ctx).at[1:1 + b_lat].set(c)
    mod_all = _modulation(cond16, ada_w, ada_b[:, None, :])

    w_in_b = _regroup_in_cols(w_in).astype(bf16)
    w_out_b = w_out.astype(bf16)
    w1_b = mlp_w1.astype(bf16)
    w2_b = mlp_w2.astype(bf16)

    cos_t, sin_t = _rope_tables(l_lat, grid_w)
    hsum = _head_sum_matrix()
    hmean = _head_sum_matrix(1.0 / HEAD_DIM)
    scan_tabs, n_steps = _scan_tables(rows)
    dft = {}
    for l in (l_ctx, l_lat):
        cm, sm, cmt, smt = _dft_tables(l)
        feat, t01 = _hy_feat(l)
        dft[l] = dict(cm=cm, sm=sm, cmb=cm.astype(bf16), smb=sm.astype(bf16),
                      cmtb=cmt.astype(bf16), smtb=smt.astype(bf16), feat=feat, t01=t01)

    new_k, new_v, new_s = [], [], []
    for l in range(depth):
        mod = mod_all[l].reshape(16, 1, N_MOD * d)
        nw = norm_w[l].reshape(4, 1, d)
        u = _in_proj(xs, mod, nw, w_in_b, l, _conv_taps(hy_short_w[l], rw_short_w[l]), rows)

        new_k.append(u[:rows.t_ctx, COL_KV:COL_KV + ATT_KV].reshape(b_ctx, l_ctx, N_KV_HEADS, HEAD_DIM))
        new_v.append(u[:rows.t_ctx, COL_KV + ATT_KV:COL_KV + 2 * ATT_KV].reshape(b_ctx, l_ctx, N_KV_HEADS, HEAD_DIM))

        att_c = _attn_ctx(u, attn_sink[l], rows)
        att_l = _attn_lat(u, cache_k[:, l].reshape(b_lat, -1, ATT_KV), cache_v[:, l].reshape(b_lat, -1, ATT_KV),
                          attn_sink[l], cos_t, sin_t, rows)

        f1p = jnp.pad(hy_f1[l], ((0, 128 - HY_EMB), (0, 0)))
        hys = []
        for (bb, ll, rb0) in ((b_ctx, l_ctx, 0), (b_lat, l_lat, rows.t_ctx // l_lat)):
            tb = dft[ll]
            hr, hi = _hy_filter(ll, tb['feat'], tb['t01'], f1p, hy_b1[l][None], hy_f2[l], hy_b2[l][None],
                                hy_f3[l], hy_decay[l][None], tb['cm'], tb['sm'])
            hys.append(_hy_conv(u, hr, hi, hy_skip[l], tb['cmb'], tb['smb'], tb['cmtb'], tb['smtb'],
                                bb, ll, rb0))

        w_lora = _lora_weight(rw_w_up[l], rw_a_up[l], rw_g_up[l])
        kk, lw, a, g = _rw_prep(u, w_lora, rw_w0[l], rw_a0[l], rw_k_k[l][None], hsum)
        s0 = jnp.concatenate([jnp.zeros((b_ctx, 2, N_RWKV_HEADS, HEAD_DIM, HEAD_DIM), f32),
                              state_rwkv[:, l]], axis=0)
        ys, sfin = _rw_scan(u, kk, lw, a, rw_k_a[l][None], s0, scan_tabs, n_steps)
        new_s.append(sfin[:b_ctx])
        rw = _rw_post(ys, u, g, rw_r_k[l].reshape(1, RWKV_WIDTH), rw_gn_w[l][None], rw_gn_b[l][None],
                      hmean, hsum, rows)

        x = _out_proj((att_c, att_l), tuple(hys), rw, w_out_b, l, xs, mod, nw, rows)
        x = _mlp(x, mod, nw, w1_b, w2_b, l, rows, split_out=(l == depth - 1))
        xs = (x,)

    y_p, y_s = x
    return (y_p.reshape(b_ctx, l_ctx, d), y_s.reshape(b_lat, l_lat, d),
            jnp.stack(new_k, axis=1), jnp.stack(new_v, axis=1), jnp.stack(new_s, axis=1))
```

```python
import functools
import math

import numpy as np
import jax
import jax.numpy as jnp
from jax import lax
from jax.experimental import pallas as pl
from jax.experimental.pallas import tpu as pltpu

f32 = jnp.float32
bf16 = jnp.bfloat16

HEAD_DIM = 64
N_ATT_HEADS = 12
N_KV_HEADS = 4
ATT_GROUP = 3
ATT_WIDTH = 768
ATT_KV = 256
WINDOW = 128
ROPE_THETA = 10000.0
NEG_INF = -1e30
HY_CH = 512
HY_BANDS = 16
HY_EMB = 33
HY_FFN = 64
HY_MOD_SHIFT = 0.05
RWKV_WIDTH = 768
N_RWKV_HEADS = 12
DECAY_LORA = 96
AAA_LORA = 96
GATE_LORA = 256
RWKV_GN_EPS = 64e-5
N_MOD = 6
RMS_EPS = 1e-6
CHUNK = 64
SCAN_SEQS = 2
CONV_ROWS = 128
LORA_PART = 256
LORA_IN = 4 * LORA_PART
COL_Q = 0
COL_RKV = COL_Q + ATT_WIDTH
COL_HY = COL_RKV + 3 * RWKV_WIDTH
COL_KV = COL_HY + 3 * HY_CH
COL_LORA = COL_KV + 2 * ATT_KV
IN_COLS = COL_LORA + LORA_IN
HI = lax.Precision.HIGHEST
VMEM_LIMIT = 56 * 1024 * 1024


def _cp(sem, vmem=VMEM_LIMIT):
    return pltpu.CompilerParams(dimension_semantics=sem, vmem_limit_bytes=vmem)


def _dot(a, b):
    return jnp.dot(a, b, preferred_element_type=f32)


def _dot_nt(a, b):
    return lax.dot_general(a, b, (((1,), (1,)), ((), ())), preferred_element_type=f32)


def _dot_tn(a, b):
    return lax.dot_general(a, b, (((0,), (0,)), ((), ())), preferred_element_type=f32)


def _dot_hi(a, b):
    return jnp.dot(a, b, preferred_element_type=f32, precision=HI)


def _full(a):
    return pl.BlockSpec(a.shape, lambda *_: (0,) * a.ndim)


def _mod_body(c_ref, w_ref, b_ref, o_ref):
    c = c_ref[...]
    s = (c * jax.nn.sigmoid(c)).astype(bf16)
    o_ref[...] = _dot(s, w_ref[...].astype(bf16)) + b_ref[...]


def _modulation(cond16, ada_w, ada_b):
    depth, d, n = ada_w.shape
    tn = 1024
    return pl.pallas_call(
        _mod_body,
        grid=(depth, n // tn),
        in_specs=[pl.BlockSpec((16, d), lambda l, j: (0, 0)),
                  pl.BlockSpec((None, d, tn), lambda l, j: (l, 0, j)),
                  pl.BlockSpec((None, 1, tn), lambda l, j: (l, 0, j))],
        out_specs=pl.BlockSpec((None, 16, tn), lambda l, j: (l, 0, j)),
        out_shape=jax.ShapeDtypeStruct((depth, 16, n), f32),
        compiler_params=_cp(("arbitrary", "arbitrary")),
        name="modulation",
    )(cond16, ada_w, ada_b)


def _normmod(x, nw, sc, sh):
    y = x * lax.rsqrt(jnp.mean(x * x, axis=-1, keepdims=True) + RMS_EPS) * nw
    return y * (1.0 + sc) + sh


class _Rows:
    def __init__(self, b_ctx, l_ctx, b_lat, l_lat):
        self.b_ctx, self.l_ctx, self.b_lat, self.l_lat = b_ctx, l_ctx, b_lat, l_lat
        self.t_ctx = b_ctx * l_ctx
        self.t_lat = b_lat * l_lat
        self.t = self.t_ctx + self.t_lat

    def mod_index(self, tm):
        nctx = self.t_ctx // tm
        per = self.l_lat // tm

        def f(i):
            return jnp.where(i < nctx, 0, 1 + (i - nctx) // per)
        return f


def _pair_specs(arrs, rows, tm, single_buffer=False):
    w = arrs[0].shape[1]
    if len(arrs) == 1:
        return [pl.BlockSpec((tm, w), lambda i, *_: (i, 0))]
    nctx = rows.t_ctx // tm
    mode = dict(pipeline_mode=pl.Buffered(1)) if single_buffer else {}
    return [pl.BlockSpec((tm, w), lambda i, *_: (jnp.minimum(i, nctx - 1), 0), **mode),
            pl.BlockSpec((tm, w), lambda i, *_: (jnp.maximum(i - nctx, 0), 0), **mode)]


def _pair_load(refs, nctx):
    if len(refs) == 1:
        return refs[0][...]
    return jnp.where(pl.program_id(0) < nctx, refs[0][...], refs[1][...])


def _inproj_body(*refs, nx, nctx, lmask_ctx, lmask_lat):
    x_refs = refs[:nx]
    sh_ref, sc_ref, nw_ref, w_ref, wc_ref, o_ref, h_scr, u_scr = refs[nx:]
    i = pl.program_id(0)

    @pl.when(pl.program_id(1) == 0)
    def _():
        h_scr[...] = _normmod(_pair_load(x_refs, nctx), nw_ref[...], sc_ref[...], sh_ref[...]).astype(bf16)

    u_scr[...] = _dot(h_scr[...], w_ref[...])
    tm = u_scr.shape[0]
    ch = CONV_ROWS
    lmask = jnp.where(i < nctx, lmask_ctx, lmask_lat)
    row = lax.broadcasted_iota(jnp.int32, (ch, 1), 0)
    wc = wc_ref[...]
    for r0 in range(0, tm, ch):
        x = u_scr[r0:r0 + ch, :]
        at_start = (r0 & lmask) == 0
        at_end = ((r0 + ch - 1) & lmask) == lmask
        prev = jnp.where(at_start, 0.0, u_scr[max(r0 - 1, 0):max(r0 - 1, 0) + 1, :])
        nxt = jnp.where(at_end, 0.0, u_scr[min(r0 + ch, tm - 1):min(r0 + ch, tm - 1) + 1, :])
        xm = jnp.where(row == 0, prev, pltpu.roll(x, 1, 0))
        xp = jnp.where(row == ch - 1, nxt, pltpu.roll(x, ch - 1, 0))
        o_ref[r0:r0 + ch, :] = xm * wc[0:1, :] + x * wc[1:2, :] + xp * wc[2:3, :]


def _in_proj(xs, mod, nw, w_in, layer, wconv, rows, tm=1024, tn=1024):
    t = rows.t
    _, d, n = w_in.shape
    for l in (rows.l_ctx, rows.l_lat):
        assert l & (l - 1) == 0 and tm % l == 0 and l % CONV_ROWS == 0
    midx = rows.mod_index(tm)
    body = functools.partial(_inproj_body, nx=len(xs), nctx=rows.t_ctx // tm,
                             lmask_ctx=rows.l_ctx - 1, lmask_lat=rows.l_lat - 1)
    return pl.pallas_call(
        body,
        grid=(t // tm, n // tn),
        in_specs=_pair_specs(xs, rows, tm, single_buffer=True) + [
            pl.BlockSpec((None, 1, d), lambda i, j: (midx(i), 0, 0)),
            pl.BlockSpec((None, 1, d), lambda i, j: (midx(i), 0, 1)),
            pl.BlockSpec((None, 1, d), lambda i, j: (0, 0, 0)),
            pl.BlockSpec((None, d, tn), lambda i, j: (layer, 0, j)),
            pl.BlockSpec((3, tn), lambda i, j: (0, j))],
        out_specs=pl.BlockSpec((tm, tn), lambda i, j: (i, j)),
        out_shape=jax.ShapeDtypeStruct((t, n), f32),
        scratch_shapes=[pltpu.VMEM((tm, d), bf16), pltpu.VMEM((tm, tn), f32)],
        compiler_params=_cp(("arbitrary", "arbitrary")),
        name="in_proj",
    )(*xs, mod, mod, nw, w_in, wconv)


def _attn_ctx_body(sink_ref, q_ref, k_ref, v_ref, o_ref):
    l = q_ref.shape[0]
    scale = HEAD_DIM ** -0.5
    kvh = range(N_KV_HEADS)
    qb = (q_ref[...] * scale).astype(bf16)
    kb = k_ref[...].astype(bf16)
    vb = v_ref[...].astype(bf16)
    hcols = [slice(n * HEAD_DIM, (n + 1) * HEAD_DIM) for n in kvh]
    qs = [jnp.concatenate([qb[:, (n * ATT_GROUP + g) * HEAD_DIM:(n * ATT_GROUP + g + 1) * HEAD_DIM]
                           for g in range(ATT_GROUP)], axis=0) for n in kvh]
    sink_col = [jnp.concatenate([jnp.full((l, 1), sink_ref[n * ATT_GROUP + g], f32) for g in range(ATT_GROUP)],
                                axis=0) for n in kvh]
    s = [_dot_nt(qs[n], kb[:, hcols[n]]) for n in kvh]
    m = [jnp.maximum(jnp.max(s[n], axis=-1, keepdims=True), sink_col[n]) for n in kvh]
    p = [jnp.exp(s[n] - m[n]) for n in kvh]
    den = [jnp.exp(sink_col[n] - m[n]) + jnp.sum(p[n], axis=-1, keepdims=True) for n in kvh]
    o = [_dot(p[n].astype(bf16), vb[:, hcols[n]]) / den[n] for n in kvh]
    for n in kvh:
        for g in range(ATT_GROUP):
            h = n * ATT_GROUP + g
            o_ref[:, h * HEAD_DIM:(h + 1) * HEAD_DIM] = o[n][g * l:(g + 1) * l, :]


def _attn_ctx(u, sink, rows):
    b, l = rows.b_ctx, rows.l_ctx
    return pl.pallas_call(
        _attn_ctx_body,
        grid=(b,),
        in_specs=[pl.BlockSpec(memory_space=pltpu.SMEM),
                  pl.BlockSpec((l, ATT_WIDTH), lambda i: (i, COL_Q // ATT_WIDTH)),
                  pl.BlockSpec((l, ATT_KV), lambda i: (i, COL_KV // ATT_KV)),
                  pl.BlockSpec((l, ATT_KV), lambda i: (i, COL_KV // ATT_KV + 1))],
        out_specs=pl.BlockSpec((l, ATT_WIDTH), lambda i: (i, 0)),
        out_shape=jax.ShapeDtypeStruct((b * l, ATT_WIDTH), f32),
        compiler_params=_cp(("arbitrary",)),
        name="attn_ctx",
    )(sink, u, u, u)


def _rope(x, cos, sin_signed):
    w = x.shape[1]
    lane = lax.broadcasted_iota(jnp.int32, (1, w), 1)
    swapped = jnp.where((lane % 32) < 16, pltpu.roll(x, w - 16, 1), pltpu.roll(x, 16, 1))
    return x * cos + swapped * sin_signed


def _attn_lat_body(sink_ref, q_ref, k_ref, v_ref, kc_ref, vc_ref, cos_ref, sin_ref, o_ref,
                   q_scr, k_scr, v_scr, kc_scr, vc_scr):
    l = q_ref.shape[0]
    blk = WINDOW
    nb = l // blk
    scale = HEAD_DIM ** -0.5
    cos = cos_ref[...]
    sin = sin_ref[...]
    q_scr[...] = (_rope(q_ref[...], cos, sin) * scale).astype(bf16)
    zpad = jnp.zeros((blk, ATT_KV), bf16)
    k_scr[0:blk, :] = zpad
    k_scr[blk + l:, :] = zpad
    v_scr[0:blk, :] = zpad
    v_scr[blk + l:, :] = zpad
    k_scr[blk:blk + l, :] = _rope(k_ref[...], cos[:, :ATT_KV], sin[:, :ATT_KV]).astype(bf16)
    v_scr[blk:blk + l, :] = v_ref[...].astype(bf16)
    r = lax.broadcasted_iota(jnp.int32, (ATT_GROUP * blk, 3 * blk), 0) % blk
    c = lax.broadcasted_iota(jnp.int32, (ATT_GROUP * blk, 3 * blk), 1)
    band = (c - r >= 0) & (c - r <= 2 * WINDOW)
    kvh = range(N_KV_HEADS)
    hcols = [slice(n * HEAD_DIM, (n + 1) * HEAD_DIM) for n in kvh]
    kc_scr[...] = kc_ref[...].astype(bf16)
    vc_scr[...] = vc_ref[...].astype(bf16)
    sink_col = [jnp.concatenate([jnp.full((blk, 1), sink_ref[n * ATT_GROUP + g], f32) for g in range(ATT_GROUP)],
                                axis=0) for n in kvh]
    for i in range(nb):
        kpos = (i - 1) * blk + c
        mask = band & (kpos >= 0) & (kpos < l)
        rows_q = slice(i * blk, (i + 1) * blk)
        rows_k = slice(i * blk, (i + 3) * blk)
        qs = [jnp.concatenate([q_scr[rows_q, (n * ATT_GROUP + g) * HEAD_DIM:(n * ATT_GROUP + g + 1) * HEAD_DIM]
                               for g in range(ATT_GROUP)], axis=0) for n in kvh]
        s_loc = [jnp.where(mask, _dot_nt(qs[n], k_scr[rows_k, hcols[n]]), NEG_INF) for n in kvh]
        s_ctx = [_dot_nt(qs[n], kc_scr[:, hcols[n]]) for n in kvh]
        m = [jnp.maximum(jnp.maximum(jnp.max(s_loc[n], axis=-1, keepdims=True),
                                     jnp.max(s_ctx[n], axis=-1, keepdims=True)), sink_col[n]) for n in kvh]
        p_loc = [jnp.exp(s_loc[n] - m[n]) for n in kvh]
        p_ctx = [jnp.exp(s_ctx[n] - m[n]) for n in kvh]
        den = [jnp.exp(sink_col[n] - m[n]) + jnp.sum(p_loc[n], axis=-1, keepdims=True)
               + jnp.sum(p_ctx[n], axis=-1, keepdims=True) for n in kvh]
        o = [(_dot(p_loc[n].astype(bf16), v_scr[rows_k, hcols[n]])
              + _dot(p_ctx[n].astype(bf16), vc_scr[:, hcols[n]])) / den[n] for n in kvh]
        for n in kvh:
            for g in range(ATT_GROUP):
                h = n * ATT_GROUP + g
                o_ref[rows_q, h * HEAD_DIM:(h + 1) * HEAD_DIM] = o[n][g * blk:(g + 1) * blk, :]


def _attn_lat(u, kc, vc, sink, cos, sin, rows):
    b, l = rows.b_lat, rows.l_lat
    assert rows.t_ctx % l == 0
    rb0 = rows.t_ctx // l
    lc = kc.shape[1]
    return pl.pallas_call(
        _attn_lat_body,
        grid=(b,),
        in_specs=[pl.BlockSpec(memory_space=pltpu.SMEM),
                  pl.BlockSpec((l, ATT_WIDTH), lambda i: (rb0 + i, COL_Q // ATT_WIDTH)),
                  pl.BlockSpec((l, ATT_KV), lambda i: (rb0 + i, COL_KV // ATT_KV)),
                  pl.BlockSpec((l, ATT_KV), lambda i: (rb0 + i, COL_KV // ATT_KV + 1)),
                  pl.BlockSpec((None, lc, ATT_KV), lambda i: (i, 0, 0)),
                  pl.BlockSpec((None, lc, ATT_KV), lambda i: (i, 0, 0)),
                  pl.BlockSpec((l, ATT_WIDTH), lambda i: (0, 0)),
                  pl.BlockSpec((l, ATT_WIDTH), lambda i: (0, 0))],
        out_specs=pl.BlockSpec((l, ATT_WIDTH), lambda i: (i, 0)),
        out_shape=jax.ShapeDtypeStruct((b * l, ATT_WIDTH), f32),
        scratch_shapes=[pltpu.VMEM((l, ATT_WIDTH), bf16),
                        pltpu.VMEM((l + 2 * WINDOW, ATT_KV), bf16),
                        pltpu.VMEM((l + 2 * WINDOW, ATT_KV), bf16),
                        pltpu.VMEM((lc, ATT_KV), bf16),
                        pltpu.VMEM((lc, ATT_KV), bf16)],
        compiler_params=_cp(("arbitrary",)),
        name="attn_lat",
    )(sink, u, u, u, kc, vc, cos, sin)


def _rope_tables(l, grid_w):
    half = HEAD_DIM // 2
    pos = jnp.arange(l)
    rowp = (pos // grid_w).astype(f32)
    colp = (pos % grid_w).astype(f32)
    freqs = ROPE_THETA ** (-jnp.arange(0, half, 2, dtype=f32) / half)
    ar = rowp[:, None] * freqs[None, :]
    ac = colp[:, None] * freqs[None, :]
    cos = jnp.concatenate([jnp.cos(ar), jnp.cos(ar), jnp.cos(ac), jnp.cos(ac)], axis=-1)
    sin = jnp.concatenate([-jnp.sin(ar), jnp.sin(ar), -jnp.sin(ac), jnp.sin(ac)], axis=-1)
    return jnp.tile(cos, (1, N_ATT_HEADS)), jnp.tile(sin, (1, N_ATT_HEADS))


def _dft_tables(l):
    f = 32
    k = jnp.arange(l, dtype=jnp.int32)[:, None]

    def narrow(svals):
        m = ((2 * k + 1) * svals[None, :]) % (4 * l)
        ang = m.astype(f32) * (math.pi / (2 * l))
        return jnp.cos(ang), jnp.sin(ang)

    c1, s1 = narrow(jnp.arange(l // f, dtype=jnp.int32) * f)
    c0, s0 = narrow(jnp.arange(f, dtype=jnp.int32))
    cm = (c1[:, :, None] * c0[:, None, :] - s1[:, :, None] * s0[:, None, :]).reshape(l, l)
    sm = (s1[:, :, None] * c0[:, None, :] + c1[:, :, None] * s0[:, None, :]).reshape(l, l)
    c1t, s1t, c0t, s0t = c1.T, s1.T, c0.T, s0.T
    cmt = (c1t[:, None, :] * c0t[None, :, :] - s1t[:, None, :] * s0t[None, :, :]).reshape(l, l)
    smt = (s1t[:, None, :] * c0t[None, :, :] + c1t[:, None, :] * s0t[None, :, :]).reshape(l, l)
    return cm, sm, cmt, smt


def _hy_feat(l):
    t = jnp.arange(l, dtype=f32)
    t01 = (t / max(l - 1, 1))[:, None]
    bands = jnp.linspace(1e-4, HY_BANDS - 1, HY_BANDS, dtype=f32)
    ang = (2.0 * math.pi / l) * t[:, None] * bands[None, :]
    feat = jnp.concatenate([t01, jnp.cos(ang), -jnp.sin(ang)], axis=-1)
    return jnp.pad(feat, ((0, 0), (0, 128 - HY_EMB))), t01


def _hy_filter_body(feat_ref, t01_ref, f1_ref, b1_ref, f2_ref, b2_ref, f3_ref, dec_ref, cm_ref, sm_ref,
                    hr_ref, hi_ref):
    h = jnp.sin(_dot_hi(feat_ref[...], f1_ref[...]) + b1_ref[...])
    h = jnp.sin(_dot_hi(h, f2_ref[...]) + b2_ref[...])
    h = _dot_hi(h, f3_ref[...])
    h = h * (jnp.exp(-t01_ref[...] * jnp.abs(dec_ref[...])) + HY_MOD_SHIFT)
    l = h.shape[0]
    fwd = h[:, :HY_CH]
    row = lax.broadcasted_iota(jnp.int32, (l, 1), 0)
    bwd = jnp.where(row == 0, 0.0, h[:, HY_CH:])
    hr_ref[...] = _dot_hi(cm_ref[...], fwd + bwd)
    hi_ref[...] = _dot_hi(sm_ref[...], bwd - fwd)


def _hy_filter(l, feat, t01, f1p, b1, f2, b2, f3, dec, cm, sm):
    args = (feat, t01, f1p, b1, f2, b2, f3, dec, cm, sm)
    return pl.pallas_call(
        _hy_filter_body,
        grid=(1,),
        in_specs=[_full(a) for a in args],
        out_specs=[pl.BlockSpec((l, HY_CH), lambda i: (0, 0))] * 2,
        out_shape=[jax.ShapeDtypeStruct((l, HY_CH), f32)] * 2,
        compiler_params=_cp(("arbitrary",)),
        name="hy_filter",
    )(*args)


def _hy_conv_body(x0_ref, x1_ref, v_ref, hr_ref, hi_ref, skip_ref, cm_ref, sm_ref, cmt_ref, smt_ref, o_ref):
    l = v_ref.shape[0]
    z = v_ref[...] * x1_ref[...]
    zb = z.astype(bf16)
    zr = _dot(cm_ref[...], zb)
    zs = _dot(sm_ref[...], zb)
    hr = hr_ref[...]
    hi = hi_ref[...]
    yr = (zr * hr + zs * hi).astype(bf16)
    yi = (zr * hi - zs * hr).astype(bf16)
    y = (_dot(cmt_ref[...], yr) - _dot(smt_ref[...], yi)) * (1.0 / l)
    o_ref[...] = (y + skip_ref[...] * z) * x0_ref[...]


def _hy_conv(u, hr, hi, skip, cm, sm, cmt, smt, b, l, rb0):
    cb = COL_HY // HY_CH
    return pl.pallas_call(
        _hy_conv_body,
        grid=(b,),
        in_specs=[pl.BlockSpec((l, HY_CH), lambda i: (rb0 + i, cb)),
                  pl.BlockSpec((l, HY_CH), lambda i: (rb0 + i, cb + 1)),
                  pl.BlockSpec((l, HY_CH), lambda i: (rb0 + i, cb + 2)),
                  _full(hr), _full(hi), _full(skip), _full(cm), _full(sm), _full(cmt), _full(smt)],
        out_specs=pl.BlockSpec((l, HY_CH), lambda i: (i, 0)),
        out_shape=jax.ShapeDtypeStruct((b * l, HY_CH), f32),
        compiler_params=_cp(("arbitrary",)),
        name="hy_conv",
    )(u, u, u, hr, hi, skip, cm, sm, cmt, smt)


def _head_sum_matrix(scale=1.0):
    i = np.arange(RWKV_WIDTH) // HEAD_DIM
    return jnp.asarray((i[:, None] == i[None, :]).astype(np.float32) * scale).astype(bf16)


def _dot2(x, m):
    hi = x.astype(bf16)
    lo = (x - hi.astype(f32)).astype(bf16)
    return _dot(hi, m) + _dot(lo, m)


def _rw_prep_body(lo_ref, k_ref, ww_ref, wa_ref, wg_ref, w0_ref, a0_ref, kk_ref_, hs_ref, kk_o, lw_o, a_o, g_o):
    p = LORA_PART
    lo_w = _dot(jnp.tanh(lo_ref[:, 0:p]).astype(bf16), ww_ref[...])
    lo_a = _dot(lo_ref[:, p:2 * p].astype(bf16), wa_ref[...])
    g_o[...] = _dot(jax.nn.sigmoid(lo_ref[:, 2 * p:3 * p]).astype(bf16), wg_ref[...])
    c = RWKV_WIDTH
    for d in range(2):
        zneg = -(w0_ref[d:d + 1, :] + lo_w[:, d * c:(d + 1) * c])
        softplus = jnp.maximum(zneg, 0.0) + jnp.log(1.0 + jnp.exp(-jnp.abs(zneg)))
        w_log = -softplus - 0.5
        lw_o[d] = -jnp.exp(w_log)
        a_o[d] = jax.nn.sigmoid(a0_ref[d:d + 1, :] + lo_a[:, d * c:(d + 1) * c])
    kk = k_ref[...] * kk_ref_[...]
    ss = _dot2(kk * kk, hs_ref[...])
    kk_o[...] = kk / jnp.maximum(jnp.sqrt(ss), 1e-12)


def _rw_prep(u, w_lora, w0, a0, k_k, hsum, tm=256):
    t = u.shape[0]
    c = RWKV_WIDTH
    return pl.pallas_call(
        _rw_prep_body,
        grid=(t // tm,),
        in_specs=[pl.BlockSpec((tm, LORA_IN), lambda i: (i, COL_LORA // LORA_IN)),
                  pl.BlockSpec((tm, c), lambda i: (i, COL_RKV // c + 1)),
                  _full(w_lora[0]), _full(w_lora[1]), _full(w_lora[2]),
                  _full(w0), _full(a0), _full(k_k), _full(hsum)],
        out_specs=[pl.BlockSpec((tm, c), lambda i: (i, 0)),
                   pl.BlockSpec((2, tm, c), lambda i: (0, i, 0)),
                   pl.BlockSpec((2, tm, c), lambda i: (0, i, 0)),
                   pl.BlockSpec((tm, c), lambda i: (i, 0))],
        out_shape=[jax.ShapeDtypeStruct((t, c), f32),
                   jax.ShapeDtypeStruct((2, t, c), f32),
                   jax.ShapeDtypeStruct((2, t, c), f32),
                   jax.ShapeDtypeStruct((t, c), f32)],
        compiler_params=_cp(("arbitrary",)),
        name="rw_prep",
    )(u, u, *w_lora, w0, a0, k_k, hsum)


def _split3(x):
    hi = x.astype(bf16)
    r1 = x - hi.astype(f32)
    mid = r1.astype(bf16)
    lo = (r1 - mid.astype(f32)).astype(bf16)
    return hi, mid, lo


def _rw_dir_operands(d, r_ref, k_ref, v_ref, kk_ref, lw_ref, a_ref, ka):
    cdim = CHUNK
    row = lax.broadcasted_iota(jnp.int32, (2 * cdim, 4 * cdim), 0)
    col = lax.broadcasted_iota(jnp.int32, (2 * cdim, 4 * cdim), 1) % cdim
    lag = (row % cdim) - col if d == 0 else col - (row % cdim)
    mask = lag >= jnp.where(row < cdim, 1, 0)
    row1 = lax.broadcasted_iota(jnp.int32, (cdim, cdim), 0)
    col1 = lax.broadcasted_iota(jnp.int32, (cdim, cdim), 1)
    incl = (col1 <= row1) if d == 0 else (col1 >= row1)
    lw = lw_ref[...]
    cs = sum(_dot(incl.astype(bf16), part) for part in _split3(lw))
    tot = jnp.sum(lw, axis=0, keepdims=True)
    a = a_ref[...]
    kk = kk_ref[...]
    beta = kk * a
    kdir = k_ref[...] * (1.0 + (a - 1.0) * ka)
    p_inv = jnp.exp(-cs)
    p_rest = jnp.exp(tot - cs)
    return dict(
        mask=mask, p_all=jnp.exp(tot),
        kap=(kk * jnp.exp(cs - lw)).astype(bf16), r=(r_ref[...] * jnp.exp(cs)).astype(bf16),
        beta=(beta * p_inv).astype(bf16), kdir=(kdir * p_inv).astype(bf16),
        beta_e=(beta * p_rest).astype(bf16), kdir_e=(kdir * p_rest).astype(bf16),
        v=v_ref[...].astype(bf16))


def _bd(x):
    lane = lax.broadcasted_iota(jnp.int32, x.shape, 1)
    zero = jnp.zeros_like(x)
    return jnp.concatenate([jnp.where(lane < HEAD_DIM, x, zero), jnp.where(lane >= HEAD_DIM, x, zero)], axis=0)


def _rw_scan_body(rb_ref, flag_ref, seq_ref, *refs):
    nj = 2 * SCAN_SEQS
    in_refs = [refs[6 * j:6 * j + 6] for j in range(nj)]
    ka_ref, s0_ref = refs[6 * nj:6 * nj + 2]
    y_refs = refs[6 * nj + 2:7 * nj + 2]
    sf_ref, s_scr = refs[7 * nj + 2:]
    s = pl.program_id(0)
    cdim = CHUNK
    pw = 2 * HEAD_DIM
    n_pairs = N_RWKV_HEADS // 2
    chains = [(j, p) for j in range(nj) for p in range(n_pairs)]
    n = range(len(chains))
    flag = flag_ref[s]

    @pl.when((flag & 1) == 1)
    def _():
        z = jnp.zeros((HEAD_DIM, HEAD_DIM), f32)
        for j, p in chains:
            s_scr[j, p] = jnp.concatenate(
                [jnp.concatenate([s0_ref[j // 2, j % 2, 2 * p], z], axis=1),
                 jnp.concatenate([z, s0_ref[j // 2, j % 2, 2 * p + 1]], axis=1)], axis=0)

    ka = ka_ref[...]
    ops = [_rw_dir_operands(j % 2, *in_refs[j], ka) for j in range(nj)]

    def sl(name, i):
        j, p = chains[i]
        return ops[j][name][:, p * pw:(p + 1) * pw]

    s_old = [s_scr[j, p] for j, p in chains]
    s_b = [x.astype(bf16) for x in s_old]
    lhs = [jnp.concatenate([sl('kap', i), sl('r', i)], axis=0) for i in n]
    rhs = [jnp.concatenate([_bd(sl('beta', i)), _bd(sl('kdir', i))], axis=0) for i in n]
    mm = [jnp.where(ops[chains[i][0]]['mask'], _dot_nt(lhs[i], rhs[i]), 0.0).astype(bf16) for i in n]
    sk = [_dot_nt(lhs[i], s_b[i]) for i in n]
    av = [_dot(mm[i][:, 2 * cdim:], _bd(sl('v', i))) for i in n]
    x = [-mm[i][:cdim, :2 * cdim] for i in n]
    m_rb = [mm[i][cdim:, :2 * cdim] for i in n]
    uu = [sk[i][:cdim] + av[i][:cdim] for i in n]
    n_sq = int(math.log2(cdim))
    for it in range(n_sq):
        uu = [uu[i] + _dot(x[i], _bd(uu[i].astype(bf16))) for i in n]
        if it + 1 < n_sq:
            x = [_dot(x[i], _bd(x[i])).astype(bf16) for i in n]
    u_b = [(-uu[i]).astype(bf16) for i in n]
    y = [sk[i][cdim:] + _dot(m_rb[i], _bd(u_b[i])) + av[i][cdim:] for i in n]
    for j in range(nj):
        y_refs[j][...] = jnp.concatenate(y[j * n_pairs:(j + 1) * n_pairs], axis=1)
    row = lax.broadcasted_iota(jnp.int32, (pw, pw), 0)
    col = lax.broadcasted_iota(jnp.int32, (pw, pw), 1)
    same_head = (row // HEAD_DIM) == (col // HEAD_DIM)
    s_new = []
    for i in n:
        j, p = chains[i]
        upd = _dot_tn(jnp.concatenate([u_b[i], sl('v', i)], axis=0),
                      jnp.concatenate([sl('beta_e', i), sl('kdir_e', i)], axis=0))
        s_new.append(s_old[i] * ops[j]['p_all'][:, p * pw:(p + 1) * pw] + jnp.where(same_head, upd, 0.0))
        s_scr[j, p] = s_new[i]

    @pl.when((flag & 2) == 2)
    def _():
        for i in n:
            j, p = chains[i]
            sf_ref[j // 2, j % 2, 2 * p] = s_new[i][:HEAD_DIM, :HEAD_DIM]
            sf_ref[j // 2, j % 2, 2 * p + 1] = s_new[i][HEAD_DIM:, HEAD_DIM:]


def _rw_scan(u, kk, lw, a, k_a, s0, tabs, n_steps):
    c = RWKV_WIDTH
    t = u.shape[0]
    nseq = s0.shape[0]
    cb = COL_RKV // c
    nj = 2 * SCAN_SEQS

    def stream_specs(j):
        d = j % 2
        return [pl.BlockSpec((CHUNK, c), lambda s, rb, fl, sq: (rb[j, s], cb)),
                pl.BlockSpec((CHUNK, c), lambda s, rb, fl, sq: (rb[j, s], cb + 1)),
                pl.BlockSpec((CHUNK, c), lambda s, rb, fl, sq: (rb[j, s], cb + 2)),
                pl.BlockSpec((CHUNK, c), lambda s, rb, fl, sq: (rb[j, s], 0)),
                pl.BlockSpec((None, CHUNK, c), lambda s, rb, fl, sq: (d, rb[j, s], 0)),
                pl.BlockSpec((None, CHUNK, c), lambda s, rb, fl, sq: (d, rb[j, s], 0))]

    state_spec = pl.BlockSpec((SCAN_SEQS, 2, N_RWKV_HEADS, HEAD_DIM, HEAD_DIM),
                              lambda s, rb, fl, sq: (sq[s], 0, 0, 0, 0))
    in_specs = []
    operands = []
    for j in range(nj):
        in_specs += stream_specs(j)
        operands += [u, u, u, kk, lw, a]
    grid_spec = pltpu.PrefetchScalarGridSpec(
        num_scalar_prefetch=3,
        grid=(n_steps,),
        in_specs=in_specs + [pl.BlockSpec((1, c), lambda s, rb, fl, sq: (0, 0)), state_spec],
        out_specs=[pl.BlockSpec((CHUNK, c), lambda s, rb, fl, sq, j=j: (rb[j, s], 0)) for j in range(nj)]
        + [state_spec],
        scratch_shapes=[pltpu.VMEM((nj, N_RWKV_HEADS // 2, 2 * HEAD_DIM, 2 * HEAD_DIM), f32)],
    )
    outs = pl.pallas_call(
        _rw_scan_body,
        grid_spec=grid_spec,
        out_shape=[jax.ShapeDtypeStruct((t, c), f32)] * nj
        + [jax.ShapeDtypeStruct((nseq, 2, N_RWKV_HEADS, HEAD_DIM, HEAD_DIM), f32)],
        compiler_params=_cp(("arbitrary",)),
        name="rw_scan",
    )(*tabs, *operands, k_a, s0)
    return outs[:nj], outs[nj]


def _scan_tables(rows):
    rb = [[] for _ in range(2 * SCAN_SEQS)]
    flags, group = [], []
    base = 0
    gidx = 0
    for b, l in ((rows.b_ctx, rows.l_ctx), (rows.b_lat, rows.l_lat)):
        nc = l // CHUNK
        assert b % SCAN_SEQS == 0
        for q in range(b // SCAN_SEQS):
            for c in range(nc):
                for m in range(SCAN_SEQS):
                    seq_base = base + (q * SCAN_SEQS + m) * nc
                    rb[2 * m].append(seq_base + c)
                    rb[2 * m + 1].append(seq_base + nc - 1 - c)
                flags.append((1 if c == 0 else 0) | (2 if c == nc - 1 else 0))
                group.append(gidx)
            gidx += 1
        base += b * nc
    return (jnp.asarray(np.array(rb, np.int32)), jnp.asarray(np.array(flags, np.int32)),
            jnp.asarray(np.array(group, np.int32))), len(flags)


def _member_of_block(i, rows, tm):
    nctx = rows.t_ctx // tm
    in_ctx = i < nctx
    per = jnp.where(in_ctx, rows.l_ctx // tm, rows.l_lat // tm)
    base = jnp.where(in_ctx, 0, nctx)
    seq = (i - base) // per
    start = base + seq * per
    return seq % SCAN_SEQS, jnp.where(start > 0, start - 1, per)


def _rw_mix(y_refs, r_ref, k_ref, v_ref, g_ref, rk_ref, gw_ref, gb_ref, hm_ref, hs_ref, rows, tm):
    mem, _ = _member_of_block(pl.program_id(0), rows, tm)
    y = y_refs[0][...] + y_refs[1][...]
    for m in range(1, SCAN_SEQS):
        y = jnp.where(mem == m, y_refs[2 * m][...] + y_refs[2 * m + 1][...], y)
    mu = _dot2(y, hm_ref[...])
    yc = y - mu
    var = _dot2(yc * yc, hm_ref[...])
    yn = yc * lax.rsqrt(var + RWKV_GN_EPS) * gw_ref[...] + gb_ref[...]
    bonus = _dot2(r_ref[...] * k_ref[...] * rk_ref[...], hs_ref[...]) * v_ref[...]
    return (yn + bonus) * g_ref[...]


def _outproj_body(*refs, nx, rows, tm):
    nctx = rows.t_ctx // tm
    ny = 2 * SCAN_SEQS
    att_refs, hy_refs = refs[0:2], refs[2:4]
    y_refs = refs[4:4 + ny]
    rw_refs = refs[4 + ny:13 + ny]
    w_ref = refs[13 + ny]
    x_refs = refs[14 + ny:14 + ny + nx]
    g_ref, nw_ref, o_ref = refs[14 + ny + nx:]
    rw = _rw_mix(y_refs, *rw_refs, rows, tm)
    o = (_dot(_pair_load(att_refs, nctx).astype(bf16), w_ref[0:ATT_WIDTH, :])
         + _dot(_pair_load(hy_refs, nctx).astype(bf16), w_ref[ATT_WIDTH:ATT_WIDTH + HY_CH, :])
         + _dot(rw.astype(bf16), w_ref[ATT_WIDTH + HY_CH:, :]))
    y = o * lax.rsqrt(jnp.mean(o * o, axis=-1, keepdims=True) + RMS_EPS) * nw_ref[...]
    o_ref[...] = _pair_load(x_refs, nctx) + g_ref[...] * y


def _out_proj(atts, hys, ys, u, g, rw_params, w_out, layer, xs, mod, nw, rows, tm=256):
    t = rows.t
    _, mix, d = w_out.shape
    c = RWKV_WIDTH
    cb = COL_RKV // c
    midx = rows.mod_index(tm)

    def y_spec(m):
        def index(i):
            mem, other = _member_of_block(i, rows, tm)
            return jnp.where(mem == m, i, other), 0
        return pl.BlockSpec((tm, c), index)

    body = functools.partial(_outproj_body, nx=len(xs), rows=rows, tm=tm)
    return pl.pallas_call(
        body,
        grid=(t // tm,),
        in_specs=_pair_specs(atts, rows, tm) + _pair_specs(hys, rows, tm)
        + [y_spec(j // 2) for j in range(2 * SCAN_SEQS)] + [
            pl.BlockSpec((tm, c), lambda i: (i, cb)),
            pl.BlockSpec((tm, c), lambda i: (i, cb + 1)),
            pl.BlockSpec((tm, c), lambda i: (i, cb + 2)),
            pl.BlockSpec((tm, c), lambda i: (i, 0))] + [_full(p) for p in rw_params] + [
            pl.BlockSpec((None, mix, d), lambda i: (layer, 0, 0))] + _pair_specs(xs, rows, tm) + [
            pl.BlockSpec((None, 1, d), lambda i: (midx(i), 0, 2)),
            pl.BlockSpec((None, 1, d), lambda i: (1, 0, 0))],
        out_specs=pl.BlockSpec((tm, d), lambda i: (i, 0)),
        out_shape=jax.ShapeDtypeStruct((t, d), f32),
        compiler_params=_cp(("arbitrary",)),
        name="out_proj",
    )(*atts, *hys, *ys, u, u, u, g, *rw_params, w_out, *xs, mod, nw)


def _mlp_body(x_ref, sh_ref, sc_ref, g_ref, nw2_ref, nw3_ref, w1_ref, w2_ref, *rest, nctx):
    o_refs, (h_scr, acc_scr) = rest[:-2], rest[-2:]
    i = pl.program_id(0)
    j = pl.program_id(1)

    @pl.when(j == 0)
    def _():
        h_scr[...] = _normmod(x_ref[...], nw2_ref[...], sc_ref[...], sh_ref[...]).astype(bf16)
        acc_scr[...] = jnp.zeros_like(acc_scr)

    a = jnp.maximum(_dot(h_scr[...], w1_ref[...]), 0.0)
    acc_scr[...] += _dot((a * a).astype(bf16), w2_ref[...])

    def result():
        f = acc_scr[...]
        y = f * lax.rsqrt(jnp.mean(f * f, axis=-1, keepdims=True) + RMS_EPS) * nw3_ref[...]
        return x_ref[...] + g_ref[...] * y

    last = j == pl.num_programs(1) - 1
    if len(o_refs) == 1:
        @pl.when(last)
        def _():
            o_refs[0][...] = result()
    else:
        @pl.when(last & (i < nctx))
        def _():
            o_refs[0][...] = result()

        @pl.when(last & (i >= nctx))
        def _():
            o_refs[1][...] = result()


def _mlp(x, mod, nw, w1, w2, layer, rows, split_out, tm=512, tf=1024):
    t, d = x.shape
    dff = w1.shape[2]
    midx = rows.mod_index(tm)
    nctx = rows.t_ctx // tm
    if split_out:
        out_specs = [pl.BlockSpec((tm, d), lambda i, j: (jnp.minimum(i, nctx - 1), 0)),
                     pl.BlockSpec((tm, d), lambda i, j: (jnp.maximum(i - nctx, 0), 0))]
        out_shape = [jax.ShapeDtypeStruct((rows.t_ctx, d), f32), jax.ShapeDtypeStruct((rows.t_lat, d), f32)]
    else:
        out_specs = pl.BlockSpec((tm, d), lambda i, j: (i, 0))
        out_shape = jax.ShapeDtypeStruct((t, d), f32)
    return pl.pallas_call(
        functools.partial(_mlp_body, nctx=nctx),
        grid=(t // tm, dff // tf),
        in_specs=[pl.BlockSpec((tm, d), lambda i, j: (i, 0)),
                  pl.BlockSpec((None, 1, d), lambda i, j: (midx(i), 0, 3)),
                  pl.BlockSpec((None, 1, d), lambda i, j: (midx(i), 0, 4)),
                  pl.BlockSpec((None, 1, d), lambda i, j: (midx(i), 0, 5)),
                  pl.BlockSpec((None, 1, d), lambda i, j: (2, 0, 0)),
                  pl.BlockSpec((None, 1, d), lambda i, j: (3, 0, 0)),
                  pl.BlockSpec((None, d, tf), lambda i, j: (layer, 0, j)),
                  pl.BlockSpec((None, tf, d), lambda i, j: (layer, j, 0))],
        out_specs=out_specs,
        out_shape=out_shape,
        scratch_shapes=[pltpu.VMEM((tm, d), bf16), pltpu.VMEM((tm, d), f32)],
        compiler_params=_cp(("arbitrary", "arbitrary")),
        name="mlp",
    )(x, mod, mod, mod, nw, nw, w1, w2)


def _lora_weight(w_up, a_up, g_up):
    c = RWKV_WIDTH

    def two_dir(up):
        r = up.shape[1]
        w = jnp.zeros((LORA_PART, 2 * c), f32)
        return w.at[0:r, 0:c].set(up[0]).at[r:2 * r, c:2 * c].set(up[1]).astype(bf16)
    return two_dir(w_up), two_dir(a_up), g_up.astype(bf16)


def _regroup_in_cols(w):
    kv0 = ATT_WIDTH
    hy0 = kv0 + 2 * ATT_KV
    rw0 = hy0 + 3 * HY_CH
    lo0 = rw0 + 3 * RWKV_WIDTH
    a0 = lo0 + 2 * DECAY_LORA
    g0 = a0 + 2 * AAA_LORA

    def padded(part, width):
        return jnp.pad(part, [(0, 0)] * (part.ndim - 1) + [(0, width - part.shape[-1])])
    return jnp.concatenate([w[..., :kv0], w[..., rw0:lo0], w[..., hy0:rw0], w[..., kv0:hy0],
                            padded(w[..., lo0:a0], LORA_PART), padded(w[..., a0:g0], LORA_PART),
                            padded(w[..., g0:], 2 * LORA_PART)], axis=-1)


def _conv_taps(hy_short_w, rw_short_w):
    def ident(n):
        return jnp.zeros((3, n), f32).at[1].set(1.0)
    return jnp.concatenate([ident(ATT_WIDTH), rw_short_w, hy_short_w, ident(2 * ATT_KV + LORA_IN)], axis=1)


def kernel(x_prompt, x_sample, cache_k, cache_v, state_rwkv, c, c_ctx, ada_w, ada_b, norm_w, w_in, w_out,
           attn_sink, hy_short_w, hy_f1, hy_b1, hy_f2, hy_b2, hy_f3, hy_decay, hy_skip, rw_short_w, rw_w0,
           rw_w_up, rw_a0, rw_a_up, rw_g_up, rw_k_k, rw_k_a, rw_r_k, rw_gn_w, rw_gn_b, mlp_w1, mlp_w2):
    b_ctx, l_ctx, d = x_prompt.shape
    b_lat, l_lat, _ = x_sample.shape
    depth = ada_w.shape[0]
    grid_w = 64
    rows = _Rows(b_ctx, l_ctx, b_lat, l_lat)

    xs = (x_prompt.reshape(rows.t_ctx, d), x_sample.reshape(rows.t_lat, d))
    cond16 = jnp.zeros((16, d), f32).at[0].set(c_ctx).at[1:1 + b_lat].set(c)
    mod_all = _modulation(cond16, ada_w, ada_b[:, None, :])

    w_in_b = _regroup_in_cols(w_in).astype(bf16)
    w_out_b = w_out.astype(bf16)
    w1_b = mlp_w1.astype(bf16)
    w2_b = mlp_w2.astype(bf16)

    cos_t, sin_t = _rope_tables(l_lat, grid_w)
    hsum = _head_sum_matrix()
    hmean = _head_sum_matrix(1.0 / HEAD_DIM)
    scan_tabs, n_steps = _scan_tables(rows)
    dft = {}
    for l in (l_ctx, l_lat):
        cm, sm, cmt, smt = _dft_tables(l)
        feat, t01 = _hy_feat(l)
        dft[l] = dict(cm=cm, sm=sm, cmb=cm.astype(bf16), smb=sm.astype(bf16),
                      cmtb=cmt.astype(bf16), smtb=smt.astype(bf16), feat=feat, t01=t01)

    new_k, new_v, new_s = [], [], []
    for l in range(depth):
        mod = mod_all[l].reshape(16, 1, N_MOD * d)
        nw = norm_w[l].reshape(4, 1, d)
        u = _in_proj(xs, mod, nw, w_in_b, l, _conv_taps(hy_short_w[l], rw_short_w[l]), rows)

        new_k.append(u[:rows.t_ctx, COL_KV:COL_KV + ATT_KV].reshape(b_ctx, l_ctx, N_KV_HEADS, HEAD_DIM))
        new_v.append(u[:rows.t_ctx, COL_KV + ATT_KV:COL_KV + 2 * ATT_KV].reshape(b_ctx, l_ctx, N_KV_HEADS, HEAD_DIM))

        att_c = _attn_ctx(u, attn_sink[l], rows)
        att_l = _attn_lat(u, cache_k[:, l].reshape(b_lat, -1, ATT_KV), cache_v[:, l].reshape(b_lat, -1, ATT_KV),
                          attn_sink[l], cos_t, sin_t, rows)

        f1p = jnp.pad(hy_f1[l], ((0, 128 - HY_EMB), (0, 0)))
        hys = []
        for (bb, ll, rb0) in ((b_ctx, l_ctx, 0), (b_lat, l_lat, rows.t_ctx // l_lat)):
            tb = dft[ll]
            hr, hi = _hy_filter(ll, tb['feat'], tb['t01'], f1p, hy_b1[l][None], hy_f2[l], hy_b2[l][None],
                                hy_f3[l], hy_decay[l][None], tb['cm'], tb['sm'])
            hys.append(_hy_conv(u, hr, hi, hy_skip[l], tb['cmb'], tb['smb'], tb['cmtb'], tb['smtb'],
                                bb, ll, rb0))

        w_lora = _lora_weight(rw_w_up[l], rw_a_up[l], rw_g_up[l])
        kk, lw, a, g = _rw_prep(u, w_lora, rw_w0[l], rw_a0[l], rw_k_k[l][None], hsum)
        s0 = jnp.concatenate([jnp.zeros((b_ctx, 2, N_RWKV_HEADS, HEAD_DIM, HEAD_DIM), f32),
                              state_rwkv[:, l]], axis=0)
        ys, sfin = _rw_scan(u, kk, lw, a, rw_k_a[l][None], s0, scan_tabs, n_steps)
        new_s.append(sfin[:b_ctx])
        rw_params = (rw_r_k[l].reshape(1, RWKV_WIDTH), rw_gn_w[l][None], rw_gn_b[l][None], hmean, hsum)

        x = _out_proj((att_c, att_l), tuple(hys), ys, u, g, rw_params, w_out_b, l, xs, mod, nw, rows)
        x = _mlp(x, mod, nw, w1_b, w2_b, l, rows, split_out=(l == depth - 1))
        xs = (x,)

    y_p, y_s = x
    return (y_p.reshape(b_ctx, l_ctx, d), y_s.reshape(b_lat, l_lat, d),
            jnp.stack(new_k, axis=1), jnp.stack(new_v, axis=1), jnp.stack(new_s, axis=1))
```

```python
import functools
import math

import numpy as np
import jax
import jax.numpy as jnp
from jax import lax
from jax.experimental import pallas as pl
from jax.experimental.pallas import tpu as pltpu

f32 = jnp.float32
bf16 = jnp.bfloat16

HEAD_DIM = 64
N_ATT_HEADS = 12
N_KV_HEADS = 4
ATT_GROUP = 3
ATT_WIDTH = 768
ATT_KV = 256
WINDOW = 128
ROPE_THETA = 10000.0
NEG_INF = -1e30
HY_CH = 512
HY_BANDS = 16
HY_EMB = 33
HY_FFN = 64
HY_MOD_SHIFT = 0.05
RWKV_WIDTH = 768
N_RWKV_HEADS = 12
DECAY_LORA = 96
AAA_LORA = 96
GATE_LORA = 256
RWKV_GN_EPS = 64e-5
N_MOD = 6
RMS_EPS = 1e-6
CHUNK = 64
SCAN_SEQS = 2
CONV_ROWS = 128
LORA_PART = 256
LORA_IN = 4 * LORA_PART
COL_Q = 0
COL_RKV = COL_Q + ATT_WIDTH
COL_HY = COL_RKV + 3 * RWKV_WIDTH
COL_KV = COL_HY + 3 * HY_CH
COL_LORA = COL_KV + 2 * ATT_KV
IN_COLS = COL_LORA + LORA_IN
HI = lax.Precision.HIGHEST
VMEM_LIMIT = 56 * 1024 * 1024


def _cp(sem, vmem=VMEM_LIMIT):
    return pltpu.CompilerParams(dimension_semantics=sem, vmem_limit_bytes=vmem)


def _dot(a, b):
    return jnp.dot(a, b, preferred_element_type=f32)


def _dot_nt(a, b):
    return lax.dot_general(a, b, (((1,), (1,)), ((), ())), preferred_element_type=f32)


def _dot_tn(a, b):
    return lax.dot_general(a, b, (((0,), (0,)), ((), ())), preferred_element_type=f32)


def _dot_hi(a, b):
    return jnp.dot(a, b, preferred_element_type=f32, precision=HI)


def _full(a):
    return pl.BlockSpec(a.shape, lambda *_: (0,) * a.ndim)


def _mod_body(c_ref, w_ref, b_ref, o_ref):
    c = c_ref[...]
    s = (c * jax.nn.sigmoid(c)).astype(bf16)
    o_ref[...] = _dot(s, w_ref[...].astype(bf16)) + b_ref[...]


def _modulation(cond16, ada_w, ada_b):
    depth, d, n = ada_w.shape
    tn = 1024
    return pl.pallas_call(
        _mod_body,
        grid=(depth, n // tn),
        in_specs=[pl.BlockSpec((16, d), lambda l, j: (0, 0)),
                  pl.BlockSpec((None, d, tn), lambda l, j: (l, 0, j)),
                  pl.BlockSpec((None, 1, tn), lambda l, j: (l, 0, j))],
        out_specs=pl.BlockSpec((None, 16, tn), lambda l, j: (l, 0, j)),
        out_shape=jax.ShapeDtypeStruct((depth, 16, n), f32),
        compiler_params=_cp(("arbitrary", "arbitrary")),
        name="modulation",
    )(cond16, ada_w, ada_b)


def _normmod(x, nw, sc, sh):
    y = x * lax.rsqrt(jnp.mean(x * x, axis=-1, keepdims=True) + RMS_EPS) * nw
    return y * (1.0 + sc) + sh


class _Rows:
    def __init__(self, b_ctx, l_ctx, b_lat, l_lat):
        self.b_ctx, self.l_ctx, self.b_lat, self.l_lat = b_ctx, l_ctx, b_lat, l_lat
        self.t_ctx = b_ctx * l_ctx
        self.t_lat = b_lat * l_lat
        self.t = self.t_ctx + self.t_lat

    def mod_index(self, tm):
        nctx = self.t_ctx // tm
        per = self.l_lat // tm

        def f(i):
            return jnp.where(i < nctx, 0, 1 + (i - nctx) // per)
        return f


def _pair_specs(arrs, rows, tm, single_buffer=False):
    w = arrs[0].shape[1]
    if len(arrs) == 1:
        return [pl.BlockSpec((tm, w), lambda i, *_: (i, 0))]
    nctx = rows.t_ctx // tm
    mode = dict(pipeline_mode=pl.Buffered(1)) if single_buffer else {}
    return [pl.BlockSpec((tm, w), lambda i, *_: (jnp.minimum(i, nctx - 1), 0), **mode),
            pl.BlockSpec((tm, w), lambda i, *_: (jnp.maximum(i - nctx, 0), 0), **mode)]


def _pair_load(refs, nctx):
    if len(refs) == 1:
        return refs[0][...]
    return jnp.where(pl.program_id(0) < nctx, refs[0][...], refs[1][...])


def _inproj_body(*refs, nx, nctx, lmask_ctx, lmask_lat):
    x_refs = refs[:nx]
    sh_ref, sc_ref, nw_ref, w_ref, wc_ref, o_ref, h_scr, u_scr = refs[nx:]
    i = pl.program_id(0)

    @pl.when(pl.program_id(1) == 0)
    def _():
        h_scr[...] = _normmod(_pair_load(x_refs, nctx), nw_ref[...], sc_ref[...], sh_ref[...]).astype(bf16)

    u_scr[...] = _dot(h_scr[...], w_ref[...])
    tm = u_scr.shape[0]
    ch = CONV_ROWS
    lmask = jnp.where(i < nctx, lmask_ctx, lmask_lat)
    row = lax.broadcasted_iota(jnp.int32, (ch, 1), 0)
    wc = wc_ref[...]
    for r0 in range(0, tm, ch):
        x = u_scr[r0:r0 + ch, :]
        at_start = (r0 & lmask) == 0
        at_end = ((r0 + ch - 1) & lmask) == lmask
        prev = jnp.where(at_start, 0.0, u_scr[max(r0 - 1, 0):max(r0 - 1, 0) + 1, :])
        nxt = jnp.where(at_end, 0.0, u_scr[min(r0 + ch, tm - 1):min(r0 + ch, tm - 1) + 1, :])
        xm = jnp.where(row == 0, prev, pltpu.roll(x, 1, 0))
        xp = jnp.where(row == ch - 1, nxt, pltpu.roll(x, ch - 1, 0))
        o_ref[r0:r0 + ch, :] = xm * wc[0:1, :] + x * wc[1:2, :] + xp * wc[2:3, :]


def _in_proj(xs, mod, nw, w_in, layer, wconv, rows, tm=1024, tn=1024):
    t = rows.t
    _, d, n = w_in.shape
    for l in (rows.l_ctx, rows.l_lat):
        assert l & (l - 1) == 0 and tm % l == 0 and l % CONV_ROWS == 0
    midx = rows.mod_index(tm)
    body = functools.partial(_inproj_body, nx=len(xs), nctx=rows.t_ctx // tm,
                             lmask_ctx=rows.l_ctx - 1, lmask_lat=rows.l_lat - 1)
    return pl.pallas_call(
        body,
        grid=(t // tm, n // tn),
        in_specs=_pair_specs(xs, rows, tm, single_buffer=True) + [
            pl.BlockSpec((None, 1, d), lambda i, j: (midx(i), 0, 0)),
            pl.BlockSpec((None, 1, d), lambda i, j: (midx(i), 0, 1)),
            pl.BlockSpec((None, 1, d), lambda i, j: (0, 0, 0)),
            pl.BlockSpec((None, d, tn), lambda i, j: (layer, 0, j)),
            pl.BlockSpec((3, tn), lambda i, j: (0, j))],
        out_specs=pl.BlockSpec((tm, tn), lambda i, j: (i, j)),
        out_shape=jax.ShapeDtypeStruct((t, n), f32),
        scratch_shapes=[pltpu.VMEM((tm, d), bf16), pltpu.VMEM((tm, tn), f32)],
        compiler_params=_cp(("arbitrary", "arbitrary")),
        name="in_proj",
    )(*xs, mod, mod, nw, w_in, wconv)


def _attn_ctx_body(sink_ref, q_ref, k_ref, v_ref, o_ref):
    l = q_ref.shape[0]
    scale = HEAD_DIM ** -0.5
    kvh = range(N_KV_HEADS)
    qb = (q_ref[...] * scale).astype(bf16)
    kb = k_ref[...].astype(bf16)
    vb = v_ref[...].astype(bf16)
    hcols = [slice(n * HEAD_DIM, (n + 1) * HEAD_DIM) for n in kvh]
    qs = [jnp.concatenate([qb[:, (n * ATT_GROUP + g) * HEAD_DIM:(n * ATT_GROUP + g + 1) * HEAD_DIM]
                           for g in range(ATT_GROUP)], axis=0) for n in kvh]
    sink_col = [jnp.concatenate([jnp.full((l, 1), sink_ref[n * ATT_GROUP + g], f32) for g in range(ATT_GROUP)],
                                axis=0) for n in kvh]
    s = [_dot_nt(qs[n], kb[:, hcols[n]]) for n in kvh]
    m = [jnp.maximum(jnp.max(s[n], axis=-1, keepdims=True), sink_col[n]) for n in kvh]
    p = [jnp.exp(s[n] - m[n]) for n in kvh]
    den = [jnp.exp(sink_col[n] - m[n]) + jnp.sum(p[n], axis=-1, keepdims=True) for n in kvh]
    o = [_dot(p[n].astype(bf16), vb[:, hcols[n]]) / den[n] for n in kvh]
    for n in kvh:
        for g in range(ATT_GROUP):
            h = n * ATT_GROUP + g
            o_ref[:, h * HEAD_DIM:(h + 1) * HEAD_DIM] = o[n][g * l:(g + 1) * l, :]


def _attn_ctx(u, sink, rows):
    b, l = rows.b_ctx, rows.l_ctx
    return pl.pallas_call(
        _attn_ctx_body,
        grid=(b,),
        in_specs=[pl.BlockSpec(memory_space=pltpu.SMEM),
                  pl.BlockSpec((l, ATT_WIDTH), lambda i: (i, COL_Q // ATT_WIDTH)),
                  pl.BlockSpec((l, ATT_KV), lambda i: (i, COL_KV // ATT_KV)),
                  pl.BlockSpec((l, ATT_KV), lambda i: (i, COL_KV // ATT_KV + 1))],
        out_specs=pl.BlockSpec((l, ATT_WIDTH), lambda i: (i, 0)),
        out_shape=jax.ShapeDtypeStruct((b * l, ATT_WIDTH), f32),
        compiler_params=_cp(("arbitrary",)),
        name="attn_ctx",
    )(sink, u, u, u)


def _rope(x, cos, sin_signed):
    w = x.shape[1]
    lane = lax.broadcasted_iota(jnp.int32, (1, w), 1)
    swapped = jnp.where((lane % 32) < 16, pltpu.roll(x, w - 16, 1), pltpu.roll(x, 16, 1))
    return x * cos + swapped * sin_signed


def _attn_lat_body(sink_ref, q_ref, k_ref, v_ref, kc_ref, vc_ref, cos_ref, sin_ref, o_ref,
                   q_scr, k_scr, v_scr, kc_scr, vc_scr):
    l = q_ref.shape[0]
    blk = WINDOW
    nb = l // blk
    scale = HEAD_DIM ** -0.5
    cos = cos_ref[...]
    sin = sin_ref[...]
    q_scr[...] = (_rope(q_ref[...], cos, sin) * scale).astype(bf16)
    zpad = jnp.zeros((blk, ATT_KV), bf16)
    k_scr[0:blk, :] = zpad
    k_scr[blk + l:, :] = zpad
    v_scr[0:blk, :] = zpad
    v_scr[blk + l:, :] = zpad
    k_scr[blk:blk + l, :] = _rope(k_ref[...], cos[:, :ATT_KV], sin[:, :ATT_KV]).astype(bf16)
    v_scr[blk:blk + l, :] = v_ref[...].astype(bf16)
    r = lax.broadcasted_iota(jnp.int32, (ATT_GROUP * blk, 3 * blk), 0) % blk
    c = lax.broadcasted_iota(jnp.int32, (ATT_GROUP * blk, 3 * blk), 1)
    band = (c - r >= 0) & (c - r <= 2 * WINDOW)
    kvh = range(N_KV_HEADS)
    hcols = [slice(n * HEAD_DIM, (n + 1) * HEAD_DIM) for n in kvh]
    kc_scr[...] = kc_ref[...].astype(bf16)
    vc_scr[...] = vc_ref[...].astype(bf16)
    sink_col = [jnp.concatenate([jnp.full((blk, 1), sink_ref[n * ATT_GROUP + g], f32) for g in range(ATT_GROUP)],
                                axis=0) for n in kvh]
    for i in range(nb):
        kpos = (i - 1) * blk + c
        mask = band & (kpos >= 0) & (kpos < l)
        rows_q = slice(i * blk, (i + 1) * blk)
        rows_k = slice(i * blk, (i + 3) * blk)
        qs = [jnp.concatenate([q_scr[rows_q, (n * ATT_GROUP + g) * HEAD_DIM:(n * ATT_GROUP + g + 1) * HEAD_DIM]
                               for g in range(ATT_GROUP)], axis=0) for n in kvh]
        s_loc = [jnp.where(mask, _dot_nt(qs[n], k_scr[rows_k, hcols[n]]), NEG_INF) for n in kvh]
        s_ctx = [_dot_nt(qs[n], kc_scr[:, hcols[n]]) for n in kvh]
        m = [jnp.maximum(jnp.maximum(jnp.max(s_loc[n], axis=-1, keepdims=True),
                                     jnp.max(s_ctx[n], axis=-1, keepdims=True)), sink_col[n]) for n in kvh]
        p_loc = [jnp.exp(s_loc[n] - m[n]) for n in kvh]
        p_ctx = [jnp.exp(s_ctx[n] - m[n]) for n in kvh]
        den = [jnp.exp(sink_col[n] - m[n]) + jnp.sum(p_loc[n], axis=-1, keepdims=True)
               + jnp.sum(p_ctx[n], axis=-1, keepdims=True) for n in kvh]
        o = [(_dot(p_loc[n].astype(bf16), v_scr[rows_k, hcols[n]])
              + _dot(p_ctx[n].astype(bf16), vc_scr[:, hcols[n]])) / den[n] for n in kvh]
        for n in kvh:
            for g in range(ATT_GROUP):
                h = n * ATT_GROUP + g
                o_ref[rows_q, h * HEAD_DIM:(h + 1) * HEAD_DIM] = o[n][g * blk:(g + 1) * blk, :]


def _attn_lat(u, kc, vc, sink, cos, sin, rows):
    b, l = rows.b_lat, rows.l_lat
    assert rows.t_ctx % l == 0
    rb0 = rows.t_ctx // l
    lc = kc.shape[1]
    return pl.pallas_call(
        _attn_lat_body,
        grid=(b,),
        in_specs=[pl.BlockSpec(memory_space=pltpu.SMEM),
                  pl.BlockSpec((l, ATT_WIDTH), lambda i: (rb0 + i, COL_Q // ATT_WIDTH)),
                  pl.BlockSpec((l, ATT_KV), lambda i: (rb0 + i, COL_KV // ATT_KV)),
                  pl.BlockSpec((l, ATT_KV), lambda i: (rb0 + i, COL_KV // ATT_KV + 1)),
                  pl.BlockSpec((None, lc, ATT_KV), lambda i: (i, 0, 0)),
                  pl.BlockSpec((None, lc, ATT_KV), lambda i: (i, 0, 0)),
                  pl.BlockSpec((l, ATT_WIDTH), lambda i: (0, 0)),
                  pl.BlockSpec((l, ATT_WIDTH), lambda i: (0, 0))],
        out_specs=pl.BlockSpec((l, ATT_WIDTH), lambda i: (i, 0)),
        out_shape=jax.ShapeDtypeStruct((b * l, ATT_WIDTH), f32),
        scratch_shapes=[pltpu.VMEM((l, ATT_WIDTH), bf16),
                        pltpu.VMEM((l + 2 * WINDOW, ATT_KV), bf16),
                        pltpu.VMEM((l + 2 * WINDOW, ATT_KV), bf16),
                        pltpu.VMEM((lc, ATT_KV), bf16),
                        pltpu.VMEM((lc, ATT_KV), bf16)],
        compiler_params=_cp(("arbitrary",)),
        name="attn_lat",
    )(sink, u, u, u, kc, vc, cos, sin)


def _rope_tables(l, grid_w):
    half = HEAD_DIM // 2
    pos = jnp.arange(l)
    rowp = (pos // grid_w).astype(f32)
    colp = (pos % grid_w).astype(f32)
    freqs = ROPE_THETA ** (-jnp.arange(0, half, 2, dtype=f32) / half)
    ar = rowp[:, None] * freqs[None, :]
    ac = colp[:, None] * freqs[None, :]
    cos = jnp.concatenate([jnp.cos(ar), jnp.cos(ar), jnp.cos(ac), jnp.cos(ac)], axis=-1)
    sin = jnp.concatenate([-jnp.sin(ar), jnp.sin(ar), -jnp.sin(ac), jnp.sin(ac)], axis=-1)
    return jnp.tile(cos, (1, N_ATT_HEADS)), jnp.tile(sin, (1, N_ATT_HEADS))


def _dft_tables(l):
    f = 32
    k = jnp.arange(l, dtype=jnp.int32)[:, None]

    def narrow(svals):
        m = ((2 * k + 1) * svals[None, :]) % (4 * l)
        ang = m.astype(f32) * (math.pi / (2 * l))
        return jnp.cos(ang), jnp.sin(ang)

    c1, s1 = narrow(jnp.arange(l // f, dtype=jnp.int32) * f)
    c0, s0 = narrow(jnp.arange(f, dtype=jnp.int32))
    cm = (c1[:, :, None] * c0[:, None, :] - s1[:, :, None] * s0[:, None, :]).reshape(l, l)
    sm = (s1[:, :, None] * c0[:, None, :] + c1[:, :, None] * s0[:, None, :]).reshape(l, l)
    c1t, s1t, c0t, s0t = c1.T, s1.T, c0.T, s0.T
    cmt = (c1t[:, None, :] * c0t[None, :, :] - s1t[:, None, :] * s0t[None, :, :]).reshape(l, l)
    smt = (s1t[:, None, :] * c0t[None, :, :] + c1t[:, None, :] * s0t[None, :, :]).reshape(l, l)
    return cm, sm, cmt, smt


def _hy_feat(l):
    t = jnp.arange(l, dtype=f32)
    t01 = (t / max(l - 1, 1))[:, None]
    bands = jnp.linspace(1e-4, HY_BANDS - 1, HY_BANDS, dtype=f32)
    ang = (2.0 * math.pi / l) * t[:, None] * bands[None, :]
    feat = jnp.concatenate([t01, jnp.cos(ang), -jnp.sin(ang)], axis=-1)
    return jnp.pad(feat, ((0, 0), (0, 128 - HY_EMB))), t01


def _hy_filter_body(feat_ref, t01_ref, f1_ref, b1_ref, f2_ref, b2_ref, f3_ref, dec_ref, cm_ref, sm_ref,
                    hr_ref, hi_ref):
    h = jnp.sin(_dot_hi(feat_ref[...], f1_ref[...]) + b1_ref[...])
    h = jnp.sin(_dot_hi(h, f2_ref[...]) + b2_ref[...])
    h = _dot_hi(h, f3_ref[...])
    h = h * (jnp.exp(-t01_ref[...] * jnp.abs(dec_ref[...])) + HY_MOD_SHIFT)
    l = h.shape[0]
    fwd = h[:, :HY_CH]
    row = lax.broadcasted_iota(jnp.int32, (l, 1), 0)
    bwd = jnp.where(row == 0, 0.0, h[:, HY_CH:])
    hr_ref[...] = _dot_hi(cm_ref[...], fwd + bwd)
    hi_ref[...] = _dot_hi(sm_ref[...], bwd - fwd)


def _hy_filter(l, feat, t01, f1p, b1, f2, b2, f3, dec, cm, sm):
    args = (feat, t01, f1p, b1, f2, b2, f3, dec, cm, sm)
    return pl.pallas_call(
        _hy_filter_body,
        grid=(1,),
        in_specs=[_full(a) for a in args],
        out_specs=[pl.BlockSpec((l, HY_CH), lambda i: (0, 0))] * 2,
        out_shape=[jax.ShapeDtypeStruct((l, HY_CH), f32)] * 2,
        compiler_params=_cp(("arbitrary",)),
        name="hy_filter",
    )(*args)


def _hy_conv_body(x0_ref, x1_ref, v_ref, hr_ref, hi_ref, skip_ref, cm_ref, sm_ref, cmt_ref, smt_ref, o_ref):
    l = v_ref.shape[0]
    z = v_ref[...] * x1_ref[...]
    zb = z.astype(bf16)
    zr = _dot(cm_ref[...], zb)
    zs = _dot(sm_ref[...], zb)
    hr = hr_ref[...]
    hi = hi_ref[...]
    yr = (zr * hr + zs * hi).astype(bf16)
    yi = (zr * hi - zs * hr).astype(bf16)
    y = (_dot(cmt_ref[...], yr) - _dot(smt_ref[...], yi)) * (1.0 / l)
    o_ref[...] = (y + skip_ref[...] * z) * x0_ref[...]


def _hy_conv(u, hr, hi, skip, cm, sm, cmt, smt, b, l, rb0):
    cb = COL_HY // HY_CH
    return pl.pallas_call(
        _hy_conv_body,
        grid=(b,),
        in_specs=[pl.BlockSpec((l, HY_CH), lambda i: (rb0 + i, cb)),
                  pl.BlockSpec((l, HY_CH), lambda i: (rb0 + i, cb + 1)),
                  pl.BlockSpec((l, HY_CH), lambda i: (rb0 + i, cb + 2)),
                  _full(hr), _full(hi), _full(skip), _full(cm), _full(sm), _full(cmt), _full(smt)],
        out_specs=pl.BlockSpec((l, HY_CH), lambda i: (i, 0)),
        out_shape=jax.ShapeDtypeStruct((b * l, HY_CH), f32),
        compiler_params=_cp(("arbitrary",)),
        name="hy_conv",
    )(u, u, u, hr, hi, skip, cm, sm, cmt, smt)


def _head_sum_matrix(scale=1.0):
    i = np.arange(RWKV_WIDTH) // HEAD_DIM
    return jnp.asarray((i[:, None] == i[None, :]).astype(np.float32) * scale).astype(bf16)


def _dot2(x, m):
    hi = x.astype(bf16)
    lo = (x - hi.astype(f32)).astype(bf16)
    return _dot(hi, m) + _dot(lo, m)


def _rw_prep_body(lo_ref, k_ref, ww_ref, wa_ref, wg_ref, w0_ref, a0_ref, kk_ref_, hs_ref, kk_o, lw_o, a_o, g_o):
    p = LORA_PART
    lo_w = _dot(jnp.tanh(lo_ref[:, 0:p]).astype(bf16), ww_ref[...])
    lo_a = _dot(lo_ref[:, p:2 * p].astype(bf16), wa_ref[...])
    g_o[...] = _dot(jax.nn.sigmoid(lo_ref[:, 2 * p:3 * p]).astype(bf16), wg_ref[...])
    c = RWKV_WIDTH
    for d in range(2):
        zneg = -(w0_ref[d:d + 1, :] + lo_w[:, d * c:(d + 1) * c])
        softplus = jnp.maximum(zneg, 0.0) + jnp.log(1.0 + jnp.exp(-jnp.abs(zneg)))
        w_log = -softplus - 0.5
        lw_o[d] = -jnp.exp(w_log)
        a_o[d] = jax.nn.sigmoid(a0_ref[d:d + 1, :] + lo_a[:, d * c:(d + 1) * c])
    kk = k_ref[...] * kk_ref_[...]
    ss = _dot2(kk * kk, hs_ref[...])
    kk_o[...] = kk / jnp.maximum(jnp.sqrt(ss), 1e-12)


def _rw_prep(u, w_lora, w0, a0, k_k, hsum, tm=256):
    t = u.shape[0]
    c = RWKV_WIDTH
    return pl.pallas_call(
        _rw_prep_body,
        grid=(t // tm,),
        in_specs=[pl.BlockSpec((tm, LORA_IN), lambda i: (i, COL_LORA // LORA_IN)),
                  pl.BlockSpec((tm, c), lambda i: (i, COL_RKV // c + 1)),
                  _full(w_lora[0]), _full(w_lora[1]), _full(w_lora[2]),
                  _full(w0), _full(a0), _full(k_k), _full(hsum)],
        out_specs=[pl.BlockSpec((tm, c), lambda i: (i, 0)),
                   pl.BlockSpec((2, tm, c), lambda i: (0, i, 0)),
                   pl.BlockSpec((2, tm, c), lambda i: (0, i, 0)),
                   pl.BlockSpec((tm, c), lambda i: (i, 0))],
        out_shape=[jax.ShapeDtypeStruct((t, c), f32),
                   jax.ShapeDtypeStruct((2, t, c), f32),
                   jax.ShapeDtypeStruct((2, t, c), f32),
                   jax.ShapeDtypeStruct((t, c), f32)],
        compiler_params=_cp(("arbitrary",)),
        name="rw_prep",
    )(u, u, *w_lora, w0, a0, k_k, hsum)


def _split3(x):
    hi = x.astype(bf16)
    r1 = x - hi.astype(f32)
    mid = r1.astype(bf16)
    lo = (r1 - mid.astype(f32)).astype(bf16)
    return hi, mid, lo


def _rw_dir_operands(d, r_ref, k_ref, v_ref, kk_ref, lw_ref, a_ref, ka):
    cdim = CHUNK
    row = lax.broadcasted_iota(jnp.int32, (2 * cdim, 4 * cdim), 0)
    col = lax.broadcasted_iota(jnp.int32, (2 * cdim, 4 * cdim), 1) % cdim
    lag = (row % cdim) - col if d == 0 else col - (row % cdim)
    mask = lag >= jnp.where(row < cdim, 1, 0)
    row1 = lax.broadcasted_iota(jnp.int32, (cdim, cdim), 0)
    col1 = lax.broadcasted_iota(jnp.int32, (cdim, cdim), 1)
    incl = (col1 <= row1) if d == 0 else (col1 >= row1)
    lw = lw_ref[...]
    cs = sum(_dot(incl.astype(bf16), part) for part in _split3(lw))
    tot = jnp.sum(lw, axis=0, keepdims=True)
    a = a_ref[...]
    kk = kk_ref[...]
    beta = kk * a
    kdir = k_ref[...] * (1.0 + (a - 1.0) * ka)
    p_inv = jnp.exp(-cs)
    p_rest = jnp.exp(tot - cs)
    return dict(
        mask=mask, p_all=jnp.exp(tot),
        kap=(kk * jnp.exp(cs - lw)).astype(bf16), r=(r_ref[...] * jnp.exp(cs)).astype(bf16),
        beta=(beta * p_inv).astype(bf16), kdir=(kdir * p_inv).astype(bf16),
        beta_e=(beta * p_rest).astype(bf16), kdir_e=(kdir * p_rest).astype(bf16),
        v=v_ref[...].astype(bf16))


def _bd(x):
    lane = lax.broadcasted_iota(jnp.int32, x.shape, 1)
    zero = jnp.zeros_like(x)
    return jnp.concatenate([jnp.where(lane < HEAD_DIM, x, zero), jnp.where(lane >= HEAD_DIM, x, zero)], axis=0)


def _rw_scan_body(rb_ref, flag_ref, seq_ref, yb_ref, *refs):
    nj = 2 * SCAN_SEQS
    in_refs = [refs[6 * j:6 * j + 6] for j in range(nj)]
    ka_ref, s0_ref = refs[6 * nj:6 * nj + 2]
    yc_refs = refs[6 * nj + 2:6 * nj + 4]
    yl_refs = refs[6 * nj + 4:6 * nj + 6]
    sf_ref, s_scr = refs[6 * nj + 6:]
    s = pl.program_id(0)
    cdim = CHUNK
    pw = 2 * HEAD_DIM
    n_pairs = N_RWKV_HEADS // 2
    chains = [(j, p) for j in range(nj) for p in range(n_pairs)]
    n = range(len(chains))
    flag = flag_ref[s]

    @pl.when((flag & 1) == 1)
    def _():
        z = jnp.zeros((HEAD_DIM, HEAD_DIM), f32)
        for j, p in chains:
            s_scr[j, p] = jnp.concatenate(
                [jnp.concatenate([s0_ref[j // 2, j % 2, 2 * p], z], axis=1),
                 jnp.concatenate([z, s0_ref[j // 2, j % 2, 2 * p + 1]], axis=1)], axis=0)

    ka = ka_ref[...]
    ops = [_rw_dir_operands(j % 2, *in_refs[j], ka) for j in range(nj)]

    def sl(name, i):
        j, p = chains[i]
        return ops[j][name][:, p * pw:(p + 1) * pw]

    s_old = [s_scr[j, p] for j, p in chains]
    s_b = [x.astype(bf16) for x in s_old]
    lhs = [jnp.concatenate([sl('kap', i), sl('r', i)], axis=0) for i in n]
    rhs = [jnp.concatenate([_bd(sl('beta', i)), _bd(sl('kdir', i))], axis=0) for i in n]
    mm = [jnp.where(ops[chains[i][0]]['mask'], _dot_nt(lhs[i], rhs[i]), 0.0).astype(bf16) for i in n]
    sk = [_dot_nt(lhs[i], s_b[i]) for i in n]
    av = [_dot(mm[i][:, 2 * cdim:], _bd(sl('v', i))) for i in n]
    x = [-mm[i][:cdim, :2 * cdim] for i in n]
    m_rb = [mm[i][cdim:, :2 * cdim] for i in n]
    uu = [sk[i][:cdim] + av[i][:cdim] for i in n]
    n_sq = int(math.log2(cdim))
    for it in range(n_sq):
        uu = [uu[i] + _dot(x[i], _bd(uu[i].astype(bf16))) for i in n]
        if it + 1 < n_sq:
            x = [_dot(x[i], _bd(x[i])).astype(bf16) for i in n]
    u_b = [(-uu[i]).astype(bf16) for i in n]
    y = [sk[i][cdim:] + _dot(m_rb[i], _bd(u_b[i])) + av[i][cdim:] for i in n]
    y_rows = [jnp.concatenate(y[j * n_pairs:(j + 1) * n_pairs], axis=1) for j in range(nj)]

    @pl.when((flag & 4) == 4)
    def _():
        for j in range(nj):
            yc_refs[j % 2][j // 2] = y_rows[j]

    @pl.when((flag & 4) == 0)
    def _():
        for j in range(nj):
            yl_refs[j % 2][j // 2] = y_rows[j]
    row = lax.broadcasted_iota(jnp.int32, (pw, pw), 0)
    col = lax.broadcasted_iota(jnp.int32, (pw, pw), 1)
    same_head = (row // HEAD_DIM) == (col // HEAD_DIM)
    s_new = []
    for i in n:
        j, p = chains[i]
        upd = _dot_tn(jnp.concatenate([u_b[i], sl('v', i)], axis=0),
                      jnp.concatenate([sl('beta_e', i), sl('kdir_e', i)], axis=0))
        s_new.append(s_old[i] * ops[j]['p_all'][:, p * pw:(p + 1) * pw] + jnp.where(same_head, upd, 0.0))
        s_scr[j, p] = s_new[i]

    @pl.when((flag & 2) == 2)
    def _():
        for i in n:
            j, p = chains[i]
            sf_ref[j // 2, j % 2, 2 * p] = s_new[i][:HEAD_DIM, :HEAD_DIM]
            sf_ref[j // 2, j % 2, 2 * p + 1] = s_new[i][HEAD_DIM:, HEAD_DIM:]


def _rw_scan(u, kk, lw, a, k_a, s0, tabs, n_steps, rows):
    c = RWKV_WIDTH
    nseq = s0.shape[0]
    cb = COL_RKV // c
    nj = 2 * SCAN_SEQS

    def stream_specs(j):
        d = j % 2
        return [pl.BlockSpec((CHUNK, c), lambda s, rb, fl, sq, yb: (rb[j, s], cb)),
                pl.BlockSpec((CHUNK, c), lambda s, rb, fl, sq, yb: (rb[j, s], cb + 1)),
                pl.BlockSpec((CHUNK, c), lambda s, rb, fl, sq, yb: (rb[j, s], cb + 2)),
                pl.BlockSpec((CHUNK, c), lambda s, rb, fl, sq, yb: (rb[j, s], 0)),
                pl.BlockSpec((None, CHUNK, c), lambda s, rb, fl, sq, yb: (d, rb[j, s], 0)),
                pl.BlockSpec((None, CHUNK, c), lambda s, rb, fl, sq, yb: (d, rb[j, s], 0))]

    state_spec = pl.BlockSpec((SCAN_SEQS, 2, N_RWKV_HEADS, HEAD_DIM, HEAD_DIM),
                              lambda s, rb, fl, sq, yb: (sq[s], 0, 0, 0, 0))
    in_specs = []
    operands = []
    for j in range(nj):
        in_specs += stream_specs(j)
        operands += [u, u, u, kk, lw, a]
    y_block = (None, SCAN_SEQS, CHUNK, c)
    y_specs = ([pl.BlockSpec(y_block, lambda s, rb, fl, sq, yb, d=d: (yb[0, s], 0, yb[1 + d, s], 0)) for d in range(2)]
               + [pl.BlockSpec(y_block, lambda s, rb, fl, sq, yb, d=d: (yb[3, s], 0, yb[4 + d, s], 0)) for d in range(2)])
    y_shapes = ([jax.ShapeDtypeStruct((rows.b_ctx // SCAN_SEQS, SCAN_SEQS, rows.l_ctx, c), f32)] * 2
                + [jax.ShapeDtypeStruct((rows.b_lat // SCAN_SEQS, SCAN_SEQS, rows.l_lat, c), f32)] * 2)
    grid_spec = pltpu.PrefetchScalarGridSpec(
        num_scalar_prefetch=4,
        grid=(n_steps,),
        in_specs=in_specs + [pl.BlockSpec((1, c), lambda s, rb, fl, sq, yb: (0, 0)), state_spec],
        out_specs=y_specs + [state_spec],
        scratch_shapes=[pltpu.VMEM((nj, N_RWKV_HEADS // 2, 2 * HEAD_DIM, 2 * HEAD_DIM), f32)],
    )
    yc0, yc1, yl0, yl1, sfin = pl.pallas_call(
        _rw_scan_body,
        grid_spec=grid_spec,
        out_shape=y_shapes + [jax.ShapeDtypeStruct((nseq, 2, N_RWKV_HEADS, HEAD_DIM, HEAD_DIM), f32)],
        compiler_params=_cp(("arbitrary",)),
        name="rw_scan",
    )(*tabs, *operands, k_a, s0)
    ys = [(yc.reshape(rows.t_ctx, c), yl.reshape(rows.t_lat, c)) for yc, yl in ((yc0, yl0), (yc1, yl1))]
    return ys, sfin


def _scan_tables(rows):
    rb = [[] for _ in range(2 * SCAN_SEQS)]
    flags, group = [], []
    yb = [[] for _ in range(6)]
    nc_ctx, nc_lat = rows.l_ctx // CHUNK, rows.l_lat // CHUNK
    base = 0
    gidx = 0
    for phase, (b, l) in enumerate(((rows.b_ctx, rows.l_ctx), (rows.b_lat, rows.l_lat))):
        nc = l // CHUNK
        assert b % SCAN_SEQS == 0
        for q in range(b // SCAN_SEQS):
            for c in range(nc):
                for m in range(SCAN_SEQS):
                    seq_base = base + (q * SCAN_SEQS + m) * nc
                    rb[2 * m].append(seq_base + c)
                    rb[2 * m + 1].append(seq_base + nc - 1 - c)
                flags.append((1 if c == 0 else 0) | (2 if c == nc - 1 else 0) | (4 if phase == 0 else 0))
                group.append(gidx)
                if phase == 0:
                    coords = (q, c, nc - 1 - c, 0, 0, nc_lat - 1)
                else:
                    coords = (rows.b_ctx // SCAN_SEQS - 1, nc_ctx - 1, 0, q, c, nc - 1 - c)
                for row, val in zip(yb, coords):
                    row.append(val)
            gidx += 1
        base += b * nc
    return (jnp.asarray(np.array(rb, np.int32)), jnp.asarray(np.array(flags, np.int32)),
            jnp.asarray(np.array(group, np.int32)), jnp.asarray(np.array(yb, np.int32))), len(flags)


def _rw_mix(y, r_ref, k_ref, v_ref, g_ref, rk_ref, gw_ref, gb_ref, hm_ref, hs_ref):
    mu = _dot2(y, hm_ref[...])
    yc = y - mu
    var = _dot2(yc * yc, hm_ref[...])
    yn = yc * lax.rsqrt(var + RWKV_GN_EPS) * gw_ref[...] + gb_ref[...]
    bonus = _dot2(r_ref[...] * k_ref[...] * rk_ref[...], hs_ref[...]) * v_ref[...]
    return (yn + bonus) * g_ref[...]


def _outproj_body(*refs, nx, nctx):
    att_refs, hy_refs, y0_refs, y1_refs = refs[0:2], refs[2:4], refs[4:6], refs[6:8]
    rw_refs = refs[8:17]
    w_ref = refs[17]
    x_refs = refs[18:18 + nx]
    g_ref, nw_ref, o_ref = refs[18 + nx:]
    rw = _rw_mix(_pair_load(y0_refs, nctx) + _pair_load(y1_refs, nctx), *rw_refs)
    o = (_dot(_pair_load(att_refs, nctx).astype(bf16), w_ref[0:ATT_WIDTH, :])
         + _dot(_pair_load(hy_refs, nctx).astype(bf16), w_ref[ATT_WIDTH:ATT_WIDTH + HY_CH, :])
         + _dot(rw.astype(bf16), w_ref[ATT_WIDTH + HY_CH:, :]))
    y = o * lax.rsqrt(jnp.mean(o * o, axis=-1, keepdims=True) + RMS_EPS) * nw_ref[...]
    o_ref[...] = _pair_load(x_refs, nctx) + g_ref[...] * y


def _out_proj(atts, hys, ys, u, g, rw_params, w_out, layer, xs, mod, nw, rows, tm=256):
    t = rows.t
    _, mix, d = w_out.shape
    c = RWKV_WIDTH
    cb = COL_RKV // c
    midx = rows.mod_index(tm)
    body = functools.partial(_outproj_body, nx=len(xs), nctx=rows.t_ctx // tm)
    return pl.pallas_call(
        body,
        grid=(t // tm,),
        in_specs=_pair_specs(atts, rows, tm) + _pair_specs(hys, rows, tm)
        + _pair_specs(ys[0], rows, tm) + _pair_specs(ys[1], rows, tm) + [
            pl.BlockSpec((tm, c), lambda i: (i, cb)),
            pl.BlockSpec((tm, c), lambda i: (i, cb + 1)),
            pl.BlockSpec((tm, c), lambda i: (i, cb + 2)),
            pl.BlockSpec((tm, c), lambda i: (i, 0))] + [_full(p) for p in rw_params] + [
            pl.BlockSpec((None, mix, d), lambda i: (layer, 0, 0))] + _pair_specs(xs, rows, tm) + [
            pl.BlockSpec((None, 1, d), lambda i: (midx(i), 0, 2)),
            pl.BlockSpec((None, 1, d), lambda i: (1, 0, 0))],
        out_specs=pl.BlockSpec((tm, d), lambda i: (i, 0)),
        out_shape=jax.ShapeDtypeStruct((t, d), f32),
        compiler_params=_cp(("arbitrary",)),
        name="out_proj",
    )(*atts, *hys, *ys[0], *ys[1], u, u, u, g, *rw_params, w_out, *xs, mod, nw)


def _mlp_body(x_ref, sh_ref, sc_ref, g_ref, nw2_ref, nw3_ref, w1_ref, w2_ref, *rest, nctx):
    o_refs, (h_scr, acc_scr) = rest[:-2], rest[-2:]
    i = pl.program_id(0)
    j = pl.program_id(1)

    @pl.when(j == 0)
    def _():
        h_scr[...] = _normmod(x_ref[...], nw2_ref[...], sc_ref[...], sh_ref[...]).astype(bf16)
        acc_scr[...] = jnp.zeros_like(acc_scr)

    a = jnp.maximum(_dot(h_scr[...], w1_ref[...]), 0.0)
    acc_scr[...] += _dot((a * a).astype(bf16), w2_ref[...])

    def result():
        f = acc_scr[...]
        y = f * lax.rsqrt(jnp.mean(f * f, axis=-1, keepdims=True) + RMS_EPS) * nw3_ref[...]
        return x_ref[...] + g_ref[...] * y

    last = j == pl.num_programs(1) - 1
    if len(o_refs) == 1:
        @pl.when(last)
        def _():
            o_refs[0][...] = result()
    else:
        @pl.when(last & (i < nctx))
        def _():
            o_refs[0][...] = result()

        @pl.when(last & (i >= nctx))
        def _():
            o_refs[1][...] = result()


def _mlp(x, mod, nw, w1, w2, layer, rows, split_out, tm=512, tf=1024):
    t, d = x.shape
    dff = w1.shape[2]
    midx = rows.mod_index(tm)
    nctx = rows.t_ctx // tm
    if split_out:
        out_specs = [pl.BlockSpec((tm, d), lambda i, j: (jnp.minimum(i, nctx - 1), 0)),
                     pl.BlockSpec((tm, d), lambda i, j: (jnp.maximum(i - nctx, 0), 0))]
        out_shape = [jax.ShapeDtypeStruct((rows.t_ctx, d), f32), jax.ShapeDtypeStruct((rows.t_lat, d), f32)]
    else:
        out_specs = pl.BlockSpec((tm, d), lambda i, j: (i, 0))
        out_shape = jax.ShapeDtypeStruct((t, d), f32)
    return pl.pallas_call(
        functools.partial(_mlp_body, nctx=nctx),
        grid=(t // tm, dff // tf),
        in_specs=[pl.BlockSpec((tm, d), lambda i, j: (i, 0)),
                  pl.BlockSpec((None, 1, d), lambda i, j: (midx(i), 0, 3)),
                  pl.BlockSpec((None, 1, d), lambda i, j: (midx(i), 0, 4)),
                  pl.BlockSpec((None, 1, d), lambda i, j: (midx(i), 0, 5)),
                  pl.BlockSpec((None, 1, d), lambda i, j: (2, 0, 0)),
                  pl.BlockSpec((None, 1, d), lambda i, j: (3, 0, 0)),
                  pl.BlockSpec((None, d, tf), lambda i, j: (layer, 0, j)),
                  pl.BlockSpec((None, tf, d), lambda i, j: (layer, j, 0))],
        out_specs=out_specs,
        out_shape=out_shape,
        scratch_shapes=[pltpu.VMEM((tm, d), bf16), pltpu.VMEM((tm, d), f32)],
        compiler_params=_cp(("arbitrary", "arbitrary")),
        name="mlp",
    )(x, mod, mod, mod, nw, nw, w1, w2)


def _lora_weight(w_up, a_up, g_up):
    c = RWKV_WIDTH

    def two_dir(up):
        r = up.shape[1]
        w = jnp.zeros((LORA_PART, 2 * c), f32)
        return w.at[0:r, 0:c].set(up[0]).at[r:2 * r, c:2 * c].set(up[1]).astype(bf16)
    return two_dir(w_up), two_dir(a_up), g_up.astype(bf16)


def _regroup_in_cols(w):
    kv0 = ATT_WIDTH
    hy0 = kv0 + 2 * ATT_KV
    rw0 = hy0 + 3 * HY_CH
    lo0 = rw0 + 3 * RWKV_WIDTH
    a0 = lo0 + 2 * DECAY_LORA
    g0 = a0 + 2 * AAA_LORA

    def padded(part, width):
        return jnp.pad(part, [(0, 0)] * (part.ndim - 1) + [(0, width - part.shape[-1])])
    return jnp.concatenate([w[..., :kv0], w[..., rw0:lo0], w[..., hy0:rw0], w[..., kv0:hy0],
                            padded(w[..., lo0:a0], LORA_PART), padded(w[..., a0:g0], LORA_PART),
                            padded(w[..., g0:], 2 * LORA_PART)], axis=-1)


def _conv_taps(hy_short_w, rw_short_w):
    def ident(n):
        return jnp.zeros((3, n), f32).at[1].set(1.0)
    return jnp.concatenate([ident(ATT_WIDTH), rw_short_w, hy_short_w, ident(2 * ATT_KV + LORA_IN)], axis=1)


def kernel(x_prompt, x_sample, cache_k, cache_v, state_rwkv, c, c_ctx, ada_w, ada_b, norm_w, w_in, w_out,
           attn_sink, hy_short_w, hy_f1, hy_b1, hy_f2, hy_b2, hy_f3, hy_decay, hy_skip, rw_short_w, rw_w0,
           rw_w_up, rw_a0, rw_a_up, rw_g_up, rw_k_k, rw_k_a, rw_r_k, rw_gn_w, rw_gn_b, mlp_w1, mlp_w2):
    b_ctx, l_ctx, d = x_prompt.shape
    b_lat, l_lat, _ = x_sample.shape
    depth = ada_w.shape[0]
    grid_w = 64
    rows = _Rows(b_ctx, l_ctx, b_lat, l_lat)

    xs = (x_prompt.reshape(rows.t_ctx, d), x_sample.reshape(rows.t_lat, d))
    cond16 = jnp.zeros((16, d), f32).at[0].set(c_ctx).at[1:1 + b_lat].set(c)
    mod_all = _modulation(cond16, ada_w, ada_b[:, None, :])

    w_in_b = _regroup_in_cols(w_in).astype(bf16)
    w_out_b = w_out.astype(bf16)
    w1_b = mlp_w1.astype(bf16)
    w2_b = mlp_w2.astype(bf16)

    cos_t, sin_t = _rope_tables(l_lat, grid_w)
    hsum = _head_sum_matrix()
    hmean = _head_sum_matrix(1.0 / HEAD_DIM)
    scan_tabs, n_steps = _scan_tables(rows)
    dft = {}
    for l in (l_ctx, l_lat):
        cm, sm, cmt, smt = _dft_tables(l)
        feat, t01 = _hy_feat(l)
        dft[l] = dict(cm=cm, sm=sm, cmb=cm.astype(bf16), smb=sm.astype(bf16),
                      cmtb=cmt.astype(bf16), smtb=smt.astype(bf16), feat=feat, t01=t01)

    new_k, new_v, new_s = [], [], []
    for l in range(depth):
        mod = mod_all[l].reshape(16, 1, N_MOD * d)
        nw = norm_w[l].reshape(4, 1, d)
        u = _in_proj(xs, mod, nw, w_in_b, l, _conv_taps(hy_short_w[l], rw_short_w[l]), rows)

        new_k.append(u[:rows.t_ctx, COL_KV:COL_KV + ATT_KV].reshape(b_ctx, l_ctx, N_KV_HEADS, HEAD_DIM))
        new_v.append(u[:rows.t_ctx, COL_KV + ATT_KV:COL_KV + 2 * ATT_KV].reshape(b_ctx, l_ctx, N_KV_HEADS, HEAD_DIM))

        att_c = _attn_ctx(u, attn_sink[l], rows)
        att_l = _attn_lat(u, cache_k[:, l].reshape(b_lat, -1, ATT_KV), cache_v[:, l].reshape(b_lat, -1, ATT_KV),
                          attn_sink[l], cos_t, sin_t, rows)

        f1p = jnp.pad(hy_f1[l], ((0, 128 - HY_EMB), (0, 0)))
        hys = []
        for (bb, ll, rb0) in ((b_ctx, l_ctx, 0), (b_lat, l_lat, rows.t_ctx // l_lat)):
            tb = dft[ll]
            hr, hi = _hy_filter(ll, tb['feat'], tb['t01'], f1p, hy_b1[l][None], hy_f2[l], hy_b2[l][None],
                                hy_f3[l], hy_decay[l][None], tb['cm'], tb['sm'])
            hys.append(_hy_conv(u, hr, hi, hy_skip[l], tb['cmb'], tb['smb'], tb['cmtb'], tb['smtb'],
                                bb, ll, rb0))

        w_lora = _lora_weight(rw_w_up[l], rw_a_up[l], rw_g_up[l])
        kk, lw, a, g = _rw_prep(u, w_lora, rw_w0[l], rw_a0[l], rw_k_k[l][None], hsum)
        s0 = jnp.concatenate([jnp.zeros((b_ctx, 2, N_RWKV_HEADS, HEAD_DIM, HEAD_DIM), f32),
                              state_rwkv[:, l]], axis=0)
        ys, sfin = _rw_scan(u, kk, lw, a, rw_k_a[l][None], s0, scan_tabs, n_steps, rows)
        new_s.append(sfin[:b_ctx])
        rw_params = (rw_r_k[l].reshape(1, RWKV_WIDTH), rw_gn_w[l][None], rw_gn_b[l][None], hmean, hsum)

        x = _out_proj((att_c, att_l), tuple(hys), ys, u, g, rw_params, w_out_b, l, xs, mod, nw, rows)
        x = _mlp(x, mod, nw, w1_b, w2_b, l, rows, split_out=(l == depth - 1))
        xs = (x,)

    y_p, y_s = x
    return (y_p.reshape(b_ctx, l_ctx, d), y_s.reshape(b_lat, l_lat, d),
            jnp.stack(new_k, axis=1), jnp.stack(new_v, axis=1), jnp.stack(new_s, axis=1))
```

```python
import functools
import math

import numpy as np
import jax
import jax.numpy as jnp
from jax import lax
from jax.experimental import pallas as pl
from jax.experimental.pallas import tpu as pltpu

f32 = jnp.float32
bf16 = jnp.bfloat16

HEAD_DIM = 64
N_ATT_HEADS = 12
N_KV_HEADS = 4
ATT_GROUP = 3
ATT_WIDTH = 768
ATT_KV = 256
WINDOW = 128
ROPE_THETA = 10000.0
NEG_INF = -1e30
HY_CH = 512
HY_BANDS = 16
HY_EMB = 33
HY_FFN = 64
HY_MOD_SHIFT = 0.05
RWKV_WIDTH = 768
N_RWKV_HEADS = 12
DECAY_LORA = 96
AAA_LORA = 96
GATE_LORA = 256
RWKV_GN_EPS = 64e-5
N_MOD = 6
RMS_EPS = 1e-6
CHUNK = 64
SCAN_SEQS = 2
CONV_ROWS = 128
LORA_PART = 256
LORA_IN = 4 * LORA_PART
COL_Q = 0
COL_RKV = COL_Q + ATT_WIDTH
COL_HY = COL_RKV + 3 * RWKV_WIDTH
COL_KV = COL_HY + 3 * HY_CH
COL_LORA = COL_KV + 2 * ATT_KV
IN_COLS = COL_LORA + LORA_IN
HI = lax.Precision.HIGHEST
VMEM_LIMIT = 56 * 1024 * 1024


def _cp(sem, vmem=VMEM_LIMIT):
    return pltpu.CompilerParams(dimension_semantics=sem, vmem_limit_bytes=vmem)


def _dot(a, b):
    return jnp.dot(a, b, preferred_element_type=f32)


def _dot_nt(a, b):
    return lax.dot_general(a, b, (((1,), (1,)), ((), ())), preferred_element_type=f32)


def _dot_tn(a, b):
    return lax.dot_general(a, b, (((0,), (0,)), ((), ())), preferred_element_type=f32)


def _dot_hi(a, b):
    return jnp.dot(a, b, preferred_element_type=f32, precision=HI)


def _full(a):
    return pl.BlockSpec(a.shape, lambda *_: (0,) * a.ndim)


def _mod_body(c_ref, w_ref, b_ref, o_ref):
    c = c_ref[...]
    s = (c * jax.nn.sigmoid(c)).astype(bf16)
    o_ref[...] = _dot(s, w_ref[...].astype(bf16)) + b_ref[...]


def _modulation(cond16, ada_w, ada_b):
    depth, d, n = ada_w.shape
    tn = 1024
    return pl.pallas_call(
        _mod_body,
        grid=(depth, n // tn),
        in_specs=[pl.BlockSpec((16, d), lambda l, j: (0, 0)),
                  pl.BlockSpec((None, d, tn), lambda l, j: (l, 0, j)),
                  pl.BlockSpec((None, 1, tn), lambda l, j: (l, 0, j))],
        out_specs=pl.BlockSpec((None, 16, tn), lambda l, j: (l, 0, j)),
        out_shape=jax.ShapeDtypeStruct((depth, 16, n), f32),
        compiler_params=_cp(("arbitrary", "arbitrary")),
        name="modulation",
    )(cond16, ada_w, ada_b)


def _normmod(x, nw, sc, sh):
    y = x * lax.rsqrt(jnp.mean(x * x, axis=-1, keepdims=True) + RMS_EPS) * nw
    return y * (1.0 + sc) + sh


class _Rows:
    def __init__(self, b_ctx, l_ctx, b_lat, l_lat):
        self.b_ctx, self.l_ctx, self.b_lat, self.l_lat = b_ctx, l_ctx, b_lat, l_lat
        self.t_ctx = b_ctx * l_ctx
        self.t_lat = b_lat * l_lat
        self.t = self.t_ctx + self.t_lat

    def mod_index(self, tm):
        nctx = self.t_ctx // tm
        per = self.l_lat // tm

        def f(i):
            return jnp.where(i < nctx, 0, 1 + (i - nctx) // per)
        return f


def _pair_specs(arrs, rows, tm, single_buffer=False):
    w = arrs[0].shape[1]
    if len(arrs) == 1:
        return [pl.BlockSpec((tm, w), lambda i, *_: (i, 0))]
    nctx = rows.t_ctx // tm
    mode = dict(pipeline_mode=pl.Buffered(1)) if single_buffer else {}
    return [pl.BlockSpec((tm, w), lambda i, *_: (jnp.minimum(i, nctx - 1), 0), **mode),
            pl.BlockSpec((tm, w), lambda i, *_: (jnp.maximum(i - nctx, 0), 0), **mode)]


def _pair_load(refs, nctx):
    if len(refs) == 1:
        return refs[0][...]
    return jnp.where(pl.program_id(0) < nctx, refs[0][...], refs[1][...])


def _inproj_body(*refs, nx, nctx, lmask_ctx, lmask_lat):
    x_refs = refs[:nx]
    sh_ref, sc_ref, nw_ref, w_ref, wc_ref, o_ref, h_scr, u_scr = refs[nx:]
    i = pl.program_id(0)

    @pl.when(pl.program_id(1) == 0)
    def _():
        h_scr[...] = _normmod(_pair_load(x_refs, nctx), nw_ref[...], sc_ref[...], sh_ref[...]).astype(bf16)

    u_scr[...] = _dot(h_scr[...], w_ref[...])
    tm = u_scr.shape[0]
    ch = CONV_ROWS
    lmask = jnp.where(i < nctx, lmask_ctx, lmask_lat)
    row = lax.broadcasted_iota(jnp.int32, (ch, 1), 0)
    wc = wc_ref[...]
    for r0 in range(0, tm, ch):
        x = u_scr[r0:r0 + ch, :]
        at_start = (r0 & lmask) == 0
        at_end = ((r0 + ch - 1) & lmask) == lmask
        prev = jnp.where(at_start, 0.0, u_scr[max(r0 - 1, 0):max(r0 - 1, 0) + 1, :])
        nxt = jnp.where(at_end, 0.0, u_scr[min(r0 + ch, tm - 1):min(r0 + ch, tm - 1) + 1, :])
        xm = jnp.where(row == 0, prev, pltpu.roll(x, 1, 0))
        xp = jnp.where(row == ch - 1, nxt, pltpu.roll(x, ch - 1, 0))
        o_ref[r0:r0 + ch, :] = xm * wc[0:1, :] + x * wc[1:2, :] + xp * wc[2:3, :]


def _in_proj(xs, mod, nw, w_in, layer, wconv, rows, tm=1024, tn=1024):
    t = rows.t
    _, d, n = w_in.shape
    for l in (rows.l_ctx, rows.l_lat):
        assert l & (l - 1) == 0 and tm % l == 0 and l % CONV_ROWS == 0
    midx = rows.mod_index(tm)
    body = functools.partial(_inproj_body, nx=len(xs), nctx=rows.t_ctx // tm,
                             lmask_ctx=rows.l_ctx - 1, lmask_lat=rows.l_lat - 1)
    return pl.pallas_call(
        body,
        grid=(t // tm, n // tn),
        in_specs=_pair_specs(xs, rows, tm, single_buffer=True) + [
            pl.BlockSpec((None, 1, d), lambda i, j: (midx(i), 0, 0)),
            pl.BlockSpec((None, 1, d), lambda i, j: (midx(i), 0, 1)),
            pl.BlockSpec((None, 1, d), lambda i, j: (0, 0, 0)),
            pl.BlockSpec((None, d, tn), lambda i, j: (layer, 0, j)),
            pl.BlockSpec((3, tn), lambda i, j: (0, j))],
        out_specs=pl.BlockSpec((tm, tn), lambda i, j: (i, j)),
        out_shape=jax.ShapeDtypeStruct((t, n), f32),
        scratch_shapes=[pltpu.VMEM((tm, d), bf16), pltpu.VMEM((tm, tn), f32)],
        compiler_params=_cp(("arbitrary", "arbitrary")),
        name="in_proj",
    )(*xs, mod, mod, nw, w_in, wconv)


def _attn_ctx_body(sink_ref, q_ref, k_ref, v_ref, o_ref):
    l = q_ref.shape[0]
    scale = HEAD_DIM ** -0.5
    kvh = range(N_KV_HEADS)
    qb = (q_ref[...] * scale).astype(bf16)
    kb = k_ref[...].astype(bf16)
    vb = v_ref[...].astype(bf16)
    hcols = [slice(n * HEAD_DIM, (n + 1) * HEAD_DIM) for n in kvh]
    qs = [jnp.concatenate([qb[:, (n * ATT_GROUP + g) * HEAD_DIM:(n * ATT_GROUP + g + 1) * HEAD_DIM]
                           for g in range(ATT_GROUP)], axis=0) for n in kvh]
    sink_col = [jnp.concatenate([jnp.full((l, 1), sink_ref[n * ATT_GROUP + g], f32) for g in range(ATT_GROUP)],
                                axis=0) for n in kvh]
    s = [_dot_nt(qs[n], kb[:, hcols[n]]) for n in kvh]
    m = [jnp.maximum(jnp.max(s[n], axis=-1, keepdims=True), sink_col[n]) for n in kvh]
    p = [jnp.exp(s[n] - m[n]) for n in kvh]
    den = [jnp.exp(sink_col[n] - m[n]) + jnp.sum(p[n], axis=-1, keepdims=True) for n in kvh]
    o = [_dot(p[n].astype(bf16), vb[:, hcols[n]]) / den[n] for n in kvh]
    for n in kvh:
        for g in range(ATT_GROUP):
            h = n * ATT_GROUP + g
            o_ref[:, h * HEAD_DIM:(h + 1) * HEAD_DIM] = o[n][g * l:(g + 1) * l, :]


def _attn_ctx(u, sink, rows):
    b, l = rows.b_ctx, rows.l_ctx
    return pl.pallas_call(
        _attn_ctx_body,
        grid=(b,),
        in_specs=[pl.BlockSpec(memory_space=pltpu.SMEM),
                  pl.BlockSpec((l, ATT_WIDTH), lambda i: (i, COL_Q // ATT_WIDTH)),
                  pl.BlockSpec((l, ATT_KV), lambda i: (i, COL_KV // ATT_KV)),
                  pl.BlockSpec((l, ATT_KV), lambda i: (i, COL_KV // ATT_KV + 1))],
        out_specs=pl.BlockSpec((l, ATT_WIDTH), lambda i: (i, 0)),
        out_shape=jax.ShapeDtypeStruct((b * l, ATT_WIDTH), f32),
        compiler_params=_cp(("arbitrary",)),
        name="attn_ctx",
    )(sink, u, u, u)


def _rope(x, cos, sin_signed):
    w = x.shape[1]
    lane = lax.broadcasted_iota(jnp.int32, (1, w), 1)
    swapped = jnp.where((lane % 32) < 16, pltpu.roll(x, w - 16, 1), pltpu.roll(x, 16, 1))
    return x * cos + swapped * sin_signed


def _attn_lat_body(sink_ref, q_ref, k_ref, v_ref, kc_ref, vc_ref, cos_ref, sin_ref, o_ref,
                   q_scr, k_scr, v_scr, kc_scr, vc_scr):
    l = q_ref.shape[0]
    blk = WINDOW
    nb = l // blk
    scale = HEAD_DIM ** -0.5
    cos = cos_ref[...]
    sin = sin_ref[...]
    q_scr[...] = (_rope(q_ref[...], cos, sin) * scale).astype(bf16)
    zpad = jnp.zeros((blk, ATT_KV), bf16)
    k_scr[0:blk, :] = zpad
    k_scr[blk + l:, :] = zpad
    v_scr[0:blk, :] = zpad
    v_scr[blk + l:, :] = zpad
    k_scr[blk:blk + l, :] = _rope(k_ref[...], cos[:, :ATT_KV], sin[:, :ATT_KV]).astype(bf16)
    v_scr[blk:blk + l, :] = v_ref[...].astype(bf16)
    r = lax.broadcasted_iota(jnp.int32, (ATT_GROUP * blk, 3 * blk), 0) % blk
    c = lax.broadcasted_iota(jnp.int32, (ATT_GROUP * blk, 3 * blk), 1)
    band = (c - r >= 0) & (c - r <= 2 * WINDOW)
    kvh = range(N_KV_HEADS)
    hcols = [slice(n * HEAD_DIM, (n + 1) * HEAD_DIM) for n in kvh]
    kc_scr[...] = kc_ref[...].astype(bf16)
    vc_scr[...] = vc_ref[...].astype(bf16)
    sink_col = [jnp.concatenate([jnp.full((blk, 1), sink_ref[n * ATT_GROUP + g], f32) for g in range(ATT_GROUP)],
                                axis=0) for n in kvh]
    for i in range(nb):
        kpos = (i - 1) * blk + c
        mask = band & (kpos >= 0) & (kpos < l)
        rows_q = slice(i * blk, (i + 1) * blk)
        rows_k = slice(i * blk, (i + 3) * blk)
        qs = [jnp.concatenate([q_scr[rows_q, (n * ATT_GROUP + g) * HEAD_DIM:(n * ATT_GROUP + g + 1) * HEAD_DIM]
                               for g in range(ATT_GROUP)], axis=0) for n in kvh]
        s_loc = [jnp.where(mask, _dot_nt(qs[n], k_scr[rows_k, hcols[n]]), NEG_INF) for n in kvh]
        s_ctx = [_dot_nt(qs[n], kc_scr[:, hcols[n]]) for n in kvh]
        m = [jnp.maximum(jnp.maximum(jnp.max(s_loc[n], axis=-1, keepdims=True),
                                     jnp.max(s_ctx[n], axis=-1, keepdims=True)), sink_col[n]) for n in kvh]
        p_loc = [jnp.exp(s_loc[n] - m[n]) for n in kvh]
        p_ctx = [jnp.exp(s_ctx[n] - m[n]) for n in kvh]
        den = [jnp.exp(sink_col[n] - m[n]) + jnp.sum(p_loc[n], axis=-1, keepdims=True)
               + jnp.sum(p_ctx[n], axis=-1, keepdims=True) for n in kvh]
        o = [(_dot(p_loc[n].astype(bf16), v_scr[rows_k, hcols[n]])
              + _dot(p_ctx[n].astype(bf16), vc_scr[:, hcols[n]])) / den[n] for n in kvh]
        for n in kvh:
            for g in range(ATT_GROUP):
                h = n * ATT_GROUP + g
                o_ref[rows_q, h * HEAD_DIM:(h + 1) * HEAD_DIM] = o[n][g * blk:(g + 1) * blk, :]


def _attn_lat(u, kc, vc, sink, cos, sin, rows):
    b, l = rows.b_lat, rows.l_lat
    assert rows.t_ctx % l == 0
    rb0 = rows.t_ctx // l
    lc = kc.shape[1]
    return pl.pallas_call(
        _attn_lat_body,
        grid=(b,),
        in_specs=[pl.BlockSpec(memory_space=pltpu.SMEM),
                  pl.BlockSpec((l, ATT_WIDTH), lambda i: (rb0 + i, COL_Q // ATT_WIDTH)),
                  pl.BlockSpec((l, ATT_KV), lambda i: (rb0 + i, COL_KV // ATT_KV)),
                  pl.BlockSpec((l, ATT_KV), lambda i: (rb0 + i, COL_KV // ATT_KV + 1)),
                  pl.BlockSpec((None, lc, ATT_KV), lambda i: (i, 0, 0)),
                  pl.BlockSpec((None, lc, ATT_KV), lambda i: (i, 0, 0)),
                  pl.BlockSpec((l, ATT_WIDTH), lambda i: (0, 0)),
                  pl.BlockSpec((l, ATT_WIDTH), lambda i: (0, 0))],
        out_specs=pl.BlockSpec((l, ATT_WIDTH), lambda i: (i, 0)),
        out_shape=jax.ShapeDtypeStruct((b * l, ATT_WIDTH), f32),
        scratch_shapes=[pltpu.VMEM((l, ATT_WIDTH), bf16),
                        pltpu.VMEM((l + 2 * WINDOW, ATT_KV), bf16),
                        pltpu.VMEM((l + 2 * WINDOW, ATT_KV), bf16),
                        pltpu.VMEM((lc, ATT_KV), bf16),
                        pltpu.VMEM((lc, ATT_KV), bf16)],
        compiler_params=_cp(("arbitrary",)),
        name="attn_lat",
    )(sink, u, u, u, kc, vc, cos, sin)


def _rope_tables(l, grid_w):
    half = HEAD_DIM // 2
    pos = jnp.arange(l)
    rowp = (pos // grid_w).astype(f32)
    colp = (pos % grid_w).astype(f32)
    freqs = ROPE_THETA ** (-jnp.arange(0, half, 2, dtype=f32) / half)
    ar = rowp[:, None] * freqs[None, :]
    ac = colp[:, None] * freqs[None, :]
    cos = jnp.concatenate([jnp.cos(ar), jnp.cos(ar), jnp.cos(ac), jnp.cos(ac)], axis=-1)
    sin = jnp.concatenate([-jnp.sin(ar), jnp.sin(ar), -jnp.sin(ac), jnp.sin(ac)], axis=-1)
    return jnp.tile(cos, (1, N_ATT_HEADS)), jnp.tile(sin, (1, N_ATT_HEADS))


def _dft_tables(l):
    f = 32
    k = jnp.arange(l, dtype=jnp.int32)[:, None]

    def narrow(svals):
        m = ((2 * k + 1) * svals[None, :]) % (4 * l)
        ang = m.astype(f32) * (math.pi / (2 * l))
        return jnp.cos(ang), jnp.sin(ang)

    c1, s1 = narrow(jnp.arange(l // f, dtype=jnp.int32) * f)
    c0, s0 = narrow(jnp.arange(f, dtype=jnp.int32))
    cm = (c1[:, :, None] * c0[:, None, :] - s1[:, :, None] * s0[:, None, :]).reshape(l, l)
    sm = (s1[:, :, None] * c0[:, None, :] + c1[:, :, None] * s0[:, None, :]).reshape(l, l)
    c1t, s1t, c0t, s0t = c1.T, s1.T, c0.T, s0.T
    cmt = (c1t[:, None, :] * c0t[None, :, :] - s1t[:, None, :] * s0t[None, :, :]).reshape(l, l)
    smt = (s1t[:, None, :] * c0t[None, :, :] + c1t[:, None, :] * s0t[None, :, :]).reshape(l, l)
    return cm, sm, cmt, smt


def _hy_feat(l):
    t = jnp.arange(l, dtype=f32)
    t01 = (t / max(l - 1, 1))[:, None]
    bands = jnp.linspace(1e-4, HY_BANDS - 1, HY_BANDS, dtype=f32)
    ang = (2.0 * math.pi / l) * t[:, None] * bands[None, :]
    feat = jnp.concatenate([t01, jnp.cos(ang), -jnp.sin(ang)], axis=-1)
    return jnp.pad(feat, ((0, 0), (0, 128 - HY_EMB))), t01


def _hy_filter_body(feat_ref, t01_ref, f1_ref, b1_ref, f2_ref, b2_ref, f3_ref, dec_ref, cm_ref, sm_ref,
                    hr_ref, hi_ref):
    h = jnp.sin(_dot_hi(feat_ref[...], f1_ref[...]) + b1_ref[...])
    h = jnp.sin(_dot_hi(h, f2_ref[...]) + b2_ref[...])
    h = _dot_hi(h, f3_ref[...])
    h = h * (jnp.exp(-t01_ref[...] * jnp.abs(dec_ref[...])) + HY_MOD_SHIFT)
    l = h.shape[0]
    fwd = h[:, :HY_CH]
    row = lax.broadcasted_iota(jnp.int32, (l, 1), 0)
    bwd = jnp.where(row == 0, 0.0, h[:, HY_CH:])
    hr_ref[...] = _dot_hi(cm_ref[...], fwd + bwd)
    hi_ref[...] = _dot_hi(sm_ref[...], bwd - fwd)


def _hy_filter(l, feat, t01, f1p, b1, f2, b2, f3, dec, cm, sm):
    args = (feat, t01, f1p, b1, f2, b2, f3, dec, cm, sm)
    return pl.pallas_call(
        _hy_filter_body,
        grid=(1,),
        in_specs=[_full(a) for a in args],
        out_specs=[pl.BlockSpec((l, HY_CH), lambda i: (0, 0))] * 2,
        out_shape=[jax.ShapeDtypeStruct((l, HY_CH), f32)] * 2,
        compiler_params=_cp(("arbitrary",)),
        name="hy_filter",
    )(*args)


def _hy_conv_body(x0_ref, x1_ref, v_ref, hr_ref, hi_ref, skip_ref, cm_ref, sm_ref, cmt_ref, smt_ref, o_ref):
    l = v_ref.shape[0]
    z = v_ref[...] * x1_ref[...]
    zb = z.astype(bf16)
    zr = _dot(cm_ref[...], zb)
    zs = _dot(sm_ref[...], zb)
    hr = hr_ref[...]
    hi = hi_ref[...]
    yr = (zr * hr + zs * hi).astype(bf16)
    yi = (zr * hi - zs * hr).astype(bf16)
    y = (_dot(cmt_ref[...], yr) - _dot(smt_ref[...], yi)) * (1.0 / l)
    o_ref[...] = (y + skip_ref[...] * z) * x0_ref[...]


def _hy_conv(u, hr, hi, skip, cm, sm, cmt, smt, b, l, rb0):
    cb = COL_HY // HY_CH
    return pl.pallas_call(
        _hy_conv_body,
        grid=(b,),
        in_specs=[pl.BlockSpec((l, HY_CH), lambda i: (rb0 + i, cb)),
                  pl.BlockSpec((l, HY_CH), lambda i: (rb0 + i, cb + 1)),
                  pl.BlockSpec((l, HY_CH), lambda i: (rb0 + i, cb + 2)),
                  _full(hr), _full(hi), _full(skip), _full(cm), _full(sm), _full(cmt), _full(smt)],
        out_specs=pl.BlockSpec((l, HY_CH), lambda i: (i, 0)),
        out_shape=jax.ShapeDtypeStruct((b * l, HY_CH), f32),
        compiler_params=_cp(("arbitrary",)),
        name="hy_conv",
    )(u, u, u, hr, hi, skip, cm, sm, cmt, smt)


def _head_sum_matrix(scale=1.0):
    i = np.arange(RWKV_WIDTH) // HEAD_DIM
    return jnp.asarray((i[:, None] == i[None, :]).astype(np.float32) * scale).astype(bf16)


def _dot2(x, m):
    hi = x.astype(bf16)
    lo = (x - hi.astype(f32)).astype(bf16)
    return _dot(hi, m) + _dot(lo, m)


def _rw_prep_body(lo_ref, k_ref, ww_ref, wa_ref, wg_ref, w0_ref, a0_ref, kk_ref_, hs_ref, kk_o, lw_o, a_o, g_o):
    p = LORA_PART
    lo_w = _dot(jnp.tanh(lo_ref[:, 0:p]).astype(bf16), ww_ref[...])
    lo_a = _dot(lo_ref[:, p:2 * p].astype(bf16), wa_ref[...])
    g_o[...] = _dot(jax.nn.sigmoid(lo_ref[:, 2 * p:3 * p]).astype(bf16), wg_ref[...])
    c = RWKV_WIDTH
    for d in range(2):
        zneg = -(w0_ref[d:d + 1, :] + lo_w[:, d * c:(d + 1) * c])
        softplus = jnp.maximum(zneg, 0.0) + jnp.log(1.0 + jnp.exp(-jnp.abs(zneg)))
        w_log = -softplus - 0.5
        lw_o[d] = -jnp.exp(w_log)
        a_o[d] = jax.nn.sigmoid(a0_ref[d:d + 1, :] + lo_a[:, d * c:(d + 1) * c])
    kk = k_ref[...] * kk_ref_[...]
    ss = _dot2(kk * kk, hs_ref[...])
    kk_o[...] = kk / jnp.maximum(jnp.sqrt(ss), 1e-12)


def _rw_prep(u, w_lora, w0, a0, k_k, hsum, tm=256):
    t = u.shape[0]
    c = RWKV_WIDTH
    return pl.pallas_call(
        _rw_prep_body,
        grid=(t // tm,),
        in_specs=[pl.BlockSpec((tm, LORA_IN), lambda i: (i, COL_LORA // LORA_IN)),
                  pl.BlockSpec((tm, c), lambda i: (i, COL_RKV // c + 1)),
                  _full(w_lora[0]), _full(w_lora[1]), _full(w_lora[2]),
                  _full(w0), _full(a0), _full(k_k), _full(hsum)],
        out_specs=[pl.BlockSpec((tm, c), lambda i: (i, 0)),
                   pl.BlockSpec((2, tm, c), lambda i: (0, i, 0)),
                   pl.BlockSpec((2, tm, c), lambda i: (0, i, 0)),
                   pl.BlockSpec((tm, c), lambda i: (i, 0))],
        out_shape=[jax.ShapeDtypeStruct((t, c), f32),
                   jax.ShapeDtypeStruct((2, t, c), f32),
                   jax.ShapeDtypeStruct((2, t, c), f32),
                   jax.ShapeDtypeStruct((t, c), f32)],
        compiler_params=_cp(("arbitrary",)),
        name="rw_prep",
    )(u, u, *w_lora, w0, a0, k_k, hsum)


def _split3(x):
    hi = x.astype(bf16)
    r1 = x - hi.astype(f32)
    mid = r1.astype(bf16)
    lo = (r1 - mid.astype(f32)).astype(bf16)
    return hi, mid, lo


def _rw_dir_operands(d, r_ref, k_ref, v_ref, kk_ref, lw_ref, a_ref, ka):
    cdim = CHUNK
    row = lax.broadcasted_iota(jnp.int32, (2 * cdim, 4 * cdim), 0)
    col = lax.broadcasted_iota(jnp.int32, (2 * cdim, 4 * cdim), 1) % cdim
    lag = (row % cdim) - col if d == 0 else col - (row % cdim)
    mask = lag >= jnp.where(row < cdim, 1, 0)
    row1 = lax.broadcasted_iota(jnp.int32, (cdim, cdim), 0)
    col1 = lax.broadcasted_iota(jnp.int32, (cdim, cdim), 1)
    incl = (col1 <= row1) if d == 0 else (col1 >= row1)
    lw = lw_ref[...]
    cs = sum(_dot(incl.astype(bf16), part) for part in _split3(lw))
    tot = jnp.sum(lw, axis=0, keepdims=True)
    a = a_ref[...]
    kk = kk_ref[...]
    beta = kk * a
    kdir = k_ref[...] * (1.0 + (a - 1.0) * ka)
    p_inv = jnp.exp(-cs)
    p_rest = jnp.exp(tot - cs)
    return dict(
        mask=mask, p_all=jnp.exp(tot),
        kap=(kk * jnp.exp(cs - lw)).astype(bf16), r=(r_ref[...] * jnp.exp(cs)).astype(bf16),
        beta=(beta * p_inv).astype(bf16), kdir=(kdir * p_inv).astype(bf16),
        beta_e=(beta * p_rest).astype(bf16), kdir_e=(kdir * p_rest).astype(bf16),
        v=v_ref[...].astype(bf16))


def _bd(x):
    lane = lax.broadcasted_iota(jnp.int32, x.shape, 1)
    zero = jnp.zeros_like(x)
    return jnp.concatenate([jnp.where(lane < HEAD_DIM, x, zero), jnp.where(lane >= HEAD_DIM, x, zero)], axis=0)


def _rw_scan_body(rb_ref, flag_ref, seq_ref, yb_ref, *refs):
    nj = 2 * SCAN_SEQS
    in_refs = [refs[6 * j:6 * j + 6] for j in range(nj)]
    ka_ref, s0_ref = refs[6 * nj:6 * nj + 2]
    y_refs = refs[6 * nj + 2:6 * nj + 4]
    sf_ref, s_scr = refs[6 * nj + 4:]
    s = pl.program_id(0)
    cdim = CHUNK
    pw = 2 * HEAD_DIM
    n_pairs = N_RWKV_HEADS // 2
    chains = [(j, p) for j in range(nj) for p in range(n_pairs)]
    n = range(len(chains))
    flag = flag_ref[s]

    @pl.when((flag & 1) == 1)
    def _():
        z = jnp.zeros((HEAD_DIM, HEAD_DIM), f32)
        for j, p in chains:
            s_scr[j, p] = jnp.concatenate(
                [jnp.concatenate([s0_ref[j // 2, j % 2, 2 * p], z], axis=1),
                 jnp.concatenate([z, s0_ref[j // 2, j % 2, 2 * p + 1]], axis=1)], axis=0)

    ka = ka_ref[...]
    ops = [_rw_dir_operands(j % 2, *in_refs[j], ka) for j in range(nj)]

    def sl(name, i):
        j, p = chains[i]
        return ops[j][name][:, p * pw:(p + 1) * pw]

    s_old = [s_scr[j, p] for j, p in chains]
    s_b = [x.astype(bf16) for x in s_old]
    lhs = [jnp.concatenate([sl('kap', i), sl('r', i)], axis=0) for i in n]
    rhs = [jnp.concatenate([_bd(sl('beta', i)), _bd(sl('kdir', i))], axis=0) for i in n]
    mm = [jnp.where(ops[chains[i][0]]['mask'], _dot_nt(lhs[i], rhs[i]), 0.0).astype(bf16) for i in n]
    sk = [_dot_nt(lhs[i], s_b[i]) for i in n]
    av = [_dot(mm[i][:, 2 * cdim:], _bd(sl('v', i))) for i in n]
    x = [-mm[i][:cdim, :2 * cdim] for i in n]
    m_rb = [mm[i][cdim:, :2 * cdim] for i in n]
    uu = [sk[i][:cdim] + av[i][:cdim] for i in n]
    n_sq = int(math.log2(cdim))
    for it in range(n_sq):
        uu = [uu[i] + _dot(x[i], _bd(uu[i].astype(bf16))) for i in n]
        if it + 1 < n_sq:
            x = [_dot(x[i], _bd(x[i])).astype(bf16) for i in n]
    u_b = [(-uu[i]).astype(bf16) for i in n]
    y = [sk[i][cdim:] + _dot(m_rb[i], _bd(u_b[i])) + av[i][cdim:] for i in n]
    y_rows = [jnp.concatenate(y[j * n_pairs:(j + 1) * n_pairs], axis=1) for j in range(nj)]

    for j in range(nj):
        y_refs[j % 2][j // 2] = y_rows[j]
    row = lax.broadcasted_iota(jnp.int32, (pw, pw), 0)
    col = lax.broadcasted_iota(jnp.int32, (pw, pw), 1)
    same_head = (row // HEAD_DIM) == (col // HEAD_DIM)
    s_new = []
    for i in n:
        j, p = chains[i]
        upd = _dot_tn(jnp.concatenate([u_b[i], sl('v', i)], axis=0),
                      jnp.concatenate([sl('beta_e', i), sl('kdir_e', i)], axis=0))
        s_new.append(s_old[i] * ops[j]['p_all'][:, p * pw:(p + 1) * pw] + jnp.where(same_head, upd, 0.0))
        s_scr[j, p] = s_new[i]

    @pl.when((flag & 2) == 2)
    def _():
        for i in n:
            j, p = chains[i]
            sf_ref[j // 2, j % 2, 2 * p] = s_new[i][:HEAD_DIM, :HEAD_DIM]
            sf_ref[j // 2, j % 2, 2 * p + 1] = s_new[i][HEAD_DIM:, HEAD_DIM:]


def _rw_scan(u, kk, lw, a, k_a, s0, tabs, n_steps, b, l):
    c = RWKV_WIDTH
    cb = COL_RKV // c
    nj = 2 * SCAN_SEQS

    def stream_specs(j):
        d = j % 2
        return [pl.BlockSpec((CHUNK, c), lambda s, rb, fl, sq, yb: (rb[j, s], cb)),
                pl.BlockSpec((CHUNK, c), lambda s, rb, fl, sq, yb: (rb[j, s], cb + 1)),
                pl.BlockSpec((CHUNK, c), lambda s, rb, fl, sq, yb: (rb[j, s], cb + 2)),
                pl.BlockSpec((CHUNK, c), lambda s, rb, fl, sq, yb: (rb[j, s], 0)),
                pl.BlockSpec((None, CHUNK, c), lambda s, rb, fl, sq, yb: (d, rb[j, s], 0)),
                pl.BlockSpec((None, CHUNK, c), lambda s, rb, fl, sq, yb: (d, rb[j, s], 0))]

    state_spec = pl.BlockSpec((SCAN_SEQS, 2, N_RWKV_HEADS, HEAD_DIM, HEAD_DIM),
                              lambda s, rb, fl, sq, yb: (sq[s], 0, 0, 0, 0))
    in_specs = []
    operands = []
    for j in range(nj):
        in_specs += stream_specs(j)
        operands += [u, u, u, kk, lw, a]
    y_specs = [pl.BlockSpec((None, SCAN_SEQS, CHUNK, c),
                            lambda s, rb, fl, sq, yb, d=d: (yb[0, s], 0, yb[1 + d, s], 0)) for d in range(2)]
    grid_spec = pltpu.PrefetchScalarGridSpec(
        num_scalar_prefetch=4,
        grid=(n_steps,),
        in_specs=in_specs + [pl.BlockSpec((1, c), lambda s, rb, fl, sq, yb: (0, 0)), state_spec],
        out_specs=y_specs + [state_spec],
        scratch_shapes=[pltpu.VMEM((nj, N_RWKV_HEADS // 2, 2 * HEAD_DIM, 2 * HEAD_DIM), f32)],
    )
    y0, y1, sfin = pl.pallas_call(
        _rw_scan_body,
        grid_spec=grid_spec,
        out_shape=[jax.ShapeDtypeStruct((b // SCAN_SEQS, SCAN_SEQS, l, c), f32)] * 2
        + [jax.ShapeDtypeStruct((b, 2, N_RWKV_HEADS, HEAD_DIM, HEAD_DIM), f32)],
        compiler_params=_cp(("arbitrary",)),
        name="rw_scan",
    )(*tabs, *operands, k_a, s0)
    return (y0.reshape(b * l, c), y1.reshape(b * l, c)), sfin


def _scan_tables(b, l, row0):
    rb = [[] for _ in range(2 * SCAN_SEQS)]
    flags, group = [], []
    yb = [[] for _ in range(3)]
    nc = l // CHUNK
    base = row0 // CHUNK
    assert b % SCAN_SEQS == 0 and row0 % CHUNK == 0
    for q in range(b // SCAN_SEQS):
        for c in range(nc):
            for m in range(SCAN_SEQS):
                seq_base = base + (q * SCAN_SEQS + m) * nc
                rb[2 * m].append(seq_base + c)
                rb[2 * m + 1].append(seq_base + nc - 1 - c)
            flags.append((1 if c == 0 else 0) | (2 if c == nc - 1 else 0))
            group.append(q)
            for row, val in zip(yb, (q, c, nc - 1 - c)):
                row.append(val)
    return (jnp.asarray(np.array(rb, np.int32)), jnp.asarray(np.array(flags, np.int32)),
            jnp.asarray(np.array(group, np.int32)), jnp.asarray(np.array(yb, np.int32))), len(flags)


def _rw_mix(y, r_ref, k_ref, v_ref, g_ref, rk_ref, gw_ref, gb_ref, hm_ref, hs_ref):
    mu = _dot2(y, hm_ref[...])
    yc = y - mu
    var = _dot2(yc * yc, hm_ref[...])
    yn = yc * lax.rsqrt(var + RWKV_GN_EPS) * gw_ref[...] + gb_ref[...]
    bonus = _dot2(r_ref[...] * k_ref[...] * rk_ref[...], hs_ref[...]) * v_ref[...]
    return (yn + bonus) * g_ref[...]


def _outproj_body(*refs, nx, nctx):
    att_refs, hy_refs, y0_refs, y1_refs = refs[0:2], refs[2:4], refs[4:6], refs[6:8]
    rw_refs = refs[8:17]
    w_ref = refs[17]
    x_refs = refs[18:18 + nx]
    g_ref, nw_ref, o_ref = refs[18 + nx:]
    rw = _rw_mix(_pair_load(y0_refs, nctx) + _pair_load(y1_refs, nctx), *rw_refs)
    o = (_dot(_pair_load(att_refs, nctx).astype(bf16), w_ref[0:ATT_WIDTH, :])
         + _dot(_pair_load(hy_refs, nctx).astype(bf16), w_ref[ATT_WIDTH:ATT_WIDTH + HY_CH, :])
         + _dot(rw.astype(bf16), w_ref[ATT_WIDTH + HY_CH:, :]))
    y = o * lax.rsqrt(jnp.mean(o * o, axis=-1, keepdims=True) + RMS_EPS) * nw_ref[...]
    o_ref[...] = _pair_load(x_refs, nctx) + g_ref[...] * y


def _out_proj(atts, hys, ys, u, g, rw_params, w_out, layer, xs, mod, nw, rows, tm=256):
    t = rows.t
    _, mix, d = w_out.shape
    c = RWKV_WIDTH
    cb = COL_RKV // c
    midx = rows.mod_index(tm)
    body = functools.partial(_outproj_body, nx=len(xs), nctx=rows.t_ctx // tm)
    return pl.pallas_call(
        body,
        grid=(t // tm,),
        in_specs=_pair_specs(atts, rows, tm) + _pair_specs(hys, rows, tm)
        + _pair_specs(ys[0], rows, tm) + _pair_specs(ys[1], rows, tm) + [
            pl.BlockSpec((tm, c), lambda i: (i, cb)),
            pl.BlockSpec((tm, c), lambda i: (i, cb + 1)),
            pl.BlockSpec((tm, c), lambda i: (i, cb + 2)),
            pl.BlockSpec((tm, c), lambda i: (i, 0))] + [_full(p) for p in rw_params] + [
            pl.BlockSpec((None, mix, d), lambda i: (layer, 0, 0))] + _pair_specs(xs, rows, tm) + [
            pl.BlockSpec((None, 1, d), lambda i: (midx(i), 0, 2)),
            pl.BlockSpec((None, 1, d), lambda i: (1, 0, 0))],
        out_specs=pl.BlockSpec((tm, d), lambda i: (i, 0)),
        out_shape=jax.ShapeDtypeStruct((t, d), f32),
        compiler_params=_cp(("arbitrary",)),
        name="out_proj",
    )(*atts, *hys, *ys[0], *ys[1], u, u, u, g, *rw_params, w_out, *xs, mod, nw)


def _mlp_body(x_ref, sh_ref, sc_ref, g_ref, nw2_ref, nw3_ref, w1_ref, w2_ref, *rest, nctx):
    o_refs, (h_scr, acc_scr) = rest[:-2], rest[-2:]
    i = pl.program_id(0)
    j = pl.program_id(1)

    @pl.when(j == 0)
    def _():
        h_scr[...] = _normmod(x_ref[...], nw2_ref[...], sc_ref[...], sh_ref[...]).astype(bf16)
        acc_scr[...] = jnp.zeros_like(acc_scr)

    a = jnp.maximum(_dot(h_scr[...], w1_ref[...]), 0.0)
    acc_scr[...] += _dot((a * a).astype(bf16), w2_ref[...])

    def result():
        f = acc_scr[...]
        y = f * lax.rsqrt(jnp.mean(f * f, axis=-1, keepdims=True) + RMS_EPS) * nw3_ref[...]
        return x_ref[...] + g_ref[...] * y

    last = j == pl.num_programs(1) - 1
    if len(o_refs) == 1:
        @pl.when(last)
        def _():
            o_refs[0][...] = result()
    else:
        @pl.when(last & (i < nctx))
        def _():
            o_refs[0][...] = result()

        @pl.when(last & (i >= nctx))
        def _():
            o_refs[1][...] = result()


def _mlp(x, mod, nw, w1, w2, layer, rows, split_out, tm=512, tf=1024):
    t, d = x.shape
    dff = w1.shape[2]
    midx = rows.mod_index(tm)
    nctx = rows.t_ctx // tm
    if split_out:
        out_specs = [pl.BlockSpec((tm, d), lambda i, j: (jnp.minimum(i, nctx - 1), 0)),
                     pl.BlockSpec((tm, d), lambda i, j: (jnp.maximum(i - nctx, 0), 0))]
        out_shape = [jax.ShapeDtypeStruct((rows.t_ctx, d), f32), jax.ShapeDtypeStruct((rows.t_lat, d), f32)]
    else:
        out_specs = pl.BlockSpec((tm, d), lambda i, j: (i, 0))
        out_shape = jax.ShapeDtypeStruct((t, d), f32)
    return pl.pallas_call(
        functools.partial(_mlp_body, nctx=nctx),
        grid=(t // tm, dff // tf),
        in_specs=[pl.BlockSpec((tm, d), lambda i, j: (i, 0)),
                  pl.BlockSpec((None, 1, d), lambda i, j: (midx(i), 0, 3)),
                  pl.BlockSpec((None, 1, d), lambda i, j: (midx(i), 0, 4)),
                  pl.BlockSpec((None, 1, d), lambda i, j: (midx(i), 0, 5)),
                  pl.BlockSpec((None, 1, d), lambda i, j: (2, 0, 0)),
                  pl.BlockSpec((None, 1, d), lambda i, j: (3, 0, 0)),
                  pl.BlockSpec((None, d, tf), lambda i, j: (layer, 0, j)),
                  pl.BlockSpec((None, tf, d), lambda i, j: (layer, j, 0))],
        out_specs=out_specs,
        out_shape=out_shape,
        scratch_shapes=[pltpu.VMEM((tm, d), bf16), pltpu.VMEM((tm, d), f32)],
        compiler_params=_cp(("arbitrary", "arbitrary")),
        name="mlp",
    )(x, mod, mod, mod, nw, nw, w1, w2)


def _lora_weight(w_up, a_up, g_up):
    c = RWKV_WIDTH

    def two_dir(up):
        r = up.shape[1]
        w = jnp.zeros((LORA_PART, 2 * c), f32)
        return w.at[0:r, 0:c].set(up[0]).at[r:2 * r, c:2 * c].set(up[1]).astype(bf16)
    return two_dir(w_up), two_dir(a_up), g_up.astype(bf16)


def _regroup_in_cols(w):
    kv0 = ATT_WIDTH
    hy0 = kv0 + 2 * ATT_KV
    rw0 = hy0 + 3 * HY_CH
    lo0 = rw0 + 3 * RWKV_WIDTH
    a0 = lo0 + 2 * DECAY_LORA
    g0 = a0 + 2 * AAA_LORA

    def padded(part, width):
        return jnp.pad(part, [(0, 0)] * (part.ndim - 1) + [(0, width - part.shape[-1])])
    return jnp.concatenate([w[..., :kv0], w[..., rw0:lo0], w[..., hy0:rw0], w[..., kv0:hy0],
                            padded(w[..., lo0:a0], LORA_PART), padded(w[..., a0:g0], LORA_PART),
                            padded(w[..., g0:], 2 * LORA_PART)], axis=-1)


def _conv_taps(hy_short_w, rw_short_w):
    def ident(n):
        return jnp.zeros((3, n), f32).at[1].set(1.0)
    return jnp.concatenate([ident(ATT_WIDTH), rw_short_w, hy_short_w, ident(2 * ATT_KV + LORA_IN)], axis=1)


def kernel(x_prompt, x_sample, cache_k, cache_v, state_rwkv, c, c_ctx, ada_w, ada_b, norm_w, w_in, w_out,
           attn_sink, hy_short_w, hy_f1, hy_b1, hy_f2, hy_b2, hy_f3, hy_decay, hy_skip, rw_short_w, rw_w0,
           rw_w_up, rw_a0, rw_a_up, rw_g_up, rw_k_k, rw_k_a, rw_r_k, rw_gn_w, rw_gn_b, mlp_w1, mlp_w2):
    b_ctx, l_ctx, d = x_prompt.shape
    b_lat, l_lat, _ = x_sample.shape
    depth = ada_w.shape[0]
    grid_w = 64
    rows = _Rows(b_ctx, l_ctx, b_lat, l_lat)

    xs = (x_prompt.reshape(rows.t_ctx, d), x_sample.reshape(rows.t_lat, d))
    cond16 = jnp.zeros((16, d), f32).at[0].set(c_ctx).at[1:1 + b_lat].set(c)
    mod_all = _modulation(cond16, ada_w, ada_b[:, None, :])

    w_in_b = _regroup_in_cols(w_in).astype(bf16)
    w_out_b = w_out.astype(bf16)
    w1_b = mlp_w1.astype(bf16)
    w2_b = mlp_w2.astype(bf16)

    cos_t, sin_t = _rope_tables(l_lat, grid_w)
    hsum = _head_sum_matrix()
    hmean = _head_sum_matrix(1.0 / HEAD_DIM)
    scan_ctx = _scan_tables(b_ctx, l_ctx, 0)
    scan_lat = _scan_tables(b_lat, l_lat, rows.t_ctx)
    dft = {}
    for l in (l_ctx, l_lat):
        cm, sm, cmt, smt = _dft_tables(l)
        feat, t01 = _hy_feat(l)
        dft[l] = dict(cm=cm, sm=sm, cmb=cm.astype(bf16), smb=sm.astype(bf16),
                      cmtb=cmt.astype(bf16), smtb=smt.astype(bf16), feat=feat, t01=t01)

    new_k, new_v, new_s = [], [], []
    for l in range(depth):
        mod = mod_all[l].reshape(16, 1, N_MOD * d)
        nw = norm_w[l].reshape(4, 1, d)
        u = _in_proj(xs, mod, nw, w_in_b, l, _conv_taps(hy_short_w[l], rw_short_w[l]), rows)

        new_k.append(u[:rows.t_ctx, COL_KV:COL_KV + ATT_KV].reshape(b_ctx, l_ctx, N_KV_HEADS, HEAD_DIM))
        new_v.append(u[:rows.t_ctx, COL_KV + ATT_KV:COL_KV + 2 * ATT_KV].reshape(b_ctx, l_ctx, N_KV_HEADS, HEAD_DIM))

        att_c = _attn_ctx(u, attn_sink[l], rows)
        att_l = _attn_lat(u, cache_k[:, l].reshape(b_lat, -1, ATT_KV), cache_v[:, l].reshape(b_lat, -1, ATT_KV),
                          attn_sink[l], cos_t, sin_t, rows)

        f1p = jnp.pad(hy_f1[l], ((0, 128 - HY_EMB), (0, 0)))
        hys = []
        for (bb, ll, rb0) in ((b_ctx, l_ctx, 0), (b_lat, l_lat, rows.t_ctx // l_lat)):
            tb = dft[ll]
            hr, hi = _hy_filter(ll, tb['feat'], tb['t01'], f1p, hy_b1[l][None], hy_f2[l], hy_b2[l][None],
                                hy_f3[l], hy_decay[l][None], tb['cm'], tb['sm'])
            hys.append(_hy_conv(u, hr, hi, hy_skip[l], tb['cmb'], tb['smb'], tb['cmtb'], tb['smtb'],
                                bb, ll, rb0))

        w_lora = _lora_weight(rw_w_up[l], rw_a_up[l], rw_g_up[l])
        kk, lw, a, g = _rw_prep(u, w_lora, rw_w0[l], rw_a0[l], rw_k_k[l][None], hsum)
        yc, s_ctx = _rw_scan(u, kk, lw, a, rw_k_a[l][None],
                             jnp.zeros((b_ctx, 2, N_RWKV_HEADS, HEAD_DIM, HEAD_DIM), f32), *scan_ctx, b_ctx, l_ctx)
        yl, _ = _rw_scan(u, kk, lw, a, rw_k_a[l][None], state_rwkv[:, l], *scan_lat, b_lat, l_lat)
        ys = ((yc[0], yl[0]), (yc[1], yl[1]))
        new_s.append(s_ctx)
        rw_params = (rw_r_k[l].reshape(1, RWKV_WIDTH), rw_gn_w[l][None], rw_gn_b[l][None], hmean, hsum)

        x = _out_proj((att_c, att_l), tuple(hys), ys, u, g, rw_params, w_out_b, l, xs, mod, nw, rows)
        x = _mlp(x, mod, nw, w1_b, w2_b, l, rows, split_out=(l == depth - 1))
        xs = (x,)

    y_p, y_s = x
    return (y_p.reshape(b_ctx, l_ctx, d), y_s.reshape(b_lat, l_lat, d),
            jnp.stack(new_k, axis=1), jnp.stack(new_v, axis=1), jnp.stack(new_s, axis=1))
```
